```python
import math
import jax, jax.numpy as jnp
from jax import lax
import numpy as np

D_MODEL = 1024
BATCH = 32
SEQ = 256
DEPTH = 4
DEC_BATCH = 4
DEC_SEQ = 4096
PAST_LEN = 512

GRID_W = 64
N_MIXERS = 3
LAYER_KIND = tuple(i % N_MIXERS for i in range(DEPTH))
N_A = LAYER_KIND.count(0)
N_B = LAYER_KIND.count(1)
N_C = LAYER_KIND.count(2)

E_A = 2 * D_MODEL
POOL_WINDOWS = (2, 4, 8, 16)
N_POOL_GROUPS = len(POOL_WINDOWS)
G_A = E_A // N_POOL_GROUPS
E_B = 2 * D_MODEL
CHUNK = 128
H_B = 8
DH_B = E_B // H_B
H_C = 16
DH_C = 64
E_C = H_C * DH_C
WIN_R_MAX = 8
WIN_C = 16

DEEPNORM_ALPHA = (2 * DEPTH) ** 0.25
DEEPNORM_BETA = (8 * DEPTH) ** -0.25
LN_EPS = 1e-5

kernel_name = "hybrid_pool_gmlp_natten_diffusion_step"


def layer_norm(x, g, b):
    xf = x.astype(jnp.float32)
    mu = jnp.mean(xf, axis=-1, keepdims=True)
    var = jnp.mean(jnp.square(xf - mu), axis=-1, keepdims=True)
    y = (xf - mu) * lax.rsqrt(var + LN_EPS)
    return (y * g.astype(jnp.float32) + b.astype(jnp.float32)).astype(x.dtype)


def adaln(cond, w_mod, b_mod):
    m = jax.nn.silu(cond) @ w_mod + b_mod
    shift, scale, gate = jnp.split(m[:, None, :], 3, axis=-1)
    return shift, scale, gate


def centred_mean_minus_self(x, w):
    L = x.shape[1]
    xf = x.astype(jnp.float32)
    cs = jnp.concatenate([jnp.zeros_like(xf[:, :1]), jnp.cumsum(xf, axis=1)], axis=1)
    t = jnp.arange(L)
    lo = jnp.clip(t - w // 2, 0, L - 1)
    hi = jnp.clip(t + w // 2 - 1, 0, L - 1)
    cnt = (hi - lo + 1).astype(jnp.float32)[None, :, None]
    return ((cs[:, hi + 1] - cs[:, lo]) / cnt - xf).astype(x.dtype)


def pool_mixer(h, w_in, w_grp, scale, w_out):
    z = h @ w_in
    xb, gate = jnp.split(z, 2, axis=-1)
    groups = jnp.split(xb, N_POOL_GROUPS, axis=-1)
    mixed = [centred_mean_minus_self(g, w) @ w_grp[i] for i, (g, w) in enumerate(zip(groups, POOL_WINDOWS))]
    y = jnp.concatenate(mixed, axis=-1) * scale * jax.nn.silu(gate)
    return y @ w_out


def chunk_mlp_mixer(h, w_in, ln_g, ln_b, w_s, b_s, w_out):
    bsz, L, _ = h.shape
    z = h @ w_in
    u, v, gate = jnp.split(z, 3, axis=-1)
    u = jax.nn.gelu(u)
    v = layer_norm(jax.nn.gelu(v), ln_g, ln_b)
    v = v.reshape(bsz, L // CHUNK, CHUNK, H_B, DH_B)
    v = jnp.einsum('hpq,bnqhd->bnphd', w_s, v) + b_s.T[None, None, :, :, None]
    y = u * v.reshape(bsz, L, E_B) * jax.nn.silu(gate)
    return y @ w_out


def na_project(h, w_in):
    z = h @ w_in
    q, k, v, gate = jnp.split(z, 4, axis=-1)
    sh = h.shape[:2] + (H_C, DH_C)
    return q.reshape(sh), k.reshape(sh), v.reshape(sh), gate


def context_attention(h, w_in, w_out):
    bsz, L, _ = h.shape
    q, k, v, gate = na_project(h, w_in)
    logits = jnp.einsum('bqhd,bkhd->bhqk', q, k).astype(jnp.float32) * (DH_C ** -0.5)
    p = jax.nn.softmax(logits, axis=-1).astype(v.dtype)
    o = jnp.einsum('bhqk,bkhd->bqhd', p, v).reshape(bsz, L, E_C)
    return (o * jax.nn.silu(gate)) @ w_out, k, v


def latent_neighbourhood_attention(h, k_ctx, v_ctx, w_in, rpb, w_out):
    bsz, L, _ = h.shape
    rows = L // GRID_W
    kr = min(WIN_R_MAX, rows)
    q, k, v, gate = na_project(h, w_in)
    qg = q.reshape(bsz, rows, GRID_W, H_C, DH_C)
    kg = k.reshape(bsz, rows, GRID_W, H_C, DH_C)
    vg = v.reshape(bsz, rows, GRID_W, H_C, DH_C)
    col = jnp.arange(GRID_W)
    c0 = jnp.clip(col - WIN_C // 2, 0, GRID_W - WIN_C)
    col_in = (col[None, :] >= c0[:, None]) & (col[None, :] < c0[:, None] + WIN_C)
    dc_idx = jnp.clip(col[None, :] - col[:, None] + WIN_C - 1, 0, 2 * WIN_C - 2)
    rpb_cols = rpb.astype(jnp.float32)[:, :, dc_idx]
    mask = jnp.tile(col_in, (1, kr))
    scale = DH_C ** -0.5
    nb = kr * GRID_W

    def row_block(r):
        r0 = jnp.clip(r - kr // 2, 0, rows - kr)
        q_r = lax.dynamic_index_in_dim(qg, r, axis=1, keepdims=False)
        k_band = lax.dynamic_slice_in_dim(kg, r0, kr, axis=1).reshape(bsz, nb, H_C, DH_C)
        v_band = lax.dynamic_slice_in_dim(vg, r0, kr, axis=1).reshape(bsz, nb, H_C, DH_C)
        dr_idx = r0 + jnp.arange(kr) - r + WIN_R_MAX - 1
        bias = jnp.take(rpb_cols, dr_idx, axis=1)
        bias = bias.transpose(0, 2, 1, 3).reshape(H_C, GRID_W, nb)
        lg_band = jnp.einsum('bqhd,bkhd->bhqk', q_r, k_band).astype(jnp.float32) * scale + bias
        lg_band = jnp.where(mask, lg_band, -jnp.inf)
        lg_ctx = jnp.einsum('bqhd,bkhd->bhqk', q_r, k_ctx).astype(jnp.float32) * scale
        p = jax.nn.softmax(jnp.concatenate([lg_band, lg_ctx], axis=-1), axis=-1).astype(v.dtype)
        return (jnp.einsum('bhqk,bkhd->bqhd', p[..., :nb], v_band)
                + jnp.einsum('bhqk,bkhd->bqhd', p[..., nb:], v_ctx))

    o = lax.map(row_block, jnp.arange(rows))
    o = jnp.moveaxis(o, 0, 1).reshape(bsz, L, E_C)
    return (o * jax.nn.silu(gate)) @ w_out


def setup_inputs(seed: int = 0) -> dict:
    key = jax.random.key(seed)
    ks = jax.random.split(key, 24)
    nrm = jax.random.normal
    f32 = jnp.float32
    D = D_MODEL
    na_w_in = nrm(ks[20], (N_C, D, 4 * E_C), f32) * D ** -0.5
    na_w_in = na_w_in.at[..., 2 * E_C:3 * E_C].multiply(DEEPNORM_BETA)
    return {
        'x_prompt': nrm(ks[0], (BATCH, SEQ, D), f32),
        'x_sample': nrm(ks[1], (DEC_BATCH, DEC_SEQ, D), f32),
        'c': nrm(ks[2], (DEC_BATCH, D), f32),
        'cache_k': nrm(ks[3], (DEC_BATCH, N_C, PAST_LEN, H_C, DH_C), f32),
        'cache_v': nrm(ks[4], (DEC_BATCH, N_C, PAST_LEN, H_C, DH_C), f32) * DEEPNORM_BETA,
        'c_ctx': nrm(ks[5], (D,), f32),
        'w_mod': nrm(ks[6], (DEPTH, D, 3 * D), f32) * D ** -0.5,
        'b_mod': nrm(ks[7], (DEPTH, 3 * D), f32) * 0.02,
        'ln_g': 1.0 + 0.02 * nrm(ks[8], (DEPTH, D), f32),
        'ln_b': 0.02 * nrm(ks[9], (DEPTH, D), f32),
        'pool_w_in': nrm(ks[10], (N_A, D, 2 * E_A), f32) * D ** -0.5,
        'pool_w_grp': nrm(ks[11], (N_A, N_POOL_GROUPS, G_A, G_A), f32) * G_A ** -0.5,
        'pool_scale': 1.0 + 0.1 * nrm(ks[12], (N_A, E_A), f32),
        'pool_w_out': nrm(ks[13], (N_A, E_A, D), f32) * (E_A ** -0.5 * DEEPNORM_BETA),
        'sgu_w_in': nrm(ks[14], (N_B, D, 3 * E_B), f32) * D ** -0.5,
        'sgu_ln_g': 1.0 + 0.02 * nrm(ks[15], (N_B, E_B), f32),
        'sgu_ln_b': 0.02 * nrm(ks[16], (N_B, E_B), f32),
        'sgu_w_s': nrm(ks[17], (N_B, H_B, CHUNK, CHUNK), f32) * CHUNK ** -0.5,
        'sgu_b_s': 1.0 + 0.02 * nrm(ks[18], (N_B, H_B, CHUNK), f32),
        'sgu_w_out': nrm(ks[19], (N_B, E_B, D), f32) * (E_B ** -0.5 * DEEPNORM_BETA),
        'na_w_in': na_w_in,
        'na_rpb': 0.1 * nrm(ks[21], (N_C, H_C, 2 * WIN_R_MAX - 1, 2 * WIN_C - 1), f32),
        'na_w_out': nrm(ks[22], (N_C, E_C, D), f32) * (E_C ** -0.5 * DEEPNORM_BETA),
    }


def reference(x_prompt, x_sample, c, cache_k, cache_v, c_ctx, w_mod, b_mod, ln_g, ln_b,
              pool_w_in, pool_w_grp, pool_scale, pool_w_out,
              sgu_w_in, sgu_ln_g, sgu_ln_b, sgu_w_s, sgu_b_s, sgu_w_out,
              na_w_in, na_rpb, na_w_out):
    yp = x_prompt
    ys = x_sample
    ctx_cond = c_ctx[None, :]
    ks_new, vs_new = [], []
    for i in range(DEPTH):
        kind = LAYER_KIND[i]
        j = i // N_MIXERS
        sp, scp, gp = adaln(ctx_cond, w_mod[i], b_mod[i])
        ss, scs, gs = adaln(c, w_mod[i], b_mod[i])
        hp = yp * (1.0 + scp) + sp
        hs = ys * (1.0 + scs) + ss
        if kind == 0:
            op = pool_mixer(hp, pool_w_in[j], pool_w_grp[j], pool_scale[j], pool_w_out[j])
            osm = pool_mixer(hs, pool_w_in[j], pool_w_grp[j], pool_scale[j], pool_w_out[j])
        elif kind == 1:
            op = chunk_mlp_mixer(hp, sgu_w_in[j], sgu_ln_g[j], sgu_ln_b[j], sgu_w_s[j], sgu_b_s[j], sgu_w_out[j])
            osm = chunk_mlp_mixer(hs, sgu_w_in[j], sgu_ln_g[j], sgu_ln_b[j], sgu_w_s[j], sgu_b_s[j], sgu_w_out[j])
        else:
            op, kp, vp = context_attention(hp, na_w_in[j], na_w_out[j])
            ks_new.append(kp)
            vs_new.append(vp)
            osm = latent_neighbourhood_attention(hs, cache_k[:, j], cache_v[:, j], na_w_in[j], na_rpb[j], na_w_out[j])
        yp = layer_norm(DEEPNORM_ALPHA * yp + gp * op, ln_g[i], ln_b[i])
        ys = layer_norm(DEEPNORM_ALPHA * ys + gs * osm, ln_g[i], ln_b[i])
    ctx_k = jnp.stack(ks_new, axis=1)
    ctx_v = jnp.stack(vs_new, axis=1)
    return (yp, ys, ctx_k, ctx_v)
```

```python
import functools

import jax
import jax.numpy as jnp
import numpy as np
from jax import lax
from jax.experimental import pallas as pl
from jax.experimental.pallas import tpu as pltpu

F32 = jnp.float32
BF16 = jnp.bfloat16

DEPTH = 4
N_MIXERS = 3
POOL_WINDOWS = (2, 4, 8, 16)
POOL_HALO = 8
CHUNK = 128
H_B = 8
H_C = 16
DH_C = 64
GRID_W = 64
WIN_R = 8
WIN_C = 16
DEEPNORM_ALPHA = (2 * DEPTH) ** 0.25
LN_EPS = 1e-5

N_COND_ROWS = 8
ROW_TILE = 256
ATT_Q_ROWS = 4
ATT_K_ROWS = 12
LANES = 128
VMEM_LIMIT = 56 * 1024 * 1024


def _const_spec(shape):
    nd = len(shape)
    return pl.BlockSpec(shape, lambda *_: (0,) * nd, pipeline_mode=pl.Buffered(1))


def _params(n_axes):
    return pltpu.CompilerParams(dimension_semantics=("arbitrary",) * n_axes, vmem_limit_bytes=VMEM_LIMIT)


def _mod_spec(layer, d, tiles_per_seq, per_sample):
    base = layer * N_COND_ROWS
    if per_sample:
        return pl.BlockSpec((1, 1, 3 * d), lambda i: (base + 1 + i // tiles_per_seq, 0, 0))
    return pl.BlockSpec((1, 1, 3 * d), lambda i: (base, 0, 0))


def _split_mod(mod_ref, d):
    m = mod_ref[0]
    return m[:, :d], m[:, d:2 * d], m[:, 2 * d:]


def _silu(x):
    return x / (1.0 + jnp.exp(-x))


def _gelu_tanh(x):
    c = np.float32(np.sqrt(2.0 / np.pi))
    return x * (0.5 * (1.0 + jnp.tanh(c * (x + 0.044715 * (x * x * x)))))


def _layer_norm(x, g, b):
    mu = jnp.mean(x, axis=-1, keepdims=True)
    d = x - mu
    var = jnp.mean(d * d, axis=-1, keepdims=True)
    return d * lax.rsqrt(var + LN_EPS) * g + b


def _dot(a, b):
    return jnp.dot(a, b, preferred_element_type=F32)


def _dot_nt(a, b):
    return lax.dot_general(a, b, (((1,), (1,)), ((), ())), preferred_element_type=F32)


def _mods_kernel(cond_ref, w_ref, b_ref, o_ref):
    a = _silu(cond_ref[...]).astype(BF16)
    o_ref[0] = _dot(a, w_ref[0].astype(BF16)) + b_ref[0]


def _mods(conds, w_mod, b_mod):
    depth, d, n = w_mod.shape
    tn = 1024
    out = pl.pallas_call(
        _mods_kernel,
        grid=(depth, n // tn),
        in_specs=[
            pl.BlockSpec((N_COND_ROWS, d), lambda l, j: (0, 0)),
            pl.BlockSpec((1, d, tn), lambda l, j: (l, 0, j)),
            pl.BlockSpec((1, 1, tn), lambda l, j: (l, 0, j)),
        ],
        out_specs=pl.BlockSpec((1, N_COND_ROWS, tn), lambda l, j: (l, 0, j)),
        out_shape=jax.ShapeDtypeStruct((depth, N_COND_ROWS, n), F32),
        compiler_params=_params(2),
        name="adaln_mods",
    )(conds, w_mod, b_mod.reshape(depth, 1, n))
    return out.reshape(depth * N_COND_ROWS, 1, n)


def _pool_kernel(tiles_per_seq, seq_len, x_ref, xp_ref, xn_ref, mod_ref, win_ref, wgrp_ref, psc_ref, wout_ref,
                 g_ref, b_ref, o_ref, e_ref, a1_ref, a2_ref, a3_ref):
    tm, d = x_ref.shape
    e_dim = psc_ref.shape[1]
    grp = e_dim // len(POOL_WINDOWS)
    it = pl.program_id(0) % tiles_per_seq
    shift, scale, gate = _split_mod(mod_ref, d)
    x = x_ref[...]
    h = x * (1.0 + scale) + shift
    hp = xp_ref[...] * (1.0 + scale) + shift
    hn = xn_ref[...] * (1.0 + scale) + shift
    zeros = jnp.zeros((POOL_HALO, d), F32)
    e_ref[0:POOL_HALO] = jnp.where(it != 0, hp, zeros)
    e_ref[POOL_HALO:POOL_HALO + tm] = h
    e_ref[POOL_HALO + tm:2 * POOL_HALO + tm] = jnp.where(it != tiles_per_seq - 1, hn, zeros)
    n = tm + 2 * POOL_HALO
    for ref in (e_ref, a1_ref, a2_ref):
        ref[n:n + POOL_HALO] = zeros
    a1_ref[0:n] = e_ref[0:n] + e_ref[1:n + 1]
    a2_ref[0:n] = a1_ref[0:n] + a1_ref[2:n + 2]
    a3_ref[0:n] = a2_ref[0:n] + a2_ref[4:n + 4]
    win_sums = (a1_ref[7:7 + tm], a2_ref[6:6 + tm], a3_ref[4:4 + tm], a3_ref[0:tm] + a3_ref[8:8 + tm])

    t = it * tm + lax.broadcasted_iota(jnp.int32, (tm, 1), 0)
    mixed = []
    for gi, (w, s) in enumerate(zip(POOL_WINDOWS, win_sums)):
        lo = jnp.maximum(t - w // 2, 0)
        hi = jnp.minimum(t + w // 2 - 1, seq_len - 1)
        cnt = (hi - lo + 1).astype(F32)
        pooled = (s / cnt - h).astype(BF16)
        u = _dot(pooled, win_ref[:, gi * grp:(gi + 1) * grp]).astype(BF16)
        mixed.append(_dot(u, wgrp_ref[gi]))
    mixed = jnp.concatenate(mixed, axis=1)
    gate_pre = _dot(h.astype(BF16), win_ref[:, e_dim:2 * e_dim])
    y = (mixed * psc_ref[...] * _silu(gate_pre)).astype(BF16)
    o = _dot(y, wout_ref[...])
    o_ref[...] = _layer_norm(DEEPNORM_ALPHA * x + gate * o, g_ref[...], b_ref[...])


def _pool_layer(x2d, seq_len, per_sample, mods, layer, w_in, w_grp, p_scale, w_out, ln_g, ln_b):
    t, d = x2d.shape
    tm = ROW_TILE
    tps = seq_len // tm
    e_dim = w_out.shape[0]
    hb = tm // POOL_HALO
    last = t // POOL_HALO - 1
    buf = pltpu.VMEM((tm + 3 * POOL_HALO, d), F32)
    return pl.pallas_call(
        functools.partial(_pool_kernel, tps, seq_len),
        grid=(t // tm,),
        in_specs=[
            pl.BlockSpec((tm, d), lambda i: (i, 0)),
            pl.BlockSpec((POOL_HALO, d), lambda i: (jnp.maximum(i * hb - 1, 0), 0)),
            pl.BlockSpec((POOL_HALO, d), lambda i: (jnp.minimum((i + 1) * hb, last), 0)),
            _mod_spec(layer, d, tps, per_sample),
            _const_spec(w_in.shape),
            _const_spec(w_grp.shape),
            _const_spec((1, e_dim)),
            _const_spec(w_out.shape),
            _const_spec((1, d)),
            _const_spec((1, d)),
        ],
        out_specs=pl.BlockSpec((tm, d), lambda i: (i, 0)),
        out_shape=jax.ShapeDtypeStruct((t, d), F32),
        scratch_shapes=[buf, buf, buf, buf],
        compiler_params=_params(1),
        name="pool_layer",
    )(x2d, x2d, x2d, mods, w_in, w_grp, p_scale.reshape(1, e_dim), w_out, ln_g.reshape(1, d), ln_b.reshape(1, d))


def _sgu_kernel(x_ref, mod_ref, win_ref, lg_ref, lb_ref, ws_ref, bs_ref, wout_ref, g_ref, b_ref, o_ref, y_ref):
    tm, d = x_ref.shape
    e_dim = lg_ref.shape[1]
    dh = e_dim // H_B
    shift, scale, gate = _split_mod(mod_ref, d)
    x = x_ref[...]
    hb = (x * (1.0 + scale) + shift).astype(BF16)
    u = _gelu_tanh(_dot(hb, win_ref[:, 0:e_dim]))
    v = _gelu_tanh(_dot(hb, win_ref[:, e_dim:2 * e_dim]))
    v = _layer_norm(v, lg_ref[...], lb_ref[...]).astype(BF16)
    ug = u * _silu(_dot(hb, win_ref[:, 2 * e_dim:3 * e_dim]))
    for c in range(tm // CHUNK):
        rows = slice(c * CHUNK, (c + 1) * CHUNK)
        for hh in range(H_B):
            cols = slice(hh * dh, (hh + 1) * dh)
            sv = _dot(ws_ref[hh], v[rows, cols]) + bs_ref[:, hh:hh + 1]
            y_ref[rows, cols] = (ug[rows, cols] * sv).astype(BF16)
    o = _dot(y_ref[...], wout_ref[...])
    o_ref[...] = _layer_norm(DEEPNORM_ALPHA * x + gate * o, g_ref[...], b_ref[...])


def _sgu_layer(x2d, seq_len, per_sample, mods, layer, w_in, sln_g, sln_b, w_s, b_s_t, w_out, ln_g, ln_b):
    t, d = x2d.shape
    tm = ROW_TILE
    tps = seq_len // tm
    e_dim = w_out.shape[0]
    return pl.pallas_call(
        _sgu_kernel,
        grid=(t // tm,),
        in_specs=[
            pl.BlockSpec((tm, d), lambda i: (i, 0)),
            _mod_spec(layer, d, tps, per_sample),
            _const_spec(w_in.shape),
            _const_spec((1, e_dim)),
            _const_spec((1, e_dim)),
            _const_spec(w_s.shape),
            _const_spec(b_s_t.shape),
            _const_spec(w_out.shape),
            _const_spec((1, d)),
            _const_spec((1, d)),
        ],
        out_specs=pl.BlockSpec((tm, d), lambda i: (i, 0)),
        out_shape=jax.ShapeDtypeStruct((t, d), F32),
        scratch_shapes=[pltpu.VMEM((tm, e_dim), BF16)],
        compiler_params=_params(1),
        name="sgu_layer",
    )(x2d, mods, w_in, sln_g.reshape(1, e_dim), sln_b.reshape(1, e_dim), w_s, b_s_t, w_out,
      ln_g.reshape(1, d), ln_b.reshape(1, d))


def _na_proj_kernel(emit_f32_kv, x_ref, mod_ref, win_ref, *out_refs):
    tm, d = x_ref.shape
    e = win_ref.shape[1] // 4
    shift, scale, _ = _split_mod(mod_ref, d)
    hb = (x_ref[...] * (1.0 + scale) + shift).astype(BF16)
    q_ref, k_ref, v_ref, sg_ref = out_refs[:4]
    q_ref[...] = (_dot(hb, win_ref[:, 0:e]) * (DH_C ** -0.5)).astype(BF16)
    k = _dot(hb, win_ref[:, e:2 * e])
    v = _dot(hb, win_ref[:, 2 * e:3 * e])
    k_ref[...] = k.astype(BF16)
    v_ref[...] = v.astype(BF16)
    sg_ref[...] = _silu(_dot(hb, win_ref[:, 3 * e:4 * e])).astype(BF16)
    if emit_f32_kv:
        out_refs[4][...] = k
        out_refs[5][...] = v


def _na_proj(x2d, seq_len, per_sample, mods, layer, w_in, emit_f32_kv):
    t, d = x2d.shape
    tm = ROW_TILE
    tps = seq_len // tm
    e = w_in.shape[1] // 4
    row_spec = pl.BlockSpec((tm, e), lambda i: (i, 0))
    n_out = 6 if emit_f32_kv else 4
    return pl.pallas_call(
        functools.partial(_na_proj_kernel, emit_f32_kv),
        grid=(t // tm,),
        in_specs=[
            pl.BlockSpec((tm, d), lambda i: (i, 0)),
            _mod_spec(layer, d, tps, per_sample),
            _const_spec(w_in.shape),
        ],
        out_specs=[row_spec] * n_out,
        out_shape=[jax.ShapeDtypeStruct((t, e), BF16)] * 4 + [jax.ShapeDtypeStruct((t, e), F32)] * (n_out - 4),
        compiler_params=_params(1),
        name="na_proj",
    )(x2d, mods, w_in)


def _lane_is_first_head():
    return lax.broadcasted_iota(jnp.int32, (1, LANES), 1) < DH_C


def _ctx_attn_kernel(q_ref, k_ref, v_ref, o_ref):
    first = _lane_is_first_head()
    for j in range(q_ref.shape[1] // LANES):
        cols = slice(j * LANES, (j + 1) * LANES)
        q, k, v = q_ref[:, cols], k_ref[:, cols], v_ref[:, cols]
        outs = []
        for sel in (first, jnp.logical_not(first)):
            s = _dot_nt(jnp.where(sel, q, jnp.zeros_like(q)), k)
            p = jnp.exp(s - jnp.max(s, axis=-1, keepdims=True))
            l = jnp.sum(p, axis=-1, keepdims=True)
            outs.append(_dot(p.astype(BF16), v) / l)
        o_ref[:, cols] = jnp.where(first, outs[0], outs[1]).astype(BF16)


def _ctx_attn(q, k, v, seq_len):
    t, e = q.shape
    spec = pl.BlockSpec((seq_len, e), lambda b: (b, 0))
    return pl.pallas_call(
        _ctx_attn_kernel,
        grid=(t // seq_len,),
        in_specs=[spec, spec, spec],
        out_specs=spec,
        out_shape=jax.ShapeDtypeStruct((t, e), BF16),
        compiler_params=_params(1),
        name="ctx_attn",
    )(q, k, v)


def _key_row_start(r_blk, rows):
    return jnp.clip(r_blk * ATT_Q_ROWS - WIN_R // 2, 0, rows - ATT_K_ROWS)


def _nbr_attn_kernel(rows, q_ref, k_ref, v_ref, kc_ref, vc_ref, bias_ref, o_ref):
    first = _lane_is_first_head()
    off = pl.multiple_of(_key_row_start(pl.program_id(2), rows) * GRID_W, GRID_W)
    nk = ATT_K_ROWS * GRID_W
    q = q_ref[...]
    kw, vw = k_ref[pl.ds(off, nk), :], v_ref[pl.ds(off, nk), :]
    kc, vc = kc_ref[...], vc_ref[...]
    outs = []
    for hh, sel in enumerate((first, jnp.logical_not(first))):
        qh = jnp.where(sel, q, jnp.zeros_like(q))
        sb = _dot_nt(qh, kw) + bias_ref[hh, 0]
        sc = _dot_nt(qh, kc)
        m = jnp.maximum(jnp.max(sb, axis=-1, keepdims=True), jnp.max(sc, axis=-1, keepdims=True))
        pb = jnp.exp(sb - m)
        pc = jnp.exp(sc - m)
        l = jnp.sum(pb, axis=-1, keepdims=True) + jnp.sum(pc, axis=-1, keepdims=True)
        outs.append((_dot(pb.astype(BF16), vw) + _dot(pc.astype(BF16), vc)) / l)
    o_ref[...] = jnp.where(first, outs[0], outs[1]).astype(BF16)


def _nbr_attn(q, k, v, kc, vc, bias, n_batch):
    t, e = q.shape
    seq = t // n_batch
    rows = seq // GRID_W
    n_blk = rows // ATT_Q_ROWS
    tq = ATT_Q_ROWS * GRID_W
    past = kc.shape[0] // n_batch
    kv_spec = pl.BlockSpec((seq, LANES), lambda b, j, r: (b, j))
    ctx_spec = pl.BlockSpec((past, LANES), lambda b, j, r: (b, j))

    def bias_idx(b, j, r):
        return (j, jnp.where(r == 0, 0, jnp.where(r == n_blk - 1, 2, 1)), 0, 0)

    return pl.pallas_call(
        functools.partial(_nbr_attn_kernel, rows),
        grid=(n_batch, e // LANES, n_blk),
        in_specs=[
            pl.BlockSpec((tq, LANES), lambda b, j, r: (b * n_blk + r, j)),
            kv_spec, kv_spec, ctx_spec, ctx_spec,
            pl.BlockSpec((2, 1) + bias.shape[2:], bias_idx),
        ],
        out_specs=pl.BlockSpec((tq, LANES), lambda b, j, r: (b * n_blk + r, j)),
        out_shape=jax.ShapeDtypeStruct((t, e), BF16),
        compiler_params=_params(3),
        name="nbr_attn",
    )(q, k, v, kc, vc, bias)


def _nbr_bias_tiles(rpb, rows):
    n_h = rpb.shape[0]
    col = np.arange(GRID_W)
    c0 = np.clip(col - WIN_C // 2, 0, GRID_W - WIN_C)
    col_in = (col[None, :] >= c0[:, None]) & (col[None, :] < c0[:, None] + WIN_C)
    dc_idx = np.clip(col[None, :] - col[:, None] + WIN_C - 1, 0, 2 * WIN_C - 2)
    col_bias = jnp.where(col_in[None, None], rpb.astype(F32)[:, :, dc_idx], -jnp.inf)
    masked = jnp.full((n_h, 1, GRID_W, GRID_W), -jnp.inf, F32)
    col_bias = jnp.concatenate([col_bias, masked], axis=1)
    n_dr = 2 * WIN_R - 1
    n_blk = rows // ATT_Q_ROWS
    sel = np.full((3, ATT_Q_ROWS, ATT_K_ROWS), n_dr, np.int32)
    for var, r_blk in enumerate((0, 1, n_blk - 1)):
        ks = int(np.clip(r_blk * ATT_Q_ROWS - WIN_R // 2, 0, rows - ATT_K_ROWS))
        for qi in range(ATT_Q_ROWS):
            r = r_blk * ATT_Q_ROWS + qi
            r0 = int(np.clip(r - WIN_R // 2, 0, rows - WIN_R))
            for ki in range(ATT_K_ROWS):
                kr = ks + ki
                if r0 <= kr < r0 + WIN_R:
                    sel[var, qi, ki] = kr - r + WIN_R - 1
    tiles = jnp.take(col_bias, jnp.asarray(sel.reshape(-1)), axis=1)
    tiles = tiles.reshape(n_h, 3, ATT_Q_ROWS, ATT_K_ROWS, GRID_W, GRID_W).transpose(0, 1, 2, 4, 3, 5)
    return tiles.reshape(n_h, 3, ATT_Q_ROWS * GRID_W, ATT_K_ROWS * GRID_W)


def _na_out_kernel(x_ref, a_ref, sg_ref, mod_ref, wout_ref, g_ref, b_ref, o_ref):
    tm, d = x_ref.shape
    _, _, gate = _split_mod(mod_ref, d)
    y = (a_ref[...].astype(F32) * sg_ref[...].astype(F32)).astype(BF16)
    o = _dot(y, wout_ref[...])
    o_ref[...] = _layer_norm(DEEPNORM_ALPHA * x_ref[...] + gate * o, g_ref[...], b_ref[...])


def _na_out(x2d, attn, sg, seq_len, per_sample, mods, layer, w_out, ln_g, ln_b):
    t, d = x2d.shape
    tm = ROW_TILE
    tps = seq_len // tm
    e = w_out.shape[0]
    return pl.pallas_call(
        _na_out_kernel,
        grid=(t // tm,),
        in_specs=[
            pl.BlockSpec((tm, d), lambda i: (i, 0)),
            pl.BlockSpec((tm, e), lambda i: (i, 0)),
            pl.BlockSpec((tm, e), lambda i: (i, 0)),
            _mod_spec(layer, d, tps, per_sample),
            _const_spec(w_out.shape),
            _const_spec((1, d)),
            _const_spec((1, d)),
        ],
        out_specs=pl.BlockSpec((tm, d), lambda i: (i, 0)),
        out_shape=jax.ShapeDtypeStruct((t, d), F32),
        compiler_params=_params(1),
        name="na_out",
    )(x2d, attn, sg, mods, w_out, ln_g.reshape(1, d), ln_b.reshape(1, d))


def kernel(x_prompt, x_sample, c, cache_k, cache_v, c_ctx, w_mod, b_mod, ln_g, ln_b, pool_w_in, pool_w_grp,
           pool_scale, pool_w_out, sgu_w_in, sgu_ln_g, sgu_ln_b, sgu_w_s, sgu_b_s, sgu_w_out, na_w_in, na_rpb,
           na_w_out):
    n_p, seq_p, d = x_prompt.shape
    n_s, seq_s, _ = x_sample.shape
    assert n_s + 1 <= N_COND_ROWS and seq_p % ROW_TILE == 0 and seq_s % ROW_TILE == 0
    assert seq_s % (GRID_W * ATT_Q_ROWS) == 0 and seq_s // GRID_W >= ATT_K_ROWS
    conds = jnp.zeros((N_COND_ROWS, d), F32).at[0].set(c_ctx).at[1:1 + n_s].set(c)
    mods = _mods(conds, w_mod, b_mod)

    yp = x_prompt.reshape(n_p * seq_p, d)
    ys = x_sample.reshape(n_s * seq_s, d)
    streams = ((seq_p, False), (seq_s, True))
    ctx_k = ctx_v = None
    for i in range(DEPTH):
        kind, j = i % N_MIXERS, i // N_MIXERS
        ys_in = (yp, ys)
        outs = []
        if kind == 0:
            w_in, w_grp, w_out = pool_w_in[j].astype(BF16), pool_w_grp[j].astype(BF16), pool_w_out[j].astype(BF16)
            for x2d, (seq, per_sample) in zip(ys_in, streams):
                outs.append(_pool_layer(x2d, seq, per_sample, mods, i, w_in, w_grp, pool_scale[j], w_out,
                                        ln_g[i], ln_b[i]))
        elif kind == 1:
            w_in, w_s, w_out = sgu_w_in[j].astype(BF16), sgu_w_s[j].astype(BF16), sgu_w_out[j].astype(BF16)
            for x2d, (seq, per_sample) in zip(ys_in, streams):
                outs.append(_sgu_layer(x2d, seq, per_sample, mods, i, w_in, sgu_ln_g[j], sgu_ln_b[j], w_s,
                                       sgu_b_s[j].T, w_out, ln_g[i], ln_b[i]))
        else:
            w_in, w_out = na_w_in[j].astype(BF16), na_w_out[j].astype(BF16)
            e = w_out.shape[0]
            q, k, v, sg, kf, vf = _na_proj(yp, seq_p, False, mods, i, w_in, True)
            ctx_k = kf.reshape(n_p, 1, seq_p, H_C, DH_C)
            ctx_v = vf.reshape(n_p, 1, seq_p, H_C, DH_C)
            attn = _ctx_attn(q, k, v, seq_p)
            outs.append(_na_out(yp, attn, sg, seq_p, False, mods, i, w_out, ln_g[i], ln_b[i]))
            q, k, v, sg = _na_proj(ys, seq_s, True, mods, i, w_in, False)
            past = cache_k.shape[2]
            kc = cache_k[:, j].astype(BF16).reshape(n_s * past, e)
            vc = cache_v[:, j].astype(BF16).reshape(n_s * past, e)
            bias = _nbr_bias_tiles(na_rpb[j], seq_s // GRID_W)
            attn = _nbr_attn(q, k, v, kc, vc, bias, n_s)
            outs.append(_na_out(ys, attn, sg, seq_s, True, mods, i, w_out, ln_g[i], ln_b[i]))
        yp, ys = outs
    return (yp.reshape(n_p, seq_p, d), ys.reshape(n_s, seq_s, d), ctx_k, ctx_v)
```

```python
import functools

import jax
import jax.numpy as jnp
import numpy as np
from jax import lax
from jax.experimental import pallas as pl
from jax.experimental.pallas import tpu as pltpu

F32 = jnp.float32
BF16 = jnp.bfloat16

DEPTH = 4
N_MIXERS = 3
POOL_WINDOWS = (2, 4, 8, 16)
POOL_HALO = 8
CHUNK = 128
H_B = 8
H_C = 16
DH_C = 64
GRID_W = 64
WIN_R = 8
WIN_C = 16
DEEPNORM_ALPHA = (2 * DEPTH) ** 0.25
LN_EPS = 1e-5

N_COND_ROWS = 8
ROW_TILE = 256
ATT_Q_ROWS = 4
ATT_K_ROWS = 12
LANES = 128
VMEM_LIMIT = 56 * 1024 * 1024


def _const_spec(shape):
    nd = len(shape)
    return pl.BlockSpec(shape, lambda *_: (0,) * nd, pipeline_mode=pl.Buffered(1))


def _params(n_axes):
    return pltpu.CompilerParams(dimension_semantics=("arbitrary",) * n_axes, vmem_limit_bytes=VMEM_LIMIT)


def _mod_spec(layer, d, tiles_per_seq, per_sample):
    base = layer * N_COND_ROWS
    if per_sample:
        return pl.BlockSpec((1, 1, 3 * d), lambda i: (base + 1 + i // tiles_per_seq, 0, 0))
    return pl.BlockSpec((1, 1, 3 * d), lambda i: (base, 0, 0))


def _split_mod(mod_ref, d):
    m = mod_ref[0]
    return m[:, :d], m[:, d:2 * d], m[:, 2 * d:]


def _silu(x):
    return x / (1.0 + jnp.exp(-x))


def _gelu_tanh(x):
    c = np.float32(np.sqrt(2.0 / np.pi))
    return x * (0.5 * (1.0 + jnp.tanh(c * (x + 0.044715 * (x * x * x)))))


def _layer_norm(x, g, b):
    mu = jnp.mean(x, axis=-1, keepdims=True)
    d = x - mu
    var = jnp.mean(d * d, axis=-1, keepdims=True)
    return d * lax.rsqrt(var + LN_EPS) * g + b


def _dot(a, b):
    return jnp.dot(a, b, preferred_element_type=F32)


def _dot_nt(a, b):
    return lax.dot_general(a, b, (((1,), (1,)), ((), ())), preferred_element_type=F32)


def _mods_kernel(cond_ref, w_ref, b_ref, o_ref):
    a = _silu(cond_ref[...]).astype(BF16)
    o_ref[0] = _dot(a, w_ref[0].astype(BF16)) + b_ref[0]


def _mods(conds, w_mod, b_mod):
    depth, d, n = w_mod.shape
    tn = 1024
    out = pl.pallas_call(
        _mods_kernel,
        grid=(depth, n // tn),
        in_specs=[
            pl.BlockSpec((N_COND_ROWS, d), lambda l, j: (0, 0)),
            pl.BlockSpec((1, d, tn), lambda l, j: (l, 0, j)),
            pl.BlockSpec((1, 1, tn), lambda l, j: (l, 0, j)),
        ],
        out_specs=pl.BlockSpec((1, N_COND_ROWS, tn), lambda l, j: (l, 0, j)),
        out_shape=jax.ShapeDtypeStruct((depth, N_COND_ROWS, n), F32),
        compiler_params=_params(2),
        name="adaln_mods",
    )(conds, w_mod, b_mod.reshape(depth, 1, n))
    return out.reshape(depth * N_COND_ROWS, 1, n)


def _pool_kernel(tiles_per_seq, seq_len, x_ref, xp_ref, xn_ref, mod_ref, win_ref, wgrp_ref, psc_ref, wout_ref,
                 g_ref, b_ref, o_ref, e_ref, a1_ref, a2_ref, a3_ref):
    tm, d = x_ref.shape
    e_dim = psc_ref.shape[1]
    grp = e_dim // len(POOL_WINDOWS)
    it = pl.program_id(0) % tiles_per_seq
    shift, scale, gate = _split_mod(mod_ref, d)
    x = x_ref[...]
    h = x * (1.0 + scale) + shift
    hp = xp_ref[...] * (1.0 + scale) + shift
    hn = xn_ref[...] * (1.0 + scale) + shift
    zeros = jnp.zeros((POOL_HALO, d), F32)
    e_ref[0:POOL_HALO] = jnp.where(it != 0, hp, zeros)
    e_ref[POOL_HALO:POOL_HALO + tm] = h
    e_ref[POOL_HALO + tm:2 * POOL_HALO + tm] = jnp.where(it != tiles_per_seq - 1, hn, zeros)
    n = tm + 2 * POOL_HALO
    for ref in (e_ref, a1_ref, a2_ref):
        ref[n:n + POOL_HALO] = zeros
    a1_ref[0:n] = e_ref[0:n] + e_ref[1:n + 1]
    a2_ref[0:n] = a1_ref[0:n] + a1_ref[2:n + 2]
    a3_ref[0:n] = a2_ref[0:n] + a2_ref[4:n + 4]
    win_sums = (a1_ref[7:7 + tm], a2_ref[6:6 + tm], a3_ref[4:4 + tm], a3_ref[0:tm] + a3_ref[8:8 + tm])

    t = it * tm + lax.broadcasted_iota(jnp.int32, (tm, 1), 0)
    mixed = []
    for gi, (w, s) in enumerate(zip(POOL_WINDOWS, win_sums)):
        lo = jnp.maximum(t - w // 2, 0)
        hi = jnp.minimum(t + w // 2 - 1, seq_len - 1)
        inv_cnt = 1.0 / (hi - lo + 1).astype(F32)
        pooled = (s * inv_cnt - h).astype(BF16)
        u = _dot(pooled, win_ref[:, gi * grp:(gi + 1) * grp]).astype(BF16)
        mixed.append(_dot(u, wgrp_ref[gi]))
    mixed = jnp.concatenate(mixed, axis=1)
    gate_pre = _dot(h.astype(BF16), win_ref[:, e_dim:2 * e_dim])
    y = (mixed * psc_ref[...] * _silu(gate_pre)).astype(BF16)
    o = _dot(y, wout_ref[...])
    o_ref[...] = _layer_norm(DEEPNORM_ALPHA * x + gate * o, g_ref[...], b_ref[...])


def _pool_layer(x2d, seq_len, per_sample, mods, layer, w_in, w_grp, p_scale, w_out, ln_g, ln_b):
    t, d = x2d.shape
    tm = ROW_TILE
    tps = seq_len // tm
    e_dim = w_out.shape[0]
    hb = tm // POOL_HALO
    last = t // POOL_HALO - 1
    buf = pltpu.VMEM((tm + 3 * POOL_HALO, d), F32)
    return pl.pallas_call(
        functools.partial(_pool_kernel, tps, seq_len),
        grid=(t // tm,),
        in_specs=[
            pl.BlockSpec((tm, d), lambda i: (i, 0)),
            pl.BlockSpec((POOL_HALO, d), lambda i: (jnp.maximum(i * hb - 1, 0), 0)),
            pl.BlockSpec((POOL_HALO, d), lambda i: (jnp.minimum((i + 1) * hb, last), 0)),
            _mod_spec(layer, d, tps, per_sample),
            _const_spec(w_in.shape),
            _const_spec(w_grp.shape),
            _const_spec((1, e_dim)),
            _const_spec(w_out.shape),
            _const_spec((1, d)),
            _const_spec((1, d)),
        ],
        out_specs=pl.BlockSpec((tm, d), lambda i: (i, 0)),
        out_shape=jax.ShapeDtypeStruct((t, d), F32),
        scratch_shapes=[buf, buf, buf, buf],
        compiler_params=_params(1),
        name="pool_layer",
    )(x2d, x2d, x2d, mods, w_in, w_grp, p_scale.reshape(1, e_dim), w_out, ln_g.reshape(1, d), ln_b.reshape(1, d))


def _sgu_kernel(x_ref, mod_ref, win_ref, lg_ref, lb_ref, ws_ref, bs_ref, wout_ref, g_ref, b_ref, o_ref, y_ref):
    tm, d = x_ref.shape
    e_dim = lg_ref.shape[1]
    dh = e_dim // H_B
    shift, scale, gate = _split_mod(mod_ref, d)
    x = x_ref[...]
    hb = (x * (1.0 + scale) + shift).astype(BF16)
    u = _gelu_tanh(_dot(hb, win_ref[:, 0:e_dim]))
    v = _gelu_tanh(_dot(hb, win_ref[:, e_dim:2 * e_dim]))
    v = _layer_norm(v, lg_ref[...], lb_ref[...]).astype(BF16)
    ug = u * _silu(_dot(hb, win_ref[:, 2 * e_dim:3 * e_dim]))
    for c in range(tm // CHUNK):
        rows = slice(c * CHUNK, (c + 1) * CHUNK)
        for hh in range(H_B):
            cols = slice(hh * dh, (hh + 1) * dh)
            sv = _dot(ws_ref[hh], v[rows, cols]) + bs_ref[:, hh:hh + 1]
            y_ref[rows, cols] = (ug[rows, cols] * sv).astype(BF16)
    o = _dot(y_ref[...], wout_ref[...])
    o_ref[...] = _layer_norm(DEEPNORM_ALPHA * x + gate * o, g_ref[...], b_ref[...])


def _sgu_layer(x2d, seq_len, per_sample, mods, layer, w_in, sln_g, sln_b, w_s, b_s_t, w_out, ln_g, ln_b):
    t, d = x2d.shape
    tm = ROW_TILE
    tps = seq_len // tm
    e_dim = w_out.shape[0]
    return pl.pallas_call(
        _sgu_kernel,
        grid=(t // tm,),
        in_specs=[
            pl.BlockSpec((tm, d), lambda i: (i, 0)),
            _mod_spec(layer, d, tps, per_sample),
            _const_spec(w_in.shape),
            _const_spec((1, e_dim)),
            _const_spec((1, e_dim)),
            _const_spec(w_s.shape),
            _const_spec(b_s_t.shape),
            _const_spec(w_out.shape),
            _const_spec((1, d)),
            _const_spec((1, d)),
        ],
        out_specs=pl.BlockSpec((tm, d), lambda i: (i, 0)),
        out_shape=jax.ShapeDtypeStruct((t, d), F32),
        scratch_shapes=[pltpu.VMEM((tm, e_dim), BF16)],
        compiler_params=_params(1),
        name="sgu_layer",
    )(x2d, mods, w_in, sln_g.reshape(1, e_dim), sln_b.reshape(1, e_dim), w_s, b_s_t, w_out,
      ln_g.reshape(1, d), ln_b.reshape(1, d))


def _na_proj_kernel(emit_f32_kv, x_ref, mod_ref, win_ref, *out_refs):
    tm, d = x_ref.shape
    e = win_ref.shape[1] // 4
    shift, scale, _ = _split_mod(mod_ref, d)
    hb = (x_ref[...] * (1.0 + scale) + shift).astype(BF16)
    q_ref, k_ref, v_ref, sg_ref = out_refs[:4]
    q_ref[...] = (_dot(hb, win_ref[:, 0:e]) * (DH_C ** -0.5)).astype(BF16)
    k = _dot(hb, win_ref[:, e:2 * e])
    v = _dot(hb, win_ref[:, 2 * e:3 * e])
    k_ref[...] = k.astype(BF16)
    v_ref[...] = v.astype(BF16)
    sg_ref[...] = _silu(_dot(hb, win_ref[:, 3 * e:4 * e])).astype(BF16)
    if emit_f32_kv:
        for kv, ref in ((k, out_refs[4]), (v, out_refs[5])):
            for h in range(H_C):
                ref[pl.ds(h, tm, stride=H_C), :] = kv[:, h * DH_C:(h + 1) * DH_C]


def _na_proj(x2d, seq_len, per_sample, mods, layer, w_in, emit_f32_kv):
    t, d = x2d.shape
    tm = ROW_TILE
    tps = seq_len // tm
    e = w_in.shape[1] // 4
    row_spec = pl.BlockSpec((tm, e), lambda i: (i, 0))
    out_specs = [row_spec] * 4
    out_shape = [jax.ShapeDtypeStruct((t, e), BF16)] * 4
    if emit_f32_kv:
        out_specs += [pl.BlockSpec((tm * H_C, DH_C), lambda i: (i, 0))] * 2
        out_shape += [jax.ShapeDtypeStruct((t * H_C, DH_C), F32)] * 2
    return pl.pallas_call(
        functools.partial(_na_proj_kernel, emit_f32_kv),
        grid=(t // tm,),
        in_specs=[
            pl.BlockSpec((tm, d), lambda i: (i, 0)),
            _mod_spec(layer, d, tps, per_sample),
            _const_spec(w_in.shape),
        ],
        out_specs=out_specs,
        out_shape=out_shape,
        compiler_params=_params(1),
        name="na_proj",
    )(x2d, mods, w_in)


def _lane_is_first_head():
    return lax.broadcasted_iota(jnp.int32, (1, LANES), 1) < DH_C


def _ctx_attn_kernel(q_ref, k_ref, v_ref, o_ref):
    first = _lane_is_first_head()
    for j in range(q_ref.shape[1] // LANES):
        cols = slice(j * LANES, (j + 1) * LANES)
        q, k, v = q_ref[:, cols], k_ref[:, cols], v_ref[:, cols]
        outs = []
        for sel in (first, jnp.logical_not(first)):
            s = _dot_nt(jnp.where(sel, q, jnp.zeros_like(q)), k)
            p = jnp.exp(s - jnp.max(s, axis=-1, keepdims=True))
            l = jnp.sum(p, axis=-1, keepdims=True)
            outs.append(_dot(p.astype(BF16), v) / l)
        o_ref[:, cols] = jnp.where(first, outs[0], outs[1]).astype(BF16)


def _ctx_attn(q, k, v, seq_len):
    t, e = q.shape
    spec = pl.BlockSpec((seq_len, e), lambda b: (b, 0))
    return pl.pallas_call(
        _ctx_attn_kernel,
        grid=(t // seq_len,),
        in_specs=[spec, spec, spec],
        out_specs=spec,
        out_shape=jax.ShapeDtypeStruct((t, e), BF16),
        compiler_params=_params(1),
        name="ctx_attn",
    )(q, k, v)


def _key_row_start(r_blk, rows):
    return jnp.clip(r_blk * ATT_Q_ROWS - WIN_R // 2, 0, rows - ATT_K_ROWS)


def _nbr_bias_build(rows, rpb_ref, bias_ref):
    n_blk = rows // ATT_Q_ROWS
    qc = lax.broadcasted_iota(jnp.int32, (GRID_W, LANES), 0)
    lane = lax.broadcasted_iota(jnp.int32, (GRID_W, LANES), 1)
    kc = lane & (GRID_W - 1)
    c0 = jnp.clip(qc - WIN_C // 2, 0, GRID_W - WIN_C)
    col_in = (kc >= c0) & (kc < c0 + WIN_C)
    first = lane < GRID_W
    neg = jnp.full((GRID_W, LANES), -jnp.inf, F32)
    for hh in range(2):
        lo, hi = [], []
        for dr in range(2 * WIN_R - 1):
            row = jnp.broadcast_to(rpb_ref[hh, dr:dr + 1, :], (GRID_W, LANES))
            lo.append(pltpu.roll(row, LANES - (WIN_C - 1), 1, stride=1, stride_axis=0))
            hi.append(pltpu.roll(row, GRID_W - (WIN_C - 1), 1, stride=1, stride_axis=0))
        for var, r_blk in enumerate((0, 1, n_blk - 1)):
            ks = min(max(r_blk * ATT_Q_ROWS - WIN_R // 2, 0), rows - ATT_K_ROWS)
            for qi in range(ATT_Q_ROWS):
                r = r_blk * ATT_Q_ROWS + qi
                r0 = min(max(r - WIN_R // 2, 0), rows - WIN_R)
                for kp in range(ATT_K_ROWS // 2):
                    halves = []
                    for half, src in enumerate((lo, hi)):
                        kr = ks + 2 * kp + half
                        halves.append(src[kr - r + WIN_R - 1] if r0 <= kr < r0 + WIN_R else neg)
                    blk = jnp.where(col_in, jnp.where(first, halves[0], halves[1]), neg)
                    bias_ref[hh, var, qi * GRID_W:(qi + 1) * GRID_W, kp * LANES:(kp + 1) * LANES] = blk


def _nbr_attn_kernel(rows, rpb_ref, q_ref, k_ref, v_ref, kc_ref, vc_ref, o_ref, bias_ref, vx_ref, s_ref, p_ref,
                     oa_ref):
    n_blk = rows // ATT_Q_ROWS
    tq = ATT_Q_ROWS * GRID_W
    nk = ATT_K_ROWS * GRID_W
    seq = q_ref.shape[0]

    @pl.when(pl.program_id(1) == 0)
    def _():
        _nbr_bias_build(rows, rpb_ref, bias_ref)

    first = _lane_is_first_head()
    head_lanes = (first, jnp.logical_not(first))
    for hh in range(2):
        vx_ref[hh, 0:seq] = jnp.where(head_lanes[hh], v_ref[...], jnp.ones(v_ref.shape, BF16))
        vx_ref[hh, seq:] = jnp.where(head_lanes[hh], vc_ref[...], jnp.ones(vc_ref.shape, BF16))

    def q_rows(r):
        return pl.ds(pl.multiple_of(r * tq, tq), tq)

    def k_start(r):
        return pl.multiple_of(_key_row_start(r, rows) * GRID_W, GRID_W)

    def logits(r, hh):
        q = q_ref[q_rows(r), :]
        qh = jnp.where(head_lanes[hh], q, jnp.zeros_like(q))
        var = jnp.where(r == 0, 0, jnp.where(r == n_blk - 1, 2, 1))
        s_ref[hh, :, 0:nk] = _dot_nt(qh, k_ref[pl.ds(k_start(r), nk), :]) + bias_ref[hh, var]
        s_ref[hh, :, nk:] = _dot_nt(qh, kc_ref[...])

    def softmax(hh):
        s = s_ref[hh]
        p_ref[hh] = jnp.exp((s - jnp.max(s, axis=-1, keepdims=True)).astype(BF16))

    def weighted_values(r, hh):
        return (_dot(p_ref[hh, :, 0:nk], vx_ref[hh, pl.ds(k_start(r), nk), :])
                + _dot(p_ref[hh, :, nk:], vx_ref[hh, seq:, :]))

    def emit(r, o_second):
        o_first = oa_ref[...]
        num = jnp.where(first, o_first, o_second)
        den = pltpu.roll(jnp.where(first, o_second, o_first), DH_C, 1)
        o_ref[q_rows(r), :] = (num / den).astype(BF16)

    p_ref[1] = jnp.ones(p_ref.shape[1:], BF16)
    oa_ref[...] = jnp.ones(oa_ref.shape, F32)
    logits(0, 0)

    def block(r, carry):
        prev = jnp.maximum(r - 1, 0)
        emit(prev, weighted_values(prev, 1))
        logits(r, 1)
        softmax(0)
        oa_ref[...] = weighted_values(r, 0)
        logits(jnp.minimum(r + 1, n_blk - 1), 0)
        softmax(1)
        return carry

    lax.fori_loop(0, n_blk, block, 0)
    emit(n_blk - 1, weighted_values(n_blk - 1, 1))


def _nbr_attn(q, k, v, kc, vc, rpb, n_batch):
    t, e = q.shape
    seq = t // n_batch
    rows = seq // GRID_W
    past = kc.shape[0] // n_batch
    tq, nk = ATT_Q_ROWS * GRID_W, ATT_K_ROWS * GRID_W
    n_dr, n_dc = rpb.shape[1:]
    rpb_pad = jnp.pad(rpb.astype(F32), ((0, 0), (0, 2 * WIN_R - n_dr), (0, LANES - n_dc)))
    tok_spec = pl.BlockSpec((seq, LANES), lambda j, b: (b, j))
    ctx_spec = pl.BlockSpec((past, LANES), lambda j, b: (b, j))
    return pl.pallas_call(
        functools.partial(_nbr_attn_kernel, rows),
        grid=(e // LANES, n_batch),
        in_specs=[pl.BlockSpec((2, 2 * WIN_R, LANES), lambda j, b: (j, 0, 0)),
                  tok_spec, tok_spec, tok_spec, ctx_spec, ctx_spec],
        out_specs=tok_spec,
        out_shape=jax.ShapeDtypeStruct((t, e), BF16),
        scratch_shapes=[
            pltpu.VMEM((2, 3, tq, nk), F32),
            pltpu.VMEM((2, seq + past, LANES), BF16),
            pltpu.VMEM((2, tq, nk + past), F32),
            pltpu.VMEM((2, tq, nk + past), BF16),
            pltpu.VMEM((tq, LANES), F32),
        ],
        compiler_params=_params(2),
        name="nbr_attn",
    )(rpb_pad, q, k, v, kc, vc)


def _na_out_kernel(x_ref, a_ref, sg_ref, mod_ref, wout_ref, g_ref, b_ref, o_ref):
    tm, d = x_ref.shape
    _, _, gate = _split_mod(mod_ref, d)
    y = (a_ref[...].astype(F32) * sg_ref[...].astype(F32)).astype(BF16)
    o = _dot(y, wout_ref[...])
    o_ref[...] = _layer_norm(DEEPNORM_ALPHA * x_ref[...] + gate * o, g_ref[...], b_ref[...])


def _na_out(x2d, attn, sg, seq_len, per_sample, mods, layer, w_out, ln_g, ln_b):
    t, d = x2d.shape
    tm = ROW_TILE
    tps = seq_len // tm
    e = w_out.shape[0]
    return pl.pallas_call(
        _na_out_kernel,
        grid=(t // tm,),
        in_specs=[
            pl.BlockSpec((tm, d), lambda i: (i, 0)),
            pl.BlockSpec((tm, e), lambda i: (i, 0)),
            pl.BlockSpec((tm, e), lambda i: (i, 0)),
            _mod_spec(layer, d, tps, per_sample),
            _const_spec(w_out.shape),
            _const_spec((1, d)),
            _const_spec((1, d)),
        ],
        out_specs=pl.BlockSpec((tm, d), lambda i: (i, 0)),
        out_shape=jax.ShapeDtypeStruct((t, d), F32),
        compiler_params=_params(1),
        name="na_out",
    )(x2d, attn, sg, mods, w_out, ln_g.reshape(1, d), ln_b.reshape(1, d))


def kernel(x_prompt, x_sample, c, cache_k, cache_v, c_ctx, w_mod, b_mod, ln_g, ln_b, pool_w_in, pool_w_grp,
           pool_scale, pool_w_out, sgu_w_in, sgu_ln_g, sgu_ln_b, sgu_w_s, sgu_b_s, sgu_w_out, na_w_in, na_rpb,
           na_w_out):
    n_p, seq_p, d = x_prompt.shape
    n_s, seq_s, _ = x_sample.shape
    assert n_s + 1 <= N_COND_ROWS and seq_p % ROW_TILE == 0 and seq_s % ROW_TILE == 0
    assert seq_s % (GRID_W * ATT_Q_ROWS) == 0 and seq_s // GRID_W >= ATT_K_ROWS
    conds = jnp.zeros((N_COND_ROWS, d), F32).at[0].set(c_ctx).at[1:1 + n_s].set(c)
    mods = _mods(conds, w_mod, b_mod)

    yp = x_prompt.reshape(n_p * seq_p, d)
    ys = x_sample.reshape(n_s * seq_s, d)
    streams = ((seq_p, False), (seq_s, True))
    ctx_k = ctx_v = None
    for i in range(DEPTH):
        kind, j = i % N_MIXERS, i // N_MIXERS
        ys_in = (yp, ys)
        outs = []
        if kind == 0:
            w_in, w_grp, w_out = pool_w_in[j].astype(BF16), pool_w_grp[j].astype(BF16), pool_w_out[j].astype(BF16)
            for x2d, (seq, per_sample) in zip(ys_in, streams):
                outs.append(_pool_layer(x2d, seq, per_sample, mods, i, w_in, w_grp, pool_scale[j], w_out,
                                        ln_g[i], ln_b[i]))
        elif kind == 1:
            w_in, w_s, w_out = sgu_w_in[j].astype(BF16), sgu_w_s[j].astype(BF16), sgu_w_out[j].astype(BF16)
            for x2d, (seq, per_sample) in zip(ys_in, streams):
                outs.append(_sgu_layer(x2d, seq, per_sample, mods, i, w_in, sgu_ln_g[j], sgu_ln_b[j], w_s,
                                       sgu_b_s[j].T, w_out, ln_g[i], ln_b[i]))
        else:
            w_in, w_out = na_w_in[j].astype(BF16), na_w_out[j].astype(BF16)
            e = w_out.shape[0]
            q, k, v, sg, kf, vf = _na_proj(yp, seq_p, False, mods, i, w_in, True)
            ctx_k = kf.reshape(n_p, 1, seq_p, H_C, DH_C)
            ctx_v = vf.reshape(n_p, 1, seq_p, H_C, DH_C)
            attn = _ctx_attn(q, k, v, seq_p)
            outs.append(_na_out(yp, attn, sg, seq_p, False, mods, i, w_out, ln_g[i], ln_b[i]))
            q, k, v, sg = _na_proj(ys, seq_s, True, mods, i, w_in, False)
            past = cache_k.shape[2]
            kc = cache_k[:, j].astype(BF16).reshape(n_s * past, e)
            vc = cache_v[:, j].astype(BF16).reshape(n_s * past, e)
            attn = _nbr_attn(q, k, v, kc, vc, na_rpb[j], n_s)
            outs.append(_na_out(ys, attn, sg, seq_s, True, mods, i, w_out, ln_g[i], ln_b[i]))
        yp, ys = outs
    return (yp.reshape(n_p, seq_p, d), ys.reshape(n_s, seq_s, d), ctx_k, ctx_v)
```

```python
import functools

import jax
import jax.numpy as jnp
import numpy as np
from jax import lax
from jax.experimental import pallas as pl
from jax.experimental.pallas import tpu as pltpu

F32 = jnp.float32
BF16 = jnp.bfloat16

DEPTH = 4
N_MIXERS = 3
POOL_WINDOWS = (2, 4, 8, 16)
POOL_HALO = 16
POOL_BLOCK = 128
CHUNK = 128
H_B = 8
H_C = 16
DH_C = 64
GRID_W = 64
WIN_R = 8
WIN_C = 16
DEEPNORM_ALPHA = (2 * DEPTH) ** 0.25
LN_EPS = 1e-5
LOG2_E = float(np.log2(np.e))

N_COND_ROWS = 8
ROW_TILE = 256
ATT_Q_ROWS = 4
ATT_K_ROWS = 12
LANES = 128
VMEM_LIMIT = 56 * 1024 * 1024


def _const_spec(shape):
    nd = len(shape)
    return pl.BlockSpec(shape, lambda *_: (0,) * nd, pipeline_mode=pl.Buffered(1))


def _params(n_axes):
    return pltpu.CompilerParams(dimension_semantics=("arbitrary",) * n_axes, vmem_limit_bytes=VMEM_LIMIT)


def _mod_spec(layer, d, tiles_per_seq, per_sample):
    base = layer * N_COND_ROWS
    if per_sample:
        return pl.BlockSpec((1, 1, 3 * d), lambda i: (base + 1 + i // tiles_per_seq, 0, 0))
    return pl.BlockSpec((1, 1, 3 * d), lambda i: (base, 0, 0))


def _split_mod(mod_ref, d):
    m = mod_ref[0]
    return m[:, :d], m[:, d:2 * d], m[:, 2 * d:]


def _silu(x):
    return x / (1.0 + jnp.exp(-x))


def _gelu_tanh(x):
    c = np.float32(np.sqrt(2.0 / np.pi))
    return x * (0.5 * (1.0 + jnp.tanh(c * (x + 0.044715 * (x * x * x)))))


def _layer_norm(x, g, b):
    mu = jnp.mean(x, axis=-1, keepdims=True)
    d = x - mu
    var = jnp.mean(d * d, axis=-1, keepdims=True)
    return d * lax.rsqrt(var + LN_EPS) * g + b


def _dot(a, b):
    return jnp.dot(a, b, preferred_element_type=F32)


def _dot_nt(a, b):
    return lax.dot_general(a, b, (((1,), (1,)), ((), ())), preferred_element_type=F32)


def _mods_kernel(cond_ref, w_ref, b_ref, o_ref):
    a = _silu(cond_ref[...]).astype(BF16)
    o_ref[0] = _dot(a, w_ref[0].astype(BF16)) + b_ref[0]


def _mods(conds, w_mod, b_mod):
    depth, d, n = w_mod.shape
    tn = 1024
    out = pl.pallas_call(
        _mods_kernel,
        grid=(depth, n // tn),
        in_specs=[
            pl.BlockSpec((N_COND_ROWS, d), lambda l, j: (0, 0)),
            pl.BlockSpec((1, d, tn), lambda l, j: (l, 0, j)),
            pl.BlockSpec((1, 1, tn), lambda l, j: (l, 0, j)),
        ],
        out_specs=pl.BlockSpec((1, N_COND_ROWS, tn), lambda l, j: (l, 0, j)),
        out_shape=jax.ShapeDtypeStruct((depth, N_COND_ROWS, n), F32),
        compiler_params=_params(2),
        name="adaln_mods",
    )(conds, w_mod, b_mod.reshape(depth, 1, n))
    return out.reshape(depth * N_COND_ROWS, 1, n)


def _pool_fold_kernel(win_x_ref, win_g_ref, wgrp_ref, fold_ref, gate_ref):
    fold_ref[0] = _dot(win_x_ref[0].astype(BF16), wgrp_ref[0, 0].astype(BF16)).astype(BF16)
    gate_ref[0] = win_g_ref[0].astype(BF16)


def _pool_fold(w_in, w_grp):
    n_layers, d, two_e = w_in.shape
    n_grp, grp = w_grp.shape[1:3]
    e_dim = two_e // 2
    out = jax.ShapeDtypeStruct((n_layers, d, e_dim), BF16)
    return pl.pallas_call(
        _pool_fold_kernel,
        grid=(n_layers, n_grp),
        in_specs=[
            pl.BlockSpec((1, d, grp), lambda l, g: (l, 0, g)),
            pl.BlockSpec((1, d, grp), lambda l, g: (l, 0, n_grp + g)),
            pl.BlockSpec((1, 1, grp, grp), lambda l, g: (l, g, 0, 0)),
        ],
        out_specs=[pl.BlockSpec((1, d, grp), lambda l, g: (l, 0, g))] * 2,
        out_shape=[out, out],
        compiler_params=_params(2),
        name="pool_fold",
    )(w_in, w_in, w_grp)


def _pool_band():
    t = np.arange(POOL_BLOCK)[:, None] + POOL_HALO
    e = np.arange(POOL_BLOCK + 2 * POOL_HALO)[None, :]
    return jnp.asarray(np.concatenate([(e >= t - w // 2) & (e <= t + w // 2 - 1) for w in POOL_WINDOWS], axis=0), BF16)


def _pool_kernel(tiles_per_seq, seq_len, x_ref, xp_ref, xn_ref, mod_ref, band_ref, wfold_ref, wgate_ref, psc_ref,
                 wout_ref, g_ref, b_ref, o_ref, e_ref, p_ref):
    tm, d = x_ref.shape
    e_dim = psc_ref.shape[1]
    n_win = len(POOL_WINDOWS)
    grp = e_dim // n_win
    it = pl.program_id(0) % tiles_per_seq
    shift, scale, gate = _split_mod(mod_ref, d)
    x = x_ref[...]
    hb = (x * (1.0 + scale) + shift).astype(BF16)
    hp = (xp_ref[...] * (1.0 + scale) + shift).astype(BF16)
    hn = (xn_ref[...] * (1.0 + scale) + shift).astype(BF16)
    zeros = jnp.zeros((POOL_HALO, d), BF16)
    e_ref[0:POOL_HALO] = jnp.where(it != 0, hp, zeros)
    e_ref[POOL_HALO:POOL_HALO + tm] = hb
    e_ref[POOL_HALO + tm:2 * POOL_HALO + tm] = jnp.where(it != tiles_per_seq - 1, hn, zeros)

    h_rounded = hb.astype(F32)
    t = it * tm + lax.broadcasted_iota(jnp.int32, (tm, 1), 0)
    inv_cnt = []
    for w in POOL_WINDOWS:
        lo = jnp.maximum(t - w // 2, 0)
        hi = jnp.minimum(t + w // 2 - 1, seq_len - 1)
        inv_cnt.append(1.0 / (hi - lo + 1).astype(F32))
    for rb in range(tm // POOL_BLOCK):
        rows = slice(rb * POOL_BLOCK, (rb + 1) * POOL_BLOCK)
        sums = _dot(band_ref[...], e_ref[rb * POOL_BLOCK:(rb + 1) * POOL_BLOCK + 2 * POOL_HALO, :])
        for gi in range(n_win):
            win_sum = sums[gi * POOL_BLOCK:(gi + 1) * POOL_BLOCK]
            p_ref[gi, rows, :] = (win_sum * inv_cnt[gi][rows] - h_rounded[rows]).astype(BF16)

    mixed = jnp.concatenate(
        [_dot(p_ref[gi], wfold_ref[:, gi * grp:(gi + 1) * grp]) for gi in range(n_win)], axis=1)
    gate_pre = _dot(hb, wgate_ref[...])
    y = (mixed * psc_ref[...] * _silu(gate_pre)).astype(BF16)
    o = _dot(y, wout_ref[...])
    o_ref[...] = _layer_norm(DEEPNORM_ALPHA * x + gate * o, g_ref[...], b_ref[...])


def _pool_layer(x2d, seq_len, per_sample, mods, layer, j, band, w_fold, w_gate, p_scale, w_out, ln_g, ln_b):
    t, d = x2d.shape
    tm = ROW_TILE
    tps = seq_len // tm
    e_dim = w_out.shape[1]
    hb = tm // POOL_HALO
    last = t // POOL_HALO - 1

    def layer_spec(shape):
        return pl.BlockSpec((None,) + shape, lambda i: (j,) + (0,) * len(shape), pipeline_mode=pl.Buffered(1))

    return pl.pallas_call(
        functools.partial(_pool_kernel, tps, seq_len),
        grid=(t // tm,),
        in_specs=[
            pl.BlockSpec((tm, d), lambda i: (i, 0)),
            pl.BlockSpec((POOL_HALO, d), lambda i: (jnp.maximum(i * hb - 1, 0), 0)),
            pl.BlockSpec((POOL_HALO, d), lambda i: (jnp.minimum((i + 1) * hb, last), 0)),
            _mod_spec(layer, d, tps, per_sample),
            _const_spec(band.shape),
            layer_spec((d, e_dim)),
            layer_spec((d, e_dim)),
            layer_spec((1, e_dim)),
            layer_spec((e_dim, d)),
            _const_spec((1, d)),
            _const_spec((1, d)),
        ],
        out_specs=pl.BlockSpec((tm, d), lambda i: (i, 0)),
        out_shape=jax.ShapeDtypeStruct((t, d), F32),
        scratch_shapes=[pltpu.VMEM((tm + 2 * POOL_HALO, d), BF16),
                        pltpu.VMEM((len(POOL_WINDOWS), tm, d), BF16)],
        compiler_params=_params(1),
        name="pool_layer",
    )(x2d, x2d, x2d, mods, band, w_fold, w_gate, p_scale.reshape(-1, 1, e_dim), w_out,
      ln_g.reshape(1, d), ln_b.reshape(1, d))


def _sgu_kernel(x_ref, mod_ref, win_ref, lg_ref, lb_ref, ws_ref, bs_ref, wout_ref, g_ref, b_ref, o_ref, y_ref):
    tm, d = x_ref.shape
    e_dim = lg_ref.shape[1]
    dh = e_dim // H_B
    shift, scale, gate = _split_mod(mod_ref, d)
    x = x_ref[...]
    hb = (x * (1.0 + scale) + shift).astype(BF16)
    u = _gelu_tanh(_dot(hb, win_ref[:, 0:e_dim]))
    v = _gelu_tanh(_dot(hb, win_ref[:, e_dim:2 * e_dim]))
    v = _layer_norm(v, lg_ref[...], lb_ref[...]).astype(BF16)
    ug = u * _silu(_dot(hb, win_ref[:, 2 * e_dim:3 * e_dim]))
    for c in range(tm // CHUNK):
        rows = slice(c * CHUNK, (c + 1) * CHUNK)
        for hh in range(H_B):
            cols = slice(hh * dh, (hh + 1) * dh)
            sv = _dot(ws_ref[hh], v[rows, cols]) + bs_ref[:, hh:hh + 1]
            y_ref[rows, cols] = (ug[rows, cols] * sv).astype(BF16)
    o = _dot(y_ref[...], wout_ref[...])
    o_ref[...] = _layer_norm(DEEPNORM_ALPHA * x + gate * o, g_ref[...], b_ref[...])


def _sgu_layer(x2d, seq_len, per_sample, mods, layer, w_in, sln_g, sln_b, w_s, b_s_t, w_out, ln_g, ln_b):
    t, d = x2d.shape
    tm = ROW_TILE
    tps = seq_len // tm
    e_dim = w_out.shape[0]
    return pl.pallas_call(
        _sgu_kernel,
        grid=(t // tm,),
        in_specs=[
            pl.BlockSpec((tm, d), lambda i: (i, 0)),
            _mod_spec(layer, d, tps, per_sample),
            _const_spec(w_in.shape),
            _const_spec((1, e_dim)),
            _const_spec((1, e_dim)),
            _const_spec(w_s.shape),
            _const_spec(b_s_t.shape),
            _const_spec(w_out.shape),
            _const_spec((1, d)),
            _const_spec((1, d)),
        ],
        out_specs=pl.BlockSpec((tm, d), lambda i: (i, 0)),
        out_shape=jax.ShapeDtypeStruct((t, d), F32),
        scratch_shapes=[pltpu.VMEM((tm, e_dim), BF16)],
        compiler_params=_params(1),
        name="sgu_layer",
    )(x2d, mods, w_in, sln_g.reshape(1, e_dim), sln_b.reshape(1, e_dim), w_s, b_s_t, w_out,
      ln_g.reshape(1, d), ln_b.reshape(1, d))


def _na_proj_kernel(emit_f32_kv, q_scale, x_ref, mod_ref, win_ref, *out_refs):
    tm, d = x_ref.shape
    e = win_ref.shape[1] // 4
    shift, scale, _ = _split_mod(mod_ref, d)
    hb = (x_ref[...] * (1.0 + scale) + shift).astype(BF16)
    q_ref, k_ref, v_ref, sg_ref = out_refs[:4]
    q_ref[...] = (_dot(hb, win_ref[:, 0:e]) * q_scale).astype(BF16)
    k = _dot(hb, win_ref[:, e:2 * e])
    v = _dot(hb, win_ref[:, 2 * e:3 * e])
    k_ref[...] = k.astype(BF16)
    v_ref[...] = v.astype(BF16)
    sg_ref[...] = _silu(_dot(hb, win_ref[:, 3 * e:4 * e])).astype(BF16)
    if emit_f32_kv:
        for kv, ref in ((k, out_refs[4]), (v, out_refs[5])):
            for h in range(H_C):
                ref[pl.ds(h, tm, stride=H_C), :] = kv[:, h * DH_C:(h + 1) * DH_C]


def _na_proj(x2d, seq_len, per_sample, mods, layer, w_in, emit_f32_kv, q_scale):
    t, d = x2d.shape
    tm = ROW_TILE
    tps = seq_len // tm
    e = w_in.shape[1] // 4
    row_spec = pl.BlockSpec((tm, e), lambda i: (i, 0))
    out_specs = [row_spec] * 4
    out_shape = [jax.ShapeDtypeStruct((t, e), BF16)] * 4
    if emit_f32_kv:
        out_specs += [pl.BlockSpec((tm * H_C, DH_C), lambda i: (i, 0))] * 2
        out_shape += [jax.ShapeDtypeStruct((t * H_C, DH_C), F32)] * 2
    return pl.pallas_call(
        functools.partial(_na_proj_kernel, emit_f32_kv, q_scale),
        grid=(t // tm,),
        in_specs=[
            pl.BlockSpec((tm, d), lambda i: (i, 0)),
            _mod_spec(layer, d, tps, per_sample),
            _const_spec(w_in.shape),
        ],
        out_specs=out_specs,
        out_shape=out_shape,
        compiler_params=_params(1),
        name="na_proj",
    )(x2d, mods, w_in)


def _lane_is_first_head():
    return lax.broadcasted_iota(jnp.int32, (1, LANES), 1) < DH_C


def _ctx_attn_kernel(q_ref, k_ref, v_ref, o_ref):
    first = _lane_is_first_head()
    for j in range(q_ref.shape[1] // LANES):
        cols = slice(j * LANES, (j + 1) * LANES)
        q, k, v = q_ref[:, cols], k_ref[:, cols], v_ref[:, cols]
        outs = []
        for sel in (first, jnp.logical_not(first)):
            s = _dot_nt(jnp.where(sel, q, jnp.zeros_like(q)), k)
            p = jnp.exp(s - jnp.max(s, axis=-1, keepdims=True))
            l = jnp.sum(p, axis=-1, keepdims=True)
            outs.append(_dot(p.astype(BF16), v) / l)
        o_ref[:, cols] = jnp.where(first, outs[0], outs[1]).astype(BF16)


def _ctx_attn(q, k, v, seq_len):
    t, e = q.shape
    spec = pl.BlockSpec((seq_len, e), lambda b: (b, 0))
    return pl.pallas_call(
        _ctx_attn_kernel,
        grid=(t // seq_len,),
        in_specs=[spec, spec, spec],
        out_specs=spec,
        out_shape=jax.ShapeDtypeStruct((t, e), BF16),
        compiler_params=_params(1),
        name="ctx_attn",
    )(q, k, v)


def _key_row_start(r_blk, rows):
    return jnp.clip(r_blk * ATT_Q_ROWS - WIN_R // 2, 0, rows - ATT_K_ROWS)


def _nbr_bias_build(rows, rpb_ref, bias_ref):
    n_blk = rows // ATT_Q_ROWS
    qc = lax.broadcasted_iota(jnp.int32, (GRID_W, LANES), 0)
    lane = lax.broadcasted_iota(jnp.int32, (GRID_W, LANES), 1)
    kc = lane & (GRID_W - 1)
    c0 = jnp.clip(qc - WIN_C // 2, 0, GRID_W - WIN_C)
    col_in = (kc >= c0) & (kc < c0 + WIN_C)
    first = lane < GRID_W
    neg = jnp.full((GRID_W, LANES), -jnp.inf, F32)
    for hh in range(2):
        lo, hi = [], []
        for dr in range(2 * WIN_R - 1):
            row = jnp.broadcast_to(rpb_ref[hh, dr:dr + 1, :] * LOG2_E, (GRID_W, LANES))
            lo.append(pltpu.roll(row, LANES - (WIN_C - 1), 1, stride=1, stride_axis=0))
            hi.append(pltpu.roll(row, GRID_W - (WIN_C - 1), 1, stride=1, stride_axis=0))
        for var, r_blk in enumerate((0, 1, n_blk - 1)):
            ks = min(max(r_blk * ATT_Q_ROWS - WIN_R // 2, 0), rows - ATT_K_ROWS)
            for qi in range(ATT_Q_ROWS):
                r = r_blk * ATT_Q_ROWS + qi
                r0 = min(max(r - WIN_R // 2, 0), rows - WIN_R)
                for kp in range(ATT_K_ROWS // 2):
                    halves = []
                    for half, src in enumerate((lo, hi)):
                        kr = ks + 2 * kp + half
                        halves.append(src[kr - r + WIN_R - 1] if r0 <= kr < r0 + WIN_R else neg)
                    blk = jnp.where(col_in, jnp.where(first, halves[0], halves[1]), neg)
                    bias_ref[hh, var, qi * GRID_W:(qi + 1) * GRID_W, kp * LANES:(kp + 1) * LANES] = blk


def _nbr_attn_kernel(rows, rpb_ref, q_ref, k_ref, v_ref, ck_ref, cv_ref, o_ref, bias_ref, kc_ref, vx_ref, s_ref,
                     p_ref, oa_ref):
    n_blk = rows // ATT_Q_ROWS
    tq = ATT_Q_ROWS * GRID_W
    nk = ATT_K_ROWS * GRID_W
    seq = q_ref.shape[0]

    @pl.when(pl.program_id(1) == 0)
    def _():
        _nbr_bias_build(rows, rpb_ref, bias_ref)

    first = _lane_is_first_head()
    head_lanes = (first, jnp.logical_not(first))
    past = kc_ref.shape[0]

    def cached_pair(ref):
        h0 = 2 * pl.program_id(0)
        halves = [ref[pl.ds(h0 + hh, past, stride=H_C), :] for hh in range(2)]
        return jnp.concatenate(halves, axis=1).astype(BF16)

    kc_ref[...] = cached_pair(ck_ref)
    vc = cached_pair(cv_ref)
    for hh in range(2):
        vx_ref[hh, 0:seq] = jnp.where(head_lanes[hh], v_ref[...], jnp.ones(v_ref.shape, BF16))
        vx_ref[hh, seq:] = jnp.where(head_lanes[hh], vc, jnp.ones(vc.shape, BF16))

    def q_rows(r):
        return pl.ds(pl.multiple_of(r * tq, tq), tq)

    def k_start(r):
        return pl.multiple_of(_key_row_start(r, rows) * GRID_W, GRID_W)

    def logits(r, hh):
        q = q_ref[q_rows(r), :]
        qh = jnp.where(head_lanes[hh], q, jnp.zeros_like(q))
        var = jnp.where(r == 0, 0, jnp.where(r == n_blk - 1, 2, 1))
        s_ref[hh, :, 0:nk] = _dot_nt(qh, k_ref[pl.ds(k_start(r), nk), :]) + bias_ref[hh, var]
        s_ref[hh, :, nk:] = _dot_nt(qh, kc_ref[...])

    def softmax(hh):
        s = s_ref[hh]
        p_ref[hh] = jnp.exp2((s - jnp.max(s, axis=-1, keepdims=True)).astype(BF16))

    def weighted_values(r, hh):
        return (_dot(p_ref[hh, :, 0:nk], vx_ref[hh, pl.ds(k_start(r), nk), :])
                + _dot(p_ref[hh, :, nk:], vx_ref[hh, seq:, :]))

    def emit(r, o_second):
        o_first = oa_ref[...]
        num = jnp.where(first, o_first, o_second)
        den = pltpu.roll(jnp.where(first, o_second, o_first), DH_C, 1)
        o_ref[q_rows(r), :] = (num / den).astype(BF16)

    p_ref[1] = jnp.ones(p_ref.shape[1:], BF16)
    oa_ref[...] = jnp.ones(oa_ref.shape, F32)
    logits(0, 0)

    def block(r, carry):
        prev = jnp.maximum(r - 1, 0)
        emit(prev, weighted_values(prev, 1))
        logits(r, 1)
        softmax(0)
        oa_ref[...] = weighted_values(r, 0)
        logits(jnp.minimum(r + 1, n_blk - 1), 0)
        softmax(1)
        return carry

    lax.fori_loop(0, n_blk, block, 0)
    emit(n_blk - 1, weighted_values(n_blk - 1, 1))


def _nbr_attn(q, k, v, cache_k, cache_v, cache_layer, rpb, n_batch):
    t, e = q.shape
    seq = t // n_batch
    rows = seq // GRID_W
    n_cached, past = cache_k.shape[1:3]
    ck = cache_k.reshape(-1, DH_C)
    cv = cache_v.reshape(-1, DH_C)
    tq, nk = ATT_Q_ROWS * GRID_W, ATT_K_ROWS * GRID_W
    n_dr, n_dc = rpb.shape[1:]
    rpb_pad = jnp.pad(rpb.astype(F32), ((0, 0), (0, 2 * WIN_R - n_dr), (0, LANES - n_dc)))
    tok_spec = pl.BlockSpec((seq, LANES), lambda j, b: (b, j))
    ctx_spec = pl.BlockSpec((past * H_C, DH_C), lambda j, b: (b * n_cached + cache_layer, 0))
    return pl.pallas_call(
        functools.partial(_nbr_attn_kernel, rows),
        grid=(e // LANES, n_batch),
        in_specs=[pl.BlockSpec((2, 2 * WIN_R, LANES), lambda j, b: (j, 0, 0)),
                  tok_spec, tok_spec, tok_spec, ctx_spec, ctx_spec],
        out_specs=tok_spec,
        out_shape=jax.ShapeDtypeStruct((t, e), BF16),
        scratch_shapes=[
            pltpu.VMEM((2, 3, tq, nk), F32),
            pltpu.VMEM((past, LANES), BF16),
            pltpu.VMEM((2, seq + past, LANES), BF16),
            pltpu.VMEM((2, tq, nk + past), F32),
            pltpu.VMEM((2, tq, nk + past), BF16),
            pltpu.VMEM((tq, LANES), F32),
        ],
        compiler_params=_params(2),
        name="nbr_attn",
    )(rpb_pad, q, k, v, ck, cv)


def _na_out_kernel(x_ref, a_ref, sg_ref, mod_ref, wout_ref, g_ref, b_ref, o_ref):
    tm, d = x_ref.shape
    _, _, gate = _split_mod(mod_ref, d)
    y = (a_ref[...].astype(F32) * sg_ref[...].astype(F32)).astype(BF16)
    o = _dot(y, wout_ref[...])
    o_ref[...] = _layer_norm(DEEPNORM_ALPHA * x_ref[...] + gate * o, g_ref[...], b_ref[...])


def _na_out(x2d, attn, sg, seq_len, per_sample, mods, layer, w_out, ln_g, ln_b):
    t, d = x2d.shape
    tm = ROW_TILE
    tps = seq_len // tm
    e = w_out.shape[0]
    return pl.pallas_call(
        _na_out_kernel,
        grid=(t // tm,),
        in_specs=[
            pl.BlockSpec((tm, d), lambda i: (i, 0)),
            pl.BlockSpec((tm, e), lambda i: (i, 0)),
            pl.BlockSpec((tm, e), lambda i: (i, 0)),
            _mod_spec(layer, d, tps, per_sample),
            _const_spec(w_out.shape),
            _const_spec((1, d)),
            _const_spec((1, d)),
        ],
        out_specs=pl.BlockSpec((tm, d), lambda i: (i, 0)),
        out_shape=jax.ShapeDtypeStruct((t, d), F32),
        compiler_params=_params(1),
        name="na_out",
    )(x2d, attn, sg, mods, w_out, ln_g.reshape(1, d), ln_b.reshape(1, d))


def kernel(x_prompt, x_sample, c, cache_k, cache_v, c_ctx, w_mod, b_mod, ln_g, ln_b, pool_w_in, pool_w_grp,
           pool_scale, pool_w_out, sgu_w_in, sgu_ln_g, sgu_ln_b, sgu_w_s, sgu_b_s, sgu_w_out, na_w_in, na_rpb,
           na_w_out):
    n_p, seq_p, d = x_prompt.shape
    n_s, seq_s, _ = x_sample.shape
    assert n_s + 1 <= N_COND_ROWS and seq_p % ROW_TILE == 0 and seq_s % ROW_TILE == 0
    assert seq_s % (GRID_W * ATT_Q_ROWS) == 0 and seq_s // GRID_W >= ATT_K_ROWS
    conds = jnp.zeros((N_COND_ROWS, d), F32).at[0].set(c_ctx).at[1:1 + n_s].set(c)
    mods = _mods(conds, w_mod, b_mod)
    pool_w_fold, pool_w_gate = _pool_fold(pool_w_in, pool_w_grp)
    pool_w_out_b = pool_w_out.astype(BF16)
    pool_band = _pool_band()

    yp = x_prompt.reshape(n_p * seq_p, d)
    ys = x_sample.reshape(n_s * seq_s, d)
    streams = ((seq_p, False), (seq_s, True))
    ctx_k = ctx_v = None
    for i in range(DEPTH):
        kind, j = i % N_MIXERS, i // N_MIXERS
        ys_in = (yp, ys)
        outs = []
        if kind == 0:
            for x2d, (seq, per_sample) in zip(ys_in, streams):
                outs.append(_pool_layer(x2d, seq, per_sample, mods, i, j, pool_band, pool_w_fold, pool_w_gate,
                                        pool_scale, pool_w_out_b, ln_g[i], ln_b[i]))
        elif kind == 1:
            w_in, w_s, w_out = sgu_w_in[j].astype(BF16), sgu_w_s[j].astype(BF16), sgu_w_out[j].astype(BF16)
            for x2d, (seq, per_sample) in zip(ys_in, streams):
                outs.append(_sgu_layer(x2d, seq, per_sample, mods, i, w_in, sgu_ln_g[j], sgu_ln_b[j], w_s,
                                       sgu_b_s[j].T, w_out, ln_g[i], ln_b[i]))
        else:
            w_in, w_out = na_w_in[j].astype(BF16), na_w_out[j].astype(BF16)
            q, k, v, sg, kf, vf = _na_proj(yp, seq_p, False, mods, i, w_in, True, DH_C ** -0.5)
            ctx_k = kf.reshape(n_p, 1, seq_p, H_C, DH_C)
            ctx_v = vf.reshape(n_p, 1, seq_p, H_C, DH_C)
            attn = _ctx_attn(q, k, v, seq_p)
            outs.append(_na_out(yp, attn, sg, seq_p, False, mods, i, w_out, ln_g[i], ln_b[i]))
            q, k, v, sg = _na_proj(ys, seq_s, True, mods, i, w_in, False, DH_C ** -0.5 * LOG2_E)
            attn = _nbr_attn(q, k, v, cache_k, cache_v, j, na_rpb[j], n_s)
            outs.append(_na_out(ys, attn, sg, seq_s, True, mods, i, w_out, ln_g[i], ln_b[i]))
        yp, ys = outs
    return (yp.reshape(n_p, seq_p, d), ys.reshape(n_s, seq_s, d), ctx_k, ctx_v)
```

```python
import functools

import jax
import jax.numpy as jnp
import numpy as np
from jax import lax
from jax.experimental import pallas as pl
from jax.experimental.pallas import tpu as pltpu

F32 = jnp.float32
BF16 = jnp.bfloat16

DEPTH = 4
N_MIXERS = 3
POOL_WINDOWS = (2, 4, 8, 16)
POOL_HALO = 16
POOL_BLOCK = 128
CHUNK = 128
H_B = 8
H_C = 16
DH_C = 64
GRID_W = 64
WIN_R = 8
WIN_C = 16
DEEPNORM_ALPHA = (2 * DEPTH) ** 0.25
LN_EPS = 1e-5
LOG2_E = float(np.log2(np.e))

N_COND_ROWS = 8
ROW_TILE = 512
ATT_Q_ROWS = 4
ATT_K_ROWS = 12
LANES = 128
VMEM_LIMIT = 56 * 1024 * 1024


def _const_spec(shape):
    nd = len(shape)
    return pl.BlockSpec(shape, lambda *_: (0,) * nd, pipeline_mode=pl.Buffered(1))


def _params(n_axes):
    return pltpu.CompilerParams(dimension_semantics=("arbitrary",) * n_axes, vmem_limit_bytes=VMEM_LIMIT)


def _mod_spec(layer, d, tiles_per_seq, per_sample):
    base = layer * N_COND_ROWS
    if per_sample:
        return pl.BlockSpec((1, 1, 3 * d), lambda i: (base + 1 + i // tiles_per_seq, 0, 0))
    return pl.BlockSpec((1, 1, 3 * d), lambda i: (base, 0, 0))


def _split_mod(mod_ref, d):
    m = mod_ref[0]
    return m[:, :d], m[:, d:2 * d], m[:, 2 * d:]


def _silu(x):
    return x / (1.0 + jnp.exp(-x))


def _gelu_tanh(x):
    c = np.float32(np.sqrt(2.0 / np.pi))
    return x * (0.5 * (1.0 + jnp.tanh(c * (x + 0.044715 * (x * x * x)))))


def _layer_norm(x, g, b):
    mu = jnp.mean(x, axis=-1, keepdims=True)
    d = x - mu
    var = jnp.mean(d * d, axis=-1, keepdims=True)
    return d * lax.rsqrt(var + LN_EPS) * g + b


def _dot(a, b):
    return jnp.dot(a, b, preferred_element_type=F32)


def _dot_nt(a, b):
    return lax.dot_general(a, b, (((1,), (1,)), ((), ())), preferred_element_type=F32)


def _mods_kernel(cond_ref, w_ref, b_ref, o_ref):
    a = _silu(cond_ref[...]).astype(BF16)
    o_ref[0] = _dot(a, w_ref[0].astype(BF16)) + b_ref[0]


def _mods(conds, w_mod, b_mod):
    depth, d, n = w_mod.shape
    tn = 1024
    out = pl.pallas_call(
        _mods_kernel,
        grid=(depth, n // tn),
        in_specs=[
            pl.BlockSpec((N_COND_ROWS, d), lambda l, j: (0, 0)),
            pl.BlockSpec((1, d, tn), lambda l, j: (l, 0, j)),
            pl.BlockSpec((1, 1, tn), lambda l, j: (l, 0, j)),
        ],
        out_specs=pl.BlockSpec((1, N_COND_ROWS, tn), lambda l, j: (l, 0, j)),
        out_shape=jax.ShapeDtypeStruct((depth, N_COND_ROWS, n), F32),
        compiler_params=_params(2),
        name="adaln_mods",
    )(conds, w_mod, b_mod.reshape(depth, 1, n))
    return out.reshape(depth * N_COND_ROWS, 1, n)


def _pool_fold_kernel(win_x_ref, win_g_ref, wgrp_ref, fold_ref, gate_ref):
    fold_ref[0] = _dot(win_x_ref[0].astype(BF16), wgrp_ref[0, 0].astype(BF16)).astype(BF16)
    gate_ref[0] = win_g_ref[0].astype(BF16)


def _pool_fold(w_in, w_grp):
    n_layers, d, two_e = w_in.shape
    n_grp, grp = w_grp.shape[1:3]
    e_dim = two_e // 2
    out = jax.ShapeDtypeStruct((n_layers, d, e_dim), BF16)
    return pl.pallas_call(
        _pool_fold_kernel,
        grid=(n_layers, n_grp),
        in_specs=[
            pl.BlockSpec((1, d, grp), lambda l, g: (l, 0, g)),
            pl.BlockSpec((1, d, grp), lambda l, g: (l, 0, n_grp + g)),
            pl.BlockSpec((1, 1, grp, grp), lambda l, g: (l, g, 0, 0)),
        ],
        out_specs=[pl.BlockSpec((1, d, grp), lambda l, g: (l, 0, g))] * 2,
        out_shape=[out, out],
        compiler_params=_params(2),
        name="pool_fold",
    )(w_in, w_in, w_grp)


def _pool_band():
    t = np.arange(POOL_BLOCK)[:, None] + POOL_HALO
    e = np.arange(POOL_BLOCK + 2 * POOL_HALO)[None, :]
    return jnp.asarray(np.concatenate([(e >= t - w // 2) & (e <= t + w // 2 - 1) for w in POOL_WINDOWS], axis=0), BF16)


def _pool_kernel(tiles_per_seq, seq_len, x_ref, xp_ref, xn_ref, mod_ref, band_ref, wfold_ref, wgate_ref, psc_ref,
                 wout_ref, g_ref, b_ref, o_ref, e_ref, p_ref):
    tm, d = x_ref.shape
    e_dim = psc_ref.shape[1]
    n_win = len(POOL_WINDOWS)
    grp = e_dim // n_win
    it = pl.program_id(0) % tiles_per_seq
    shift, scale, gate = _split_mod(mod_ref, d)
    x = x_ref[...]
    hb = (x * (1.0 + scale) + shift).astype(BF16)
    hp = (xp_ref[...] * (1.0 + scale) + shift).astype(BF16)
    hn = (xn_ref[...] * (1.0 + scale) + shift).astype(BF16)
    zeros = jnp.zeros((POOL_HALO, d), BF16)
    e_ref[0:POOL_HALO] = jnp.where(it != 0, hp, zeros)
    e_ref[POOL_HALO:POOL_HALO + tm] = hb
    e_ref[POOL_HALO + tm:2 * POOL_HALO + tm] = jnp.where(it != tiles_per_seq - 1, hn, zeros)

    h_rounded = hb.astype(F32)
    t = it * tm + lax.broadcasted_iota(jnp.int32, (tm, 1), 0)
    inv_cnt = []
    for w in POOL_WINDOWS:
        lo = jnp.maximum(t - w // 2, 0)
        hi = jnp.minimum(t + w // 2 - 1, seq_len - 1)
        inv_cnt.append(1.0 / (hi - lo + 1).astype(F32))
    for rb in range(tm // POOL_BLOCK):
        rows = slice(rb * POOL_BLOCK, (rb + 1) * POOL_BLOCK)
        sums = _dot(band_ref[...], e_ref[rb * POOL_BLOCK:(rb + 1) * POOL_BLOCK + 2 * POOL_HALO, :])
        for gi in range(n_win):
            win_sum = sums[gi * POOL_BLOCK:(gi + 1) * POOL_BLOCK]
            p_ref[gi, rows, :] = (win_sum * inv_cnt[gi][rows] - h_rounded[rows]).astype(BF16)

    mixed = jnp.concatenate(
        [_dot(p_ref[gi], wfold_ref[:, gi * grp:(gi + 1) * grp]) for gi in range(n_win)], axis=1)
    gate_pre = _dot(hb, wgate_ref[...])
    y = (mixed * psc_ref[...] * _silu(gate_pre)).astype(BF16)
    o = _dot(y, wout_ref[...])
    o_ref[...] = _layer_norm(DEEPNORM_ALPHA * x + gate * o, g_ref[...], b_ref[...])


def _pool_layer(x2d, seq_len, per_sample, mods, layer, j, band, w_fold, w_gate, p_scale, w_out, ln_g, ln_b):
    t, d = x2d.shape
    tm = min(ROW_TILE, seq_len)
    tps = seq_len // tm
    e_dim = w_out.shape[1]
    hb = tm // POOL_HALO
    last = t // POOL_HALO - 1

    def layer_spec(shape):
        return pl.BlockSpec((None,) + shape, lambda i: (j,) + (0,) * len(shape), pipeline_mode=pl.Buffered(1))

    return pl.pallas_call(
        functools.partial(_pool_kernel, tps, seq_len),
        grid=(t // tm,),
        in_specs=[
            pl.BlockSpec((tm, d), lambda i: (i, 0)),
            pl.BlockSpec((POOL_HALO, d), lambda i: (jnp.maximum(i * hb - 1, 0), 0)),
            pl.BlockSpec((POOL_HALO, d), lambda i: (jnp.minimum((i + 1) * hb, last), 0)),
            _mod_spec(layer, d, tps, per_sample),
            _const_spec(band.shape),
            layer_spec((d, e_dim)),
            layer_spec((d, e_dim)),
            layer_spec((1, e_dim)),
            layer_spec((e_dim, d)),
            _const_spec((1, d)),
            _const_spec((1, d)),
        ],
        out_specs=pl.BlockSpec((tm, d), lambda i: (i, 0)),
        out_shape=jax.ShapeDtypeStruct((t, d), F32),
        scratch_shapes=[pltpu.VMEM((tm + 2 * POOL_HALO, d), BF16),
                        pltpu.VMEM((len(POOL_WINDOWS), tm, d), BF16)],
        compiler_params=_params(1),
        name="pool_layer",
    )(x2d, x2d, x2d, mods, band, w_fold, w_gate, p_scale.reshape(-1, 1, e_dim), w_out,
      ln_g.reshape(1, d), ln_b.reshape(1, d))


def _sgu_kernel(x_ref, mod_ref, win_ref, lg_ref, lb_ref, ws_ref, bs_ref, wout_ref, g_ref, b_ref, o_ref, y_ref):
    tm, d = x_ref.shape
    e_dim = lg_ref.shape[1]
    dh = e_dim // H_B
    shift, scale, gate = _split_mod(mod_ref, d)
    x = x_ref[...]
    hb = (x * (1.0 + scale) + shift).astype(BF16)
    u = _gelu_tanh(_dot(hb, win_ref[:, 0:e_dim]))
    v = _gelu_tanh(_dot(hb, win_ref[:, e_dim:2 * e_dim]))
    v = _layer_norm(v, lg_ref[...], lb_ref[...]).astype(BF16)
    ug = u * _silu(_dot(hb, win_ref[:, 2 * e_dim:3 * e_dim]))
    for c in range(tm // CHUNK):
        rows = slice(c * CHUNK, (c + 1) * CHUNK)
        for hh in range(H_B):
            cols = slice(hh * dh, (hh + 1) * dh)
            sv = _dot(ws_ref[hh], v[rows, cols]) + bs_ref[:, hh:hh + 1]
            y_ref[rows, cols] = (ug[rows, cols] * sv).astype(BF16)
    o = _dot(y_ref[...], wout_ref[...])
    o_ref[...] = _layer_norm(DEEPNORM_ALPHA * x + gate * o, g_ref[...], b_ref[...])


def _sgu_layer(x2d, seq_len, per_sample, mods, layer, w_in, sln_g, sln_b, w_s, b_s_t, w_out, ln_g, ln_b):
    t, d = x2d.shape
    tm = ROW_TILE
    tps = max(seq_len // tm, 1)
    e_dim = w_out.shape[0]
    return pl.pallas_call(
        _sgu_kernel,
        grid=(t // tm,),
        in_specs=[
            pl.BlockSpec((tm, d), lambda i: (i, 0)),
            _mod_spec(layer, d, tps, per_sample),
            _const_spec(w_in.shape),
            _const_spec((1, e_dim)),
            _const_spec((1, e_dim)),
            _const_spec(w_s.shape),
            _const_spec(b_s_t.shape),
            _const_spec(w_out.shape),
            _const_spec((1, d)),
            _const_spec((1, d)),
        ],
        out_specs=pl.BlockSpec((tm, d), lambda i: (i, 0)),
        out_shape=jax.ShapeDtypeStruct((t, d), F32),
        scratch_shapes=[pltpu.VMEM((tm, e_dim), BF16)],
        compiler_params=_params(1),
        name="sgu_layer",
    )(x2d, mods, w_in, sln_g.reshape(1, e_dim), sln_b.reshape(1, e_dim), w_s, b_s_t, w_out,
      ln_g.reshape(1, d), ln_b.reshape(1, d))


def _na_proj_kernel(emit_f32_kv, q_scale, x_ref, mod_ref, win_ref, *out_refs):
    tm, d = x_ref.shape
    e = win_ref.shape[1] // 4
    shift, scale, _ = _split_mod(mod_ref, d)
    hb = (x_ref[...] * (1.0 + scale) + shift).astype(BF16)
    q_ref, k_ref, v_ref, sg_ref = out_refs[:4]
    q_ref[...] = (_dot(hb, win_ref[:, 0:e]) * q_scale).astype(BF16)
    k = _dot(hb, win_ref[:, e:2 * e])
    v = _dot(hb, win_ref[:, 2 * e:3 * e])
    k_ref[...] = k.astype(BF16)
    v_ref[...] = v.astype(BF16)
    sg_ref[...] = _silu(_dot(hb, win_ref[:, 3 * e:4 * e])).astype(BF16)
    if emit_f32_kv:
        for kv, ref in ((k, out_refs[4]), (v, out_refs[5])):
            for h in range(H_C):
                ref[pl.ds(h, tm, stride=H_C), :] = kv[:, h * DH_C:(h + 1) * DH_C]


def _na_proj(x2d, seq_len, per_sample, mods, layer, w_in, emit_f32_kv, q_scale):
    t, d = x2d.shape
    tm = ROW_TILE
    tps = max(seq_len // tm, 1)
    e = w_in.shape[1] // 4
    row_spec = pl.BlockSpec((tm, e), lambda i: (i, 0))
    out_specs = [row_spec] * 4
    out_shape = [jax.ShapeDtypeStruct((t, e), BF16)] * 4
    if emit_f32_kv:
        out_specs += [pl.BlockSpec((tm * H_C, DH_C), lambda i: (i, 0))] * 2
        out_shape += [jax.ShapeDtypeStruct((t * H_C, DH_C), F32)] * 2
    return pl.pallas_call(
        functools.partial(_na_proj_kernel, emit_f32_kv, q_scale),
        grid=(t // tm,),
        in_specs=[
            pl.BlockSpec((tm, d), lambda i: (i, 0)),
            _mod_spec(layer, d, tps, per_sample),
            _const_spec(w_in.shape),
        ],
        out_specs=out_specs,
        out_shape=out_shape,
        compiler_params=_params(1),
        name="na_proj",
    )(x2d, mods, w_in)


def _lane_is_first_head():
    return lax.broadcasted_iota(jnp.int32, (1, LANES), 1) < DH_C


def _ctx_attn_kernel(q_ref, k_ref, v_ref, o_ref):
    first = _lane_is_first_head()
    for j in range(q_ref.shape[1] // LANES):
        cols = slice(j * LANES, (j + 1) * LANES)
        q, k, v = q_ref[:, cols], k_ref[:, cols], v_ref[:, cols]
        outs = []
        for sel in (first, jnp.logical_not(first)):
            s = _dot_nt(jnp.where(sel, q, jnp.zeros_like(q)), k)
            p = jnp.exp(s - jnp.max(s, axis=-1, keepdims=True))
            l = jnp.sum(p, axis=-1, keepdims=True)
            outs.append(_dot(p.astype(BF16), v) / l)
        o_ref[:, cols] = jnp.where(first, outs[0], outs[1]).astype(BF16)


def _ctx_attn(q, k, v, seq_len):
    t, e = q.shape
    spec = pl.BlockSpec((seq_len, e), lambda b: (b, 0))
    return pl.pallas_call(
        _ctx_attn_kernel,
        grid=(t // seq_len,),
        in_specs=[spec, spec, spec],
        out_specs=spec,
        out_shape=jax.ShapeDtypeStruct((t, e), BF16),
        compiler_params=_params(1),
        name="ctx_attn",
    )(q, k, v)


def _key_row_start(r_blk, rows):
    return jnp.clip(r_blk * ATT_Q_ROWS - WIN_R // 2, 0, rows - ATT_K_ROWS)


def _nbr_bias_build(rows, rpb_ref, bias_ref):
    n_blk = rows // ATT_Q_ROWS
    qc = lax.broadcasted_iota(jnp.int32, (GRID_W, LANES), 0)
    lane = lax.broadcasted_iota(jnp.int32, (GRID_W, LANES), 1)
    kc = lane & (GRID_W - 1)
    c0 = jnp.clip(qc - WIN_C // 2, 0, GRID_W - WIN_C)
    col_in = (kc >= c0) & (kc < c0 + WIN_C)
    first = lane < GRID_W
    neg = jnp.full((GRID_W, LANES), -jnp.inf, F32)
    for hh in range(2):
        lo, hi = [], []
        for dr in range(2 * WIN_R - 1):
            row = jnp.broadcast_to(rpb_ref[hh, dr:dr + 1, :] * LOG2_E, (GRID_W, LANES))
            lo.append(pltpu.roll(row, LANES - (WIN_C - 1), 1, stride=1, stride_axis=0))
            hi.append(pltpu.roll(row, GRID_W - (WIN_C - 1), 1, stride=1, stride_axis=0))
        for var, r_blk in enumerate((0, 1, n_blk - 1)):
            ks = min(max(r_blk * ATT_Q_ROWS - WIN_R // 2, 0), rows - ATT_K_ROWS)
            for qi in range(ATT_Q_ROWS):
                r = r_blk * ATT_Q_ROWS + qi
                r0 = min(max(r - WIN_R // 2, 0), rows - WIN_R)
                for kp in range(ATT_K_ROWS // 2):
                    halves = []
                    for half, src in enumerate((lo, hi)):
                        kr = ks + 2 * kp + half
                        halves.append(src[kr - r + WIN_R - 1] if r0 <= kr < r0 + WIN_R else neg)
                    blk = jnp.where(col_in, jnp.where(first, halves[0], halves[1]), neg)
                    bias_ref[hh, var, qi * GRID_W:(qi + 1) * GRID_W, kp * LANES:(kp + 1) * LANES] = blk


def _nbr_attn_kernel(rows, rpb_ref, q_ref, k_ref, v_ref, ck_ref, cv_ref, o_ref, bias_ref, kc_ref, vx_ref, s_ref,
                     p_ref, oa_ref):
    n_blk = rows // ATT_Q_ROWS
    tq = ATT_Q_ROWS * GRID_W
    nk = ATT_K_ROWS * GRID_W
    seq = q_ref.shape[0]

    @pl.when(pl.program_id(1) == 0)
    def _():
        _nbr_bias_build(rows, rpb_ref, bias_ref)

    first = _lane_is_first_head()
    head_lanes = (first, jnp.logical_not(first))
    past = kc_ref.shape[0]

    def cached_pair(ref):
        h0 = 2 * pl.program_id(0)
        halves = [ref[pl.ds(h0 + hh, past, stride=H_C), :] for hh in range(2)]
        return jnp.concatenate(halves, axis=1).astype(BF16)

    kc_ref[...] = cached_pair(ck_ref)
    vc = cached_pair(cv_ref)
    for hh in range(2):
        vx_ref[hh, 0:seq] = jnp.where(head_lanes[hh], v_ref[...], jnp.ones(v_ref.shape, BF16))
        vx_ref[hh, seq:] = jnp.where(head_lanes[hh], vc, jnp.ones(vc.shape, BF16))

    def q_rows(r):
        return pl.ds(pl.multiple_of(r * tq, tq), tq)

    def k_start(r):
        return pl.multiple_of(_key_row_start(r, rows) * GRID_W, GRID_W)

    def logits(r, hh):
        q = q_ref[q_rows(r), :]
        qh = jnp.where(head_lanes[hh], q, jnp.zeros_like(q))
        var = jnp.where(r == 0, 0, jnp.where(r == n_blk - 1, 2, 1))
        s_ref[hh, :, 0:nk] = _dot_nt(qh, k_ref[pl.ds(k_start(r), nk), :]) + bias_ref[hh, var]
        s_ref[hh, :, nk:] = _dot_nt(qh, kc_ref[...])

    def softmax(hh):
        s = s_ref[hh]
        p_ref[hh] = jnp.exp2((s - jnp.max(s, axis=-1, keepdims=True)).astype(BF16))

    def weighted_values(r, hh):
        return (_dot(p_ref[hh, :, 0:nk], vx_ref[hh, pl.ds(k_start(r), nk), :])
                + _dot(p_ref[hh, :, nk:], vx_ref[hh, seq:, :]))

    def emit(r, o_second):
        o_first = oa_ref[...]
        num = jnp.where(first, o_first, o_second)
        den = pltpu.roll(jnp.where(first, o_second, o_first), DH_C, 1)
        o_ref[q_rows(r), :] = (num / den).astype(BF16)

    p_ref[1] = jnp.ones(p_ref.shape[1:], BF16)
    oa_ref[...] = jnp.ones(oa_ref.shape, F32)
    logits(0, 0)

    def block(r, carry):
        prev = jnp.maximum(r - 1, 0)
        emit(prev, weighted_values(prev, 1))
        logits(r, 1)
        softmax(0)
        oa_ref[...] = weighted_values(r, 0)
        logits(jnp.minimum(r + 1, n_blk - 1), 0)
        softmax(1)
        return carry

    lax.fori_loop(0, n_blk, block, 0, unroll=2)
    emit(n_blk - 1, weighted_values(n_blk - 1, 1))


def _nbr_attn(q, k, v, cache_k, cache_v, cache_layer, rpb, n_batch):
    t, e = q.shape
    seq = t // n_batch
    rows = seq // GRID_W
    n_cached, past = cache_k.shape[1:3]
    ck = cache_k.reshape(-1, DH_C)
    cv = cache_v.reshape(-1, DH_C)
    tq, nk = ATT_Q_ROWS * GRID_W, ATT_K_ROWS * GRID_W
    n_dr, n_dc = rpb.shape[1:]
    rpb_pad = jnp.pad(rpb.astype(F32), ((0, 0), (0, 2 * WIN_R - n_dr), (0, LANES - n_dc)))
    tok_spec = pl.BlockSpec((seq, LANES), lambda j, b: (b, j))
    ctx_spec = pl.BlockSpec((past * H_C, DH_C), lambda j, b: (b * n_cached + cache_layer, 0))
    return pl.pallas_call(
        functools.partial(_nbr_attn_kernel, rows),
        grid=(e // LANES, n_batch),
        in_specs=[pl.BlockSpec((2, 2 * WIN_R, LANES), lambda j, b: (j, 0, 0)),
                  tok_spec, tok_spec, tok_spec, ctx_spec, ctx_spec],
        out_specs=tok_spec,
        out_shape=jax.ShapeDtypeStruct((t, e), BF16),
        scratch_shapes=[
            pltpu.VMEM((2, 3, tq, nk), F32),
            pltpu.VMEM((past, LANES), BF16),
            pltpu.VMEM((2, seq + past, LANES), BF16),
            pltpu.VMEM((2, tq, nk + past), F32),
            pltpu.VMEM((2, tq, nk + past), BF16),
            pltpu.VMEM((tq, LANES), F32),
        ],
        compiler_params=_params(2),
        name="nbr_attn",
    )(rpb_pad, q, k, v, ck, cv)


def _na_out_kernel(x_ref, a_ref, sg_ref, mod_ref, wout_ref, g_ref, b_ref, o_ref):
    tm, d = x_ref.shape
    _, _, gate = _split_mod(mod_ref, d)
    y = (a_ref[...].astype(F32) * sg_ref[...].astype(F32)).astype(BF16)
    o = _dot(y, wout_ref[...])
    o_ref[...] = _layer_norm(DEEPNORM_ALPHA * x_ref[...] + gate * o, g_ref[...], b_ref[...])


def _na_out(x2d, attn, sg, seq_len, per_sample, mods, layer, w_out, ln_g, ln_b):
    t, d = x2d.shape
    tm = ROW_TILE
    tps = max(seq_len // tm, 1)
    e = w_out.shape[0]
    return pl.pallas_call(
        _na_out_kernel,
        grid=(t // tm,),
        in_specs=[
            pl.BlockSpec((tm, d), lambda i: (i, 0)),
            pl.BlockSpec((tm, e), lambda i: (i, 0)),
            pl.BlockSpec((tm, e), lambda i: (i, 0)),
            _mod_spec(layer, d, tps, per_sample),
            _const_spec(w_out.shape),
            _const_spec((1, d)),
            _const_spec((1, d)),
        ],
        out_specs=pl.BlockSpec((tm, d), lambda i: (i, 0)),
        out_shape=jax.ShapeDtypeStruct((t, d), F32),
        compiler_params=_params(1),
        name="na_out",
    )(x2d, attn, sg, mods, w_out, ln_g.reshape(1, d), ln_b.reshape(1, d))


def kernel(x_prompt, x_sample, c, cache_k, cache_v, c_ctx, w_mod, b_mod, ln_g, ln_b, pool_w_in, pool_w_grp,
           pool_scale, pool_w_out, sgu_w_in, sgu_ln_g, sgu_ln_b, sgu_w_s, sgu_b_s, sgu_w_out, na_w_in, na_rpb,
           na_w_out):
    n_p, seq_p, d = x_prompt.shape
    n_s, seq_s, _ = x_sample.shape
    assert n_s + 1 <= N_COND_ROWS and (n_p * seq_p) % ROW_TILE == 0 and seq_s % ROW_TILE == 0
    assert seq_p % CHUNK == 0 and (seq_p % ROW_TILE == 0 or ROW_TILE % seq_p == 0)
    assert seq_s % (GRID_W * ATT_Q_ROWS) == 0 and seq_s // GRID_W >= ATT_K_ROWS
    conds = jnp.zeros((N_COND_ROWS, d), F32).at[0].set(c_ctx).at[1:1 + n_s].set(c)
    mods = _mods(conds, w_mod, b_mod)
    pool_w_fold, pool_w_gate = _pool_fold(pool_w_in, pool_w_grp)
    pool_w_out_b = pool_w_out.astype(BF16)
    pool_band = _pool_band()

    yp = x_prompt.reshape(n_p * seq_p, d)
    ys = x_sample.reshape(n_s * seq_s, d)
    streams = ((seq_p, False), (seq_s, True))
    ctx_k = ctx_v = None
    for i in range(DEPTH):
        kind, j = i % N_MIXERS, i // N_MIXERS
        ys_in = (yp, ys)
        outs = []
        if kind == 0:
            for x2d, (seq, per_sample) in zip(ys_in, streams):
                outs.append(_pool_layer(x2d, seq, per_sample, mods, i, j, pool_band, pool_w_fold, pool_w_gate,
                                        pool_scale, pool_w_out_b, ln_g[i], ln_b[i]))
        elif kind == 1:
            w_in, w_s, w_out = sgu_w_in[j].astype(BF16), sgu_w_s[j].astype(BF16), sgu_w_out[j].astype(BF16)
            for x2d, (seq, per_sample) in zip(ys_in, streams):
                outs.append(_sgu_layer(x2d, seq, per_sample, mods, i, w_in, sgu_ln_g[j], sgu_ln_b[j], w_s,
                                       sgu_b_s[j].T, w_out, ln_g[i], ln_b[i]))
        else:
            w_in, w_out = na_w_in[j].astype(BF16), na_w_out[j].astype(BF16)
            q, k, v, sg, kf, vf = _na_proj(yp, seq_p, False, mods, i, w_in, True, DH_C ** -0.5)
            ctx_k = kf.reshape(n_p, 1, seq_p, H_C, DH_C)
            ctx_v = vf.reshape(n_p, 1, seq_p, H_C, DH_C)
            attn = _ctx_attn(q, k, v, seq_p)
            outs.append(_na_out(yp, attn, sg, seq_p, False, mods, i, w_out, ln_g[i], ln_b[i]))
            q, k, v, sg = _na_proj(ys, seq_s, True, mods, i, w_in, False, DH_C ** -0.5 * LOG2_E)
            attn = _nbr_attn(q, k, v, cache_k, cache_v, j, na_rpb[j], n_s)
            outs.append(_na_out(ys, attn, sg, seq_s, True, mods, i, w_out, ln_g[i], ln_b[i]))
        yp, ys = outs
    return (yp.reshape(n_p, seq_p, d), ys.reshape(n_s, seq_s, d), ctx_k, ctx_v)
```

```python
import functools

import jax
import jax.numpy as jnp
import numpy as np
from jax import lax
from jax.experimental import pallas as pl
from jax.experimental.pallas import tpu as pltpu

F32 = jnp.float32
BF16 = jnp.bfloat16

DEPTH = 4
N_MIXERS = 3
POOL_WINDOWS = (2, 4, 8, 16)
POOL_HALO = 16
POOL_BLOCK = 128
CHUNK = 128
H_B = 8
H_C = 16
DH_C = 64
GRID_W = 64
WIN_R = 8
WIN_C = 16
DEEPNORM_ALPHA = (2 * DEPTH) ** 0.25
LN_EPS = 1e-5
LOG2_E = float(np.log2(np.e))

N_COND_ROWS = 8
ROW_TILE = 512
ATT_Q_ROWS = 4
ATT_K_ROWS = 12
LANES = 128
VMEM_LIMIT = 56 * 1024 * 1024


def _const_spec(shape):
    nd = len(shape)
    return pl.BlockSpec(shape, lambda *_: (0,) * nd, pipeline_mode=pl.Buffered(1))


def _params(n_axes):
    return pltpu.CompilerParams(dimension_semantics=("arbitrary",) * n_axes, vmem_limit_bytes=VMEM_LIMIT)


def _mod_spec(layer, d, tiles_per_seq, per_sample):
    base = layer * N_COND_ROWS
    if per_sample:
        return pl.BlockSpec((1, 1, 3 * d), lambda i: (base + 1 + i // tiles_per_seq, 0, 0))
    return pl.BlockSpec((1, 1, 3 * d), lambda i: (base, 0, 0))


def _split_mod(mod_ref, d):
    m = mod_ref[0]
    return m[:, :d], m[:, d:2 * d], m[:, 2 * d:]


def _silu(x):
    return x / (1.0 + jnp.exp(-x))


def _gelu_tanh(x):
    c = np.float32(np.sqrt(2.0 / np.pi))
    return x * (0.5 * (1.0 + jnp.tanh(c * (x + 0.044715 * (x * x * x)))))


def _layer_norm(x, g, b):
    mu = jnp.mean(x, axis=-1, keepdims=True)
    d = x - mu
    var = jnp.mean(d * d, axis=-1, keepdims=True)
    return d * lax.rsqrt(var + LN_EPS) * g + b


def _dot(a, b):
    return jnp.dot(a, b, preferred_element_type=F32)


def _dot_nt(a, b):
    return lax.dot_general(a, b, (((1,), (1,)), ((), ())), preferred_element_type=F32)


def _mods_kernel(cond_ref, w_ref, b_ref, o_ref):
    a = _silu(cond_ref[...]).astype(BF16)
    o_ref[0] = _dot(a, w_ref[0].astype(BF16)) + b_ref[0]


def _mods(conds, w_mod, b_mod):
    depth, d, n = w_mod.shape
    tn = 1024
    out = pl.pallas_call(
        _mods_kernel,
        grid=(depth, n // tn),
        in_specs=[
            pl.BlockSpec((N_COND_ROWS, d), lambda l, j: (0, 0)),
            pl.BlockSpec((1, d, tn), lambda l, j: (l, 0, j)),
            pl.BlockSpec((1, 1, tn), lambda l, j: (l, 0, j)),
        ],
        out_specs=pl.BlockSpec((1, N_COND_ROWS, tn), lambda l, j: (l, 0, j)),
        out_shape=jax.ShapeDtypeStruct((depth, N_COND_ROWS, n), F32),
        compiler_params=_params(2),
        name="adaln_mods",
    )(conds, w_mod, b_mod.reshape(depth, 1, n))
    return out.reshape(depth * N_COND_ROWS, 1, n)


def _pool_fold_kernel(win_x_ref, win_g_ref, wgrp_ref, fold_ref, gate_ref):
    fold_ref[0] = _dot(win_x_ref[0].astype(BF16), wgrp_ref[0, 0].astype(BF16)).astype(BF16)
    gate_ref[0] = win_g_ref[0].astype(BF16)


def _pool_fold(w_in, w_grp):
    n_layers, d, two_e = w_in.shape
    n_grp, grp = w_grp.shape[1:3]
    e_dim = two_e // 2
    out = jax.ShapeDtypeStruct((n_layers, d, e_dim), BF16)
    return pl.pallas_call(
        _pool_fold_kernel,
        grid=(n_layers, n_grp),
        in_specs=[
            pl.BlockSpec((1, d, grp), lambda l, g: (l, 0, g)),
            pl.BlockSpec((1, d, grp), lambda l, g: (l, 0, n_grp + g)),
            pl.BlockSpec((1, 1, grp, grp), lambda l, g: (l, g, 0, 0)),
        ],
        out_specs=[pl.BlockSpec((1, d, grp), lambda l, g: (l, 0, g))] * 2,
        out_shape=[out, out],
        compiler_params=_params(2),
        name="pool_fold",
    )(w_in, w_in, w_grp)


def _pool_band():
    t = np.arange(POOL_BLOCK)[:, None] + POOL_HALO
    e = np.arange(POOL_BLOCK + 2 * POOL_HALO)[None, :]
    return jnp.asarray(np.stack([(e >= t - w // 2) & (e <= t + w // 2 - 1) for w in POOL_WINDOWS]), BF16)


def _pool_kernel(tiles_per_seq, seq_len, x_ref, xp_ref, xn_ref, mod_ref, band_ref, wfold_ref, wgate_ref, psc_ref,
                 wout_ref, g_ref, b_ref, o_ref, e_ref, u_ref):
    tm, d = x_ref.shape
    e_dim = psc_ref.shape[1]
    n_win = len(POOL_WINDOWS)
    grp = e_dim // n_win
    it = pl.program_id(0) % tiles_per_seq
    shift, scale, gate = _split_mod(mod_ref, d)
    x = x_ref[...]
    hb = (x * (1.0 + scale) + shift).astype(BF16)
    hp = (xp_ref[...] * (1.0 + scale) + shift).astype(BF16)
    hn = (xn_ref[...] * (1.0 + scale) + shift).astype(BF16)
    zeros = jnp.zeros((POOL_HALO, d), BF16)
    e_ref[0:POOL_HALO] = jnp.where(it != 0, hp, zeros)
    e_ref[POOL_HALO:POOL_HALO + tm] = hb
    e_ref[POOL_HALO + tm:2 * POOL_HALO + tm] = jnp.where(it != tiles_per_seq - 1, hn, zeros)
    u_ref[...] = _dot(e_ref[...], wfold_ref[...]).astype(BF16)

    t = it * tm + lax.broadcasted_iota(jnp.int32, (tm, 1), 0)
    mixed = []
    for gi, w in enumerate(POOL_WINDOWS):
        cols = slice(gi * grp, (gi + 1) * grp)
        lo = jnp.maximum(t - w // 2, 0)
        hi = jnp.minimum(t + w // 2 - 1, seq_len - 1)
        inv_cnt = 1.0 / (hi - lo + 1).astype(F32)
        sums = jnp.concatenate(
            [_dot(band_ref[gi], u_ref[rb * POOL_BLOCK:(rb + 1) * POOL_BLOCK + 2 * POOL_HALO, cols])
             for rb in range(tm // POOL_BLOCK)], axis=0)
        mixed.append(sums * inv_cnt - u_ref[POOL_HALO:POOL_HALO + tm, cols].astype(F32))
    mixed = jnp.concatenate(mixed, axis=1)
    gate_pre = _dot(hb, wgate_ref[...])
    y = (mixed * psc_ref[...] * _silu(gate_pre)).astype(BF16)
    o = _dot(y, wout_ref[...])
    o_ref[...] = _layer_norm(DEEPNORM_ALPHA * x + gate * o, g_ref[...], b_ref[...])


def _pool_layer(x2d, seq_len, per_sample, mods, layer, j, band, w_fold, w_gate, p_scale, w_out, ln_g, ln_b):
    t, d = x2d.shape
    tm = min(ROW_TILE, seq_len)
    tps = seq_len // tm
    e_dim = w_out.shape[1]
    hb = tm // POOL_HALO
    last = t // POOL_HALO - 1

    def layer_spec(shape):
        return pl.BlockSpec((None,) + shape, lambda i: (j,) + (0,) * len(shape), pipeline_mode=pl.Buffered(1))

    return pl.pallas_call(
        functools.partial(_pool_kernel, tps, seq_len),
        grid=(t // tm,),
        in_specs=[
            pl.BlockSpec((tm, d), lambda i: (i, 0)),
            pl.BlockSpec((POOL_HALO, d), lambda i: (jnp.maximum(i * hb - 1, 0), 0)),
            pl.BlockSpec((POOL_HALO, d), lambda i: (jnp.minimum((i + 1) * hb, last), 0)),
            _mod_spec(layer, d, tps, per_sample),
            _const_spec(band.shape),
            layer_spec((d, e_dim)),
            layer_spec((d, e_dim)),
            layer_spec((1, e_dim)),
            layer_spec((e_dim, d)),
            _const_spec((1, d)),
            _const_spec((1, d)),
        ],
        out_specs=pl.BlockSpec((tm, d), lambda i: (i, 0)),
        out_shape=jax.ShapeDtypeStruct((t, d), F32),
        scratch_shapes=[pltpu.VMEM((tm + 2 * POOL_HALO, d), BF16),
                        pltpu.VMEM((tm + 2 * POOL_HALO, e_dim), BF16)],
        compiler_params=_params(1),
        name="pool_layer",
    )(x2d, x2d, x2d, mods, band, w_fold, w_gate, p_scale.reshape(-1, 1, e_dim), w_out,
      ln_g.reshape(1, d), ln_b.reshape(1, d))


def _sgu_kernel(x_ref, mod_ref, win_ref, lg_ref, lb_ref, ws_ref, bs_ref, wout_ref, g_ref, b_ref, o_ref, y_ref):
    tm, d = x_ref.shape
    e_dim = lg_ref.shape[1]
    dh = e_dim // H_B
    shift, scale, gate = _split_mod(mod_ref, d)
    x = x_ref[...]
    hb = (x * (1.0 + scale) + shift).astype(BF16)
    u = _gelu_tanh(_dot(hb, win_ref[:, 0:e_dim]))
    v = _gelu_tanh(_dot(hb, win_ref[:, e_dim:2 * e_dim]))
    v = _layer_norm(v, lg_ref[...], lb_ref[...]).astype(BF16)
    ug = u * _silu(_dot(hb, win_ref[:, 2 * e_dim:3 * e_dim]))
    for c in range(tm // CHUNK):
        rows = slice(c * CHUNK, (c + 1) * CHUNK)
        for hh in range(H_B):
            cols = slice(hh * dh, (hh + 1) * dh)
            sv = _dot(ws_ref[hh], v[rows, cols]) + bs_ref[:, hh:hh + 1]
            y_ref[rows, cols] = (ug[rows, cols] * sv).astype(BF16)
    o = _dot(y_ref[...], wout_ref[...])
    o_ref[...] = _layer_norm(DEEPNORM_ALPHA * x + gate * o, g_ref[...], b_ref[...])


def _sgu_layer(x2d, seq_len, per_sample, mods, layer, w_in, sln_g, sln_b, w_s, b_s_t, w_out, ln_g, ln_b):
    t, d = x2d.shape
    tm = ROW_TILE
    tps = max(seq_len // tm, 1)
    e_dim = w_out.shape[0]
    return pl.pallas_call(
        _sgu_kernel,
        grid=(t // tm,),
        in_specs=[
            pl.BlockSpec((tm, d), lambda i: (i, 0)),
            _mod_spec(layer, d, tps, per_sample),
            _const_spec(w_in.shape),
            _const_spec((1, e_dim)),
            _const_spec((1, e_dim)),
            _const_spec(w_s.shape),
            _const_spec(b_s_t.shape),
            _const_spec(w_out.shape),
            _const_spec((1, d)),
            _const_spec((1, d)),
        ],
        out_specs=pl.BlockSpec((tm, d), lambda i: (i, 0)),
        out_shape=jax.ShapeDtypeStruct((t, d), F32),
        scratch_shapes=[pltpu.VMEM((tm, e_dim), BF16)],
        compiler_params=_params(1),
        name="sgu_layer",
    )(x2d, mods, w_in, sln_g.reshape(1, e_dim), sln_b.reshape(1, e_dim), w_s, b_s_t, w_out,
      ln_g.reshape(1, d), ln_b.reshape(1, d))


def _na_proj_kernel(emit_f32_kv, q_scale, x_ref, mod_ref, win_ref, *out_refs):
    tm, d = x_ref.shape
    e = win_ref.shape[1] // 4
    shift, scale, _ = _split_mod(mod_ref, d)
    hb = (x_ref[...] * (1.0 + scale) + shift).astype(BF16)
    q_ref, k_ref, v_ref, sg_ref = out_refs[:4]
    q_ref[...] = (_dot(hb, win_ref[:, 0:e]) * q_scale).astype(BF16)
    k = _dot(hb, win_ref[:, e:2 * e])
    v = _dot(hb, win_ref[:, 2 * e:3 * e])
    k_ref[...] = k.astype(BF16)
    v_ref[...] = v.astype(BF16)
    sg_ref[...] = _silu(_dot(hb, win_ref[:, 3 * e:4 * e])).astype(BF16)
    if emit_f32_kv:
        for kv, ref in ((k, out_refs[4]), (v, out_refs[5])):
            for h in range(H_C):
                ref[pl.ds(h, tm, stride=H_C), :] = kv[:, h * DH_C:(h + 1) * DH_C]


def _na_proj(x2d, seq_len, per_sample, mods, layer, w_in, emit_f32_kv, q_scale):
    t, d = x2d.shape
    tm = ROW_TILE
    tps = max(seq_len // tm, 1)
    e = w_in.shape[1] // 4
    row_spec = pl.BlockSpec((tm, e), lambda i: (i, 0))
    out_specs = [row_spec] * 4
    out_shape = [jax.ShapeDtypeStruct((t, e), BF16)] * 4
    if emit_f32_kv:
        out_specs += [pl.BlockSpec((tm * H_C, DH_C), lambda i: (i, 0))] * 2
        out_shape += [jax.ShapeDtypeStruct((t * H_C, DH_C), F32)] * 2
    return pl.pallas_call(
        functools.partial(_na_proj_kernel, emit_f32_kv, q_scale),
        grid=(t // tm,),
        in_specs=[
            pl.BlockSpec((tm, d), lambda i: (i, 0)),
            _mod_spec(layer, d, tps, per_sample),
            _const_spec(w_in.shape),
        ],
        out_specs=out_specs,
        out_shape=out_shape,
        compiler_params=_params(1),
        name="na_proj",
    )(x2d, mods, w_in)


def _lane_is_first_head():
    return lax.broadcasted_iota(jnp.int32, (1, LANES), 1) < DH_C


def _ctx_attn_kernel(q_ref, k_ref, v_ref, o_ref):
    first = _lane_is_first_head()
    for j in range(q_ref.shape[1] // LANES):
        cols = slice(j * LANES, (j + 1) * LANES)
        q, k, v = q_ref[:, cols], k_ref[:, cols], v_ref[:, cols]
        outs = []
        for sel in (first, jnp.logical_not(first)):
            s = _dot_nt(jnp.where(sel, q, jnp.zeros_like(q)), k)
            p = jnp.exp(s - jnp.max(s, axis=-1, keepdims=True))
            l = jnp.sum(p, axis=-1, keepdims=True)
            outs.append(_dot(p.astype(BF16), v) / l)
        o_ref[:, cols] = jnp.where(first, outs[0], outs[1]).astype(BF16)


def _ctx_attn(q, k, v, seq_len):
    t, e = q.shape
    spec = pl.BlockSpec((seq_len, e), lambda b: (b, 0))
    return pl.pallas_call(
        _ctx_attn_kernel,
        grid=(t // seq_len,),
        in_specs=[spec, spec, spec],
        out_specs=spec,
        out_shape=jax.ShapeDtypeStruct((t, e), BF16),
        compiler_params=_params(1),
        name="ctx_attn",
    )(q, k, v)


def _key_row_start(r_blk, rows):
    return jnp.clip(r_blk * ATT_Q_ROWS - WIN_R // 2, 0, rows - ATT_K_ROWS)


def _nbr_bias_build(rows, rpb_ref, bias_ref):
    n_blk = rows // ATT_Q_ROWS
    qc = lax.broadcasted_iota(jnp.int32, (GRID_W, LANES), 0)
    lane = lax.broadcasted_iota(jnp.int32, (GRID_W, LANES), 1)
    kc = lane & (GRID_W - 1)
    c0 = jnp.clip(qc - WIN_C // 2, 0, GRID_W - WIN_C)
    col_in = (kc >= c0) & (kc < c0 + WIN_C)
    first = lane < GRID_W
    neg = jnp.full((GRID_W, LANES), -jnp.inf, F32)
    for hh in range(2):
        lo, hi = [], []
        for dr in range(2 * WIN_R - 1):
            row = jnp.broadcast_to(rpb_ref[hh, dr:dr + 1, :] * LOG2_E, (GRID_W, LANES))
            lo.append(pltpu.roll(row, LANES - (WIN_C - 1), 1, stride=1, stride_axis=0))
            hi.append(pltpu.roll(row, GRID_W - (WIN_C - 1), 1, stride=1, stride_axis=0))
        for var, r_blk in enumerate((0, 1, n_blk - 1)):
            ks = min(max(r_blk * ATT_Q_ROWS - WIN_R // 2, 0), rows - ATT_K_ROWS)
            for qi in range(ATT_Q_ROWS):
                r = r_blk * ATT_Q_ROWS + qi
                r0 = min(max(r - WIN_R // 2, 0), rows - WIN_R)
                for kp in range(ATT_K_ROWS // 2):
                    halves = []
                    for half, src in enumerate((lo, hi)):
                        kr = ks + 2 * kp + half
                        halves.append(src[kr - r + WIN_R - 1] if r0 <= kr < r0 + WIN_R else neg)
                    blk = jnp.where(col_in, jnp.where(first, halves[0], halves[1]), neg)
                    bias_ref[hh, var, qi * GRID_W:(qi + 1) * GRID_W, kp * LANES:(kp + 1) * LANES] = blk


def _nbr_attn_kernel(rows, rpb_ref, q_ref, k_ref, v_ref, ck_ref, cv_ref, o_ref, bias_ref, kc_ref, vx_ref, s_ref,
                     p_ref, oa_ref):
    n_blk = rows // ATT_Q_ROWS
    tq = ATT_Q_ROWS * GRID_W
    nk = ATT_K_ROWS * GRID_W
    seq = q_ref.shape[0]

    @pl.when(pl.program_id(1) == 0)
    def _():
        _nbr_bias_build(rows, rpb_ref, bias_ref)

    first = _lane_is_first_head()
    head_lanes = (first, jnp.logical_not(first))
    kc_ref[...] = ck_ref[...].astype(BF16)
    vc = cv_ref[...].T.astype(BF16)
    for hh in range(2):
        vx_ref[hh, 0:seq] = jnp.where(head_lanes[hh], v_ref[...], jnp.ones(v_ref.shape, BF16))
        vx_ref[hh, seq:] = jnp.where(head_lanes[hh], vc, jnp.ones(vc.shape, BF16))

    def q_rows(r):
        return pl.ds(pl.multiple_of(r * tq, tq), tq)

    def k_start(r):
        return pl.multiple_of(_key_row_start(r, rows) * GRID_W, GRID_W)

    def logits(r, hh):
        q = q_ref[q_rows(r), :]
        qh = jnp.where(head_lanes[hh], q, jnp.zeros_like(q))
        var = jnp.where(r == 0, 0, jnp.where(r == n_blk - 1, 2, 1))
        s_ref[hh, :, 0:nk] = _dot_nt(qh, k_ref[pl.ds(k_start(r), nk), :]) + bias_ref[hh, var]
        s_ref[hh, :, nk:] = _dot(qh, kc_ref[...])

    def softmax(hh):
        s = s_ref[hh]
        p_ref[hh] = jnp.exp2((s - jnp.max(s, axis=-1, keepdims=True)).astype(BF16))

    def weighted_values(r, hh):
        return (_dot(p_ref[hh, :, 0:nk], vx_ref[hh, pl.ds(k_start(r), nk), :])
                + _dot(p_ref[hh, :, nk:], vx_ref[hh, seq:, :]))

    def emit(r, o_second):
        o_first = oa_ref[...]
        num = jnp.where(first, o_first, o_second)
        den = pltpu.roll(jnp.where(first, o_second, o_first), DH_C, 1)
        o_ref[q_rows(r), :] = (num / den).astype(BF16)

    p_ref[1] = jnp.ones(p_ref.shape[1:], BF16)
    oa_ref[...] = jnp.ones(oa_ref.shape, F32)
    logits(0, 0)

    def block(r, carry):
        prev = jnp.maximum(r - 1, 0)
        emit(prev, weighted_values(prev, 1))
        logits(r, 1)
        softmax(0)
        oa_ref[...] = weighted_values(r, 0)
        logits(jnp.minimum(r + 1, n_blk - 1), 0)
        softmax(1)
        return carry

    lax.fori_loop(0, n_blk, block, 0, unroll=2)
    emit(n_blk - 1, weighted_values(n_blk - 1, 1))


def _nbr_attn(q, k, v, cache_k, cache_v, cache_layer, rpb, n_batch):
    t, e = q.shape
    seq = t // n_batch
    rows = seq // GRID_W
    n_cached, past = cache_k.shape[1:3]
    ck = jnp.transpose(cache_k, (0, 1, 3, 4, 2)).reshape(n_batch * n_cached, e, past)
    cv = jnp.transpose(cache_v, (0, 1, 3, 4, 2)).reshape(n_batch * n_cached, e, past)
    tq, nk = ATT_Q_ROWS * GRID_W, ATT_K_ROWS * GRID_W
    n_dr, n_dc = rpb.shape[1:]
    rpb_pad = jnp.pad(rpb.astype(F32), ((0, 0), (0, 2 * WIN_R - n_dr), (0, LANES - n_dc)))
    tok_spec = pl.BlockSpec((seq, LANES), lambda j, b: (b, j))
    ctx_spec = pl.BlockSpec((None, LANES, past), lambda j, b: (b * n_cached + cache_layer, j, 0))
    return pl.pallas_call(
        functools.partial(_nbr_attn_kernel, rows),
        grid=(e // LANES, n_batch),
        in_specs=[pl.BlockSpec((2, 2 * WIN_R, LANES), lambda j, b: (j, 0, 0)),
                  tok_spec, tok_spec, tok_spec, ctx_spec, ctx_spec],
        out_specs=tok_spec,
        out_shape=jax.ShapeDtypeStruct((t, e), BF16),
        scratch_shapes=[
            pltpu.VMEM((2, 3, tq, nk), F32),
            pltpu.VMEM((LANES, past), BF16),
            pltpu.VMEM((2, seq + past, LANES), BF16),
            pltpu.VMEM((2, tq, nk + past), F32),
            pltpu.VMEM((2, tq, nk + past), BF16),
            pltpu.VMEM((tq, LANES), F32),
        ],
        compiler_params=_params(2),
        name="nbr_attn",
    )(rpb_pad, q, k, v, ck, cv)


def _na_out_kernel(x_ref, a_ref, sg_ref, mod_ref, wout_ref, g_ref, b_ref, o_ref):
    tm, d = x_ref.shape
    _, _, gate = _split_mod(mod_ref, d)
    y = (a_ref[...].astype(F32) * sg_ref[...].astype(F32)).astype(BF16)
    o = _dot(y, wout_ref[...])
    o_ref[...] = _layer_norm(DEEPNORM_ALPHA * x_ref[...] + gate * o, g_ref[...], b_ref[...])


def _na_out(x2d, attn, sg, seq_len, per_sample, mods, layer, w_out, ln_g, ln_b):
    t, d = x2d.shape
    tm = ROW_TILE
    tps = max(seq_len // tm, 1)
    e = w_out.shape[0]
    return pl.pallas_call(
        _na_out_kernel,
        grid=(t // tm,),
        in_specs=[
            pl.BlockSpec((tm, d), lambda i: (i, 0)),
            pl.BlockSpec((tm, e), lambda i: (i, 0)),
            pl.BlockSpec((tm, e), lambda i: (i, 0)),
            _mod_spec(layer, d, tps, per_sample),
            _const_spec(w_out.shape),
            _const_spec((1, d)),
            _const_spec((1, d)),
        ],
        out_specs=pl.BlockSpec((tm, d), lambda i: (i, 0)),
        out_shape=jax.ShapeDtypeStruct((t, d), F32),
        compiler_params=_params(1),
        name="na_out",
    )(x2d, attn, sg, mods, w_out, ln_g.reshape(1, d), ln_b.reshape(1, d))


def kernel(x_prompt, x_sample, c, cache_k, cache_v, c_ctx, w_mod, b_mod, ln_g, ln_b, pool_w_in, pool_w_grp,
           pool_scale, pool_w_out, sgu_w_in, sgu_ln_g, sgu_ln_b, sgu_w_s, sgu_b_s, sgu_w_out, na_w_in, na_rpb,
           na_w_out):
    n_p, seq_p, d = x_prompt.shape
    n_s, seq_s, _ = x_sample.shape
    assert n_s + 1 <= N_COND_ROWS and (n_p * seq_p) % ROW_TILE == 0 and seq_s % ROW_TILE == 0
    assert seq_p % CHUNK == 0 and (seq_p % ROW_TILE == 0 or ROW_TILE % seq_p == 0)
    assert seq_s % (GRID_W * ATT_Q_ROWS) == 0 and seq_s // GRID_W >= ATT_K_ROWS
    conds = jnp.zeros((N_COND_ROWS, d), F32).at[0].set(c_ctx).at[1:1 + n_s].set(c)
    mods = _mods(conds, w_mod, b_mod)
    pool_w_fold, pool_w_gate = _pool_fold(pool_w_in, pool_w_grp)
    pool_w_out_b = pool_w_out.astype(BF16)
    pool_band = _pool_band()

    yp = x_prompt.reshape(n_p * seq_p, d)
    ys = x_sample.reshape(n_s * seq_s, d)
    streams = ((seq_p, False), (seq_s, True))
    ctx_k = ctx_v = None
    for i in range(DEPTH):
        kind, j = i % N_MIXERS, i // N_MIXERS
        ys_in = (yp, ys)
        outs = []
        if kind == 0:
            for x2d, (seq, per_sample) in zip(ys_in, streams):
                outs.append(_pool_layer(x2d, seq, per_sample, mods, i, j, pool_band, pool_w_fold, pool_w_gate,
                                        pool_scale, pool_w_out_b, ln_g[i], ln_b[i]))
        elif kind == 1:
            w_in, w_s, w_out = sgu_w_in[j].astype(BF16), sgu_w_s[j].astype(BF16), sgu_w_out[j].astype(BF16)
            for x2d, (seq, per_sample) in zip(ys_in, streams):
                outs.append(_sgu_layer(x2d, seq, per_sample, mods, i, w_in, sgu_ln_g[j], sgu_ln_b[j], w_s,
                                       sgu_b_s[j].T, w_out, ln_g[i], ln_b[i]))
        else:
            w_in, w_out = na_w_in[j].astype(BF16), na_w_out[j].astype(BF16)
            q, k, v, sg, kf, vf = _na_proj(yp, seq_p, False, mods, i, w_in, True, DH_C ** -0.5)
            ctx_k = kf.reshape(n_p, 1, seq_p, H_C, DH_C)
            ctx_v = vf.reshape(n_p, 1, seq_p, H_C, DH_C)
            attn = _ctx_attn(q, k, v, seq_p)
            outs.append(_na_out(yp, attn, sg, seq_p, False, mods, i, w_out, ln_g[i], ln_b[i]))
            q, k, v, sg = _na_proj(ys, seq_s, True, mods, i, w_in, False, DH_C ** -0.5 * LOG2_E)
            attn = _nbr_attn(q, k, v, cache_k, cache_v, j, na_rpb[j], n_s)
            outs.append(_na_out(ys, attn, sg, seq_s, True, mods, i, w_out, ln_g[i], ln_b[i]))
        yp, ys = outs
    return (yp.reshape(n_p, seq_p, d), ys.reshape(n_s, seq_s, d), ctx_k, ctx_v)
```

```python
import functools

import jax
import jax.numpy as jnp
import numpy as np
from jax import lax
from jax.experimental import pallas as pl
from jax.experimental.pallas import tpu as pltpu

F32 = jnp.float32
BF16 = jnp.bfloat16

DEPTH = 4
N_MIXERS = 3
POOL_WINDOWS = (2, 4, 8, 16)
POOL_HALO = 16
POOL_BLOCK = 128
CHUNK = 128
H_B = 8
H_C = 16
DH_C = 64
GRID_W = 64
WIN_R = 8
WIN_C = 16
DEEPNORM_ALPHA = (2 * DEPTH) ** 0.25
LN_EPS = 1e-5
LOG2_E = float(np.log2(np.e))

N_COND_ROWS = 8
ROW_TILE = 512
ATT_Q_ROWS = 4
ATT_K_ROWS = 12
LANES = 128
MXU_TILE = 256
VMEM_LIMIT = 56 * 1024 * 1024


def _const_spec(shape):
    nd = len(shape)
    return pl.BlockSpec(shape, lambda *_: (0,) * nd, pipeline_mode=pl.Buffered(1))


def _params(n_axes):
    return pltpu.CompilerParams(dimension_semantics=("arbitrary",) * n_axes, vmem_limit_bytes=VMEM_LIMIT)


def _mod_spec(layer, d, tiles_per_seq, per_sample):
    base = layer * N_COND_ROWS
    if per_sample:
        return pl.BlockSpec((1, 1, 3 * d), lambda i: (base + 1 + i // tiles_per_seq, 0, 0))
    return pl.BlockSpec((1, 1, 3 * d), lambda i: (base, 0, 0))


def _split_mod(mod_ref, d):
    m = mod_ref[0]
    return m[:, :d], m[:, d:2 * d], m[:, 2 * d:]


def _silu(x):
    return x / (1.0 + jnp.exp(-x))


def _gelu_tanh(x):
    c = np.float32(np.sqrt(2.0 / np.pi))
    return x * (0.5 * (1.0 + jnp.tanh(c * (x + 0.044715 * (x * x * x)))))


def _layer_norm(x, g, b):
    mu = jnp.mean(x, axis=-1, keepdims=True)
    d = x - mu
    var = jnp.mean(d * d, axis=-1, keepdims=True)
    return d * lax.rsqrt(var + LN_EPS) * g + b


def _dot(a, b):
    return jnp.dot(a, b, preferred_element_type=F32)


def _dot_nt(a, b):
    return lax.dot_general(a, b, (((1,), (1,)), ((), ())), preferred_element_type=F32)


def _mods_kernel(cond_ref, w_ref, b_ref, o_ref):
    a = _silu(cond_ref[...]).astype(BF16)
    o_ref[0] = _dot(a, w_ref[0].astype(BF16)) + b_ref[0]


def _mods(conds, w_mod, b_mod):
    depth, d, n = w_mod.shape
    tn = 1024
    out = pl.pallas_call(
        _mods_kernel,
        grid=(depth, n // tn),
        in_specs=[
            pl.BlockSpec((N_COND_ROWS, d), lambda l, j: (0, 0)),
            pl.BlockSpec((1, d, tn), lambda l, j: (l, 0, j)),
            pl.BlockSpec((1, 1, tn), lambda l, j: (l, 0, j)),
        ],
        out_specs=pl.BlockSpec((1, N_COND_ROWS, tn), lambda l, j: (l, 0, j)),
        out_shape=jax.ShapeDtypeStruct((depth, N_COND_ROWS, n), F32),
        compiler_params=_params(2),
        name="adaln_mods",
    )(conds, w_mod, b_mod.reshape(depth, 1, n))
    return out.reshape(depth * N_COND_ROWS, 1, n)


def _pool_fold_kernel(win_x_ref, win_g_ref, wgrp_ref, fold_ref, gate_ref):
    fold_ref[0] = _dot(win_x_ref[0].astype(BF16), wgrp_ref[0, 0].astype(BF16)).astype(BF16)
    gate_ref[0] = win_g_ref[0].astype(BF16)


def _pool_fold(w_in, w_grp):
    n_layers, d, two_e = w_in.shape
    n_grp, grp = w_grp.shape[1:3]
    e_dim = two_e // 2
    out = jax.ShapeDtypeStruct((n_layers, d, e_dim), BF16)
    return pl.pallas_call(
        _pool_fold_kernel,
        grid=(n_layers, n_grp),
        in_specs=[
            pl.BlockSpec((1, d, grp), lambda l, g: (l, 0, g)),
            pl.BlockSpec((1, d, grp), lambda l, g: (l, 0, n_grp + g)),
            pl.BlockSpec((1, 1, grp, grp), lambda l, g: (l, g, 0, 0)),
        ],
        out_specs=[pl.BlockSpec((1, d, grp), lambda l, g: (l, 0, g))] * 2,
        out_shape=[out, out],
        compiler_params=_params(2),
        name="pool_fold",
    )(w_in, w_in, w_grp)


def _pool_band():
    t = np.arange(POOL_BLOCK)[:, None] + POOL_HALO
    e = np.arange(POOL_BLOCK + 2 * POOL_HALO)[None, :]
    return jnp.asarray(np.stack([(e >= t - w // 2) & (e <= t + w // 2 - 1) for w in POOL_WINDOWS]), BF16)


def _pool_kernel(tiles_per_seq, seq_len, x_ref, xp_ref, xn_ref, mod_ref, band_ref, wfold_ref, wgate_ref, psc_ref,
                 wout_ref, g_ref, b_ref, o_ref, e_ref, u_ref):
    tm, d = x_ref.shape
    e_dim = psc_ref.shape[1]
    n_win = len(POOL_WINDOWS)
    grp = e_dim // n_win
    it = pl.program_id(0) % tiles_per_seq
    shift, scale, gate = _split_mod(mod_ref, d)
    x = x_ref[...]
    hb = (x * (1.0 + scale) + shift).astype(BF16)
    hp = (xp_ref[...] * (1.0 + scale) + shift).astype(BF16)
    hn = (xn_ref[...] * (1.0 + scale) + shift).astype(BF16)
    zeros = jnp.zeros((POOL_HALO, d), BF16)
    e_ref[0:POOL_HALO] = jnp.where(it != 0, hp, zeros)
    e_ref[POOL_HALO:POOL_HALO + tm] = hb
    e_ref[POOL_HALO + tm:2 * POOL_HALO + tm] = jnp.where(it != tiles_per_seq - 1, hn, zeros)
    u_ref[...] = _dot(e_ref[...], wfold_ref[...]).astype(BF16)

    t = it * tm + lax.broadcasted_iota(jnp.int32, (tm, 1), 0)
    mixed = []
    for gi, w in enumerate(POOL_WINDOWS):
        cols = slice(gi * grp, (gi + 1) * grp)
        lo = jnp.maximum(t - w // 2, 0)
        hi = jnp.minimum(t + w // 2 - 1, seq_len - 1)
        inv_cnt = 1.0 / (hi - lo + 1).astype(F32)
        sums = jnp.concatenate(
            [_dot(band_ref[gi], u_ref[rb * POOL_BLOCK:(rb + 1) * POOL_BLOCK + 2 * POOL_HALO, cols])
             for rb in range(tm // POOL_BLOCK)], axis=0)
        mixed.append(sums * inv_cnt - u_ref[POOL_HALO:POOL_HALO + tm, cols].astype(F32))
    mixed = jnp.concatenate(mixed, axis=1)
    gate_pre = _dot(hb, wgate_ref[...])
    y = (mixed * psc_ref[...] * _silu(gate_pre)).astype(BF16)
    o = _dot(y, wout_ref[...])
    o_ref[...] = _layer_norm(DEEPNORM_ALPHA * x + gate * o, g_ref[...], b_ref[...])


def _pool_layer(x2d, seq_len, per_sample, mods, layer, j, band, w_fold, w_gate, p_scale, w_out, ln_g, ln_b):
    t, d = x2d.shape
    tm = min(ROW_TILE, seq_len)
    tps = seq_len // tm
    e_dim = w_out.shape[1]
    hb = tm // POOL_HALO
    last = t // POOL_HALO - 1

    def layer_spec(shape):
        return pl.BlockSpec((None,) + shape, lambda i: (j,) + (0,) * len(shape), pipeline_mode=pl.Buffered(1))

    return pl.pallas_call(
        functools.partial(_pool_kernel, tps, seq_len),
        grid=(t // tm,),
        in_specs=[
            pl.BlockSpec((tm, d), lambda i: (i, 0)),
            pl.BlockSpec((POOL_HALO, d), lambda i: (jnp.maximum(i * hb - 1, 0), 0)),
            pl.BlockSpec((POOL_HALO, d), lambda i: (jnp.minimum((i + 1) * hb, last), 0)),
            _mod_spec(layer, d, tps, per_sample),
            _const_spec(band.shape),
            layer_spec((d, e_dim)),
            layer_spec((d, e_dim)),
            layer_spec((1, e_dim)),
            layer_spec((e_dim, d)),
            _const_spec((1, d)),
            _const_spec((1, d)),
        ],
        out_specs=pl.BlockSpec((tm, d), lambda i: (i, 0)),
        out_shape=jax.ShapeDtypeStruct((t, d), F32),
        scratch_shapes=[pltpu.VMEM((tm + 2 * POOL_HALO, d), BF16),
                        pltpu.VMEM((tm + 2 * POOL_HALO, e_dim), BF16)],
        compiler_params=_params(1),
        name="pool_layer",
    )(x2d, x2d, x2d, mods, band, w_fold, w_gate, p_scale.reshape(-1, 1, e_dim), w_out,
      ln_g.reshape(1, d), ln_b.reshape(1, d))


def _sgu_kernel(x_ref, mod_ref, win_ref, lg_ref, lb_ref, ws_ref, bs_ref, wout_ref, g_ref, b_ref, o_ref, y_ref):
    tm, d = x_ref.shape
    e_dim = lg_ref.shape[1]
    dh = e_dim // H_B
    shift, scale, gate = _split_mod(mod_ref, d)
    x = x_ref[...]
    hb = (x * (1.0 + scale) + shift).astype(BF16)
    u = _gelu_tanh(_dot(hb, win_ref[:, 0:e_dim]))
    v = _gelu_tanh(_dot(hb, win_ref[:, e_dim:2 * e_dim]))
    v = _layer_norm(v, lg_ref[...], lb_ref[...]).astype(BF16)
    ug = u * _silu(_dot(hb, win_ref[:, 2 * e_dim:3 * e_dim]))
    for c in range(tm // CHUNK):
        rows = slice(c * CHUNK, (c + 1) * CHUNK)
        for hh in range(H_B):
            cols = slice(hh * dh, (hh + 1) * dh)
            sv = _dot(ws_ref[hh], v[rows, cols]) + bs_ref[:, hh:hh + 1]
            y_ref[rows, cols] = (ug[rows, cols] * sv).astype(BF16)
    o = _dot(y_ref[...], wout_ref[...])
    o_ref[...] = _layer_norm(DEEPNORM_ALPHA * x + gate * o, g_ref[...], b_ref[...])


def _sgu_layer(x2d, seq_len, per_sample, mods, layer, w_in, sln_g, sln_b, w_s, b_s_t, w_out, ln_g, ln_b):
    t, d = x2d.shape
    tm = ROW_TILE
    tps = max(seq_len // tm, 1)
    e_dim = w_out.shape[0]
    return pl.pallas_call(
        _sgu_kernel,
        grid=(t // tm,),
        in_specs=[
            pl.BlockSpec((tm, d), lambda i: (i, 0)),
            _mod_spec(layer, d, tps, per_sample),
            _const_spec(w_in.shape),
            _const_spec((1, e_dim)),
            _const_spec((1, e_dim)),
            _const_spec(w_s.shape),
            _const_spec(b_s_t.shape),
            _const_spec(w_out.shape),
            _const_spec((1, d)),
            _const_spec((1, d)),
        ],
        out_specs=pl.BlockSpec((tm, d), lambda i: (i, 0)),
        out_shape=jax.ShapeDtypeStruct((t, d), F32),
        scratch_shapes=[pltpu.VMEM((tm, e_dim), BF16)],
        compiler_params=_params(1),
        name="sgu_layer",
    )(x2d, mods, w_in, sln_g.reshape(1, e_dim), sln_b.reshape(1, e_dim), w_s, b_s_t, w_out,
      ln_g.reshape(1, d), ln_b.reshape(1, d))


def _na_proj_kernel(emit_f32_kv, q_scale, x_ref, mod_ref, win_ref, *out_refs):
    tm, d = x_ref.shape
    e = win_ref.shape[1] // 4
    shift, scale, _ = _split_mod(mod_ref, d)
    hb = (x_ref[...] * (1.0 + scale) + shift).astype(BF16)
    q_ref, k_ref, v_ref, sg_ref = out_refs[:4]
    q_ref[...] = (_dot(hb, win_ref[:, 0:e]) * q_scale).astype(BF16)
    k = _dot(hb, win_ref[:, e:2 * e])
    v = _dot(hb, win_ref[:, 2 * e:3 * e])
    k_ref[...] = k.astype(BF16)
    v_ref[...] = v.astype(BF16)
    sg_ref[...] = _silu(_dot(hb, win_ref[:, 3 * e:4 * e])).astype(BF16)
    if emit_f32_kv:
        for kv, ref in ((k, out_refs[4]), (v, out_refs[5])):
            for h in range(H_C):
                ref[pl.ds(h, tm, stride=H_C), :] = kv[:, h * DH_C:(h + 1) * DH_C]


def _na_proj(x2d, seq_len, per_sample, mods, layer, w_in, emit_f32_kv, q_scale):
    t, d = x2d.shape
    tm = ROW_TILE
    tps = max(seq_len // tm, 1)
    e = w_in.shape[1] // 4
    row_spec = pl.BlockSpec((tm, e), lambda i: (i, 0))
    out_specs = [row_spec] * 4
    out_shape = [jax.ShapeDtypeStruct((t, e), BF16)] * 4
    if emit_f32_kv:
        out_specs += [pl.BlockSpec((tm * H_C, DH_C), lambda i: (i, 0))] * 2
        out_shape += [jax.ShapeDtypeStruct((t * H_C, DH_C), F32)] * 2
    return pl.pallas_call(
        functools.partial(_na_proj_kernel, emit_f32_kv, q_scale),
        grid=(t // tm,),
        in_specs=[
            pl.BlockSpec((tm, d), lambda i: (i, 0)),
            _mod_spec(layer, d, tps, per_sample),
            _const_spec(w_in.shape),
        ],
        out_specs=out_specs,
        out_shape=out_shape,
        compiler_params=_params(1),
        name="na_proj",
    )(x2d, mods, w_in)


def _lane_is_first_head():
    return lax.broadcasted_iota(jnp.int32, (1, LANES), 1) < DH_C


def _ctx_attn_kernel(q_ref, k_ref, v_ref, o_ref):
    first = _lane_is_first_head()
    for j in range(q_ref.shape[1] // LANES):
        cols = slice(j * LANES, (j + 1) * LANES)
        q, k, v = q_ref[:, cols], k_ref[:, cols], v_ref[:, cols]
        outs = []
        for sel in (first, jnp.logical_not(first)):
            s = _dot_nt(jnp.where(sel, q, jnp.zeros_like(q)), k)
            p = jnp.exp(s - jnp.max(s, axis=-1, keepdims=True))
            l = jnp.sum(p, axis=-1, keepdims=True)
            outs.append(_dot(p.astype(BF16), v) / l)
        o_ref[:, cols] = jnp.where(first, outs[0], outs[1]).astype(BF16)


def _ctx_attn(q, k, v, seq_len):
    t, e = q.shape
    spec = pl.BlockSpec((seq_len, e), lambda b: (b, 0))
    return pl.pallas_call(
        _ctx_attn_kernel,
        grid=(t // seq_len,),
        in_specs=[spec, spec, spec],
        out_specs=spec,
        out_shape=jax.ShapeDtypeStruct((t, e), BF16),
        compiler_params=_params(1),
        name="ctx_attn",
    )(q, k, v)


def _key_row_start(r_blk, rows):
    return jnp.clip(r_blk * ATT_Q_ROWS - WIN_R // 2, 0, rows - ATT_K_ROWS)


def _nbr_bias_build(rows, rpb_ref, bias_ref):
    n_blk = rows // ATT_Q_ROWS
    kc = lax.broadcasted_iota(jnp.int32, (GRID_W, LANES), 0)
    lane = lax.broadcasted_iota(jnp.int32, (GRID_W, LANES), 1)
    qc = lane & (GRID_W - 1)
    c0 = jnp.clip(qc - WIN_C // 2, 0, GRID_W - WIN_C)
    col_in = (kc >= c0) & (kc < c0 + WIN_C)
    first = lane < GRID_W
    neg = jnp.full((GRID_W, LANES), -jnp.inf, F32)
    for hh in range(2):
        lo, hi = [], []
        for dr in range(2 * WIN_R - 1):
            row = jnp.broadcast_to(rpb_ref[hh, dr:dr + 1, :] * LOG2_E, (GRID_W, LANES))
            lo.append(pltpu.roll(row, LANES - (WIN_C - 1), 1, stride=1, stride_axis=0))
            hi.append(pltpu.roll(row, GRID_W - (WIN_C - 1), 1, stride=1, stride_axis=0))
        for var, r_blk in enumerate((0, 1, n_blk - 1)):
            ks = min(max(r_blk * ATT_Q_ROWS - WIN_R // 2, 0), rows - ATT_K_ROWS)
            for ki in range(ATT_K_ROWS):
                kr = ks + ki
                for qp in range(ATT_Q_ROWS // 2):
                    halves = []
                    for half, src in enumerate((lo, hi)):
                        r = r_blk * ATT_Q_ROWS + 2 * qp + half
                        r0 = min(max(r - WIN_R // 2, 0), rows - WIN_R)
                        halves.append(src[kr - r + WIN_R - 1] if r0 <= kr < r0 + WIN_R else neg)
                    blk = jnp.where(col_in, jnp.where(first, halves[0], halves[1]), neg)
                    bias_ref[hh, var, ki * GRID_W:(ki + 1) * GRID_W, qp * LANES:(qp + 1) * LANES] = blk


def _nbr_attn_kernel(rows, rpb_ref, q_ref, k_ref, v_ref, ck_ref, cv_ref, o_ref, bias_ref, kc_ref, vx_ref, s_ref,
                     p_ref, oa_ref):
    n_blk = rows // ATT_Q_ROWS
    tq = ATT_Q_ROWS * GRID_W
    nk = ATT_K_ROWS * GRID_W
    seq = q_ref.shape[0]
    n_tok_tiles = seq // MXU_TILE
    n_ctx_tiles = kc_ref.shape[0] // MXU_TILE

    @pl.when(pl.program_id(1) == 0)
    def _():
        _nbr_bias_build(rows, rpb_ref, bias_ref)

    first = _lane_is_first_head()
    head_lanes = (first, jnp.logical_not(first))
    kc_ref[...] = ck_ref[...].T.astype(BF16)
    v_t = v_ref[...].astype(F32).T
    top = lax.broadcasted_iota(jnp.int32, (LANES, 1), 0) < DH_C
    for hh, sel in enumerate((top, jnp.logical_not(top))):
        for kt in range(n_tok_tiles):
            vx_ref[hh, kt] = jnp.where(sel, v_t[:, kt * MXU_TILE:(kt + 1) * MXU_TILE], 1.0).astype(BF16)
        for kt in range(n_ctx_tiles):
            vx_ref[hh, n_tok_tiles + kt] = jnp.where(
                sel, cv_ref[:, kt * MXU_TILE:(kt + 1) * MXU_TILE], 1.0).astype(BF16)

    def q_rows(r):
        return pl.ds(pl.multiple_of(r * tq, tq), tq)

    def k_start(r):
        return pl.multiple_of(_key_row_start(r, rows) * GRID_W, MXU_TILE)

    def logits(r, hh):
        q = q_ref[q_rows(r), :]
        qh = jnp.where(head_lanes[hh], q, jnp.zeros_like(q))
        var = jnp.where(r == 0, 0, jnp.where(r == n_blk - 1, 2, 1))
        s_ref[hh, 0:nk, :] = _dot_nt(k_ref[pl.ds(k_start(r), nk), :], qh) + bias_ref[hh, var]
        s_ref[hh, nk:, :] = _dot_nt(kc_ref[...], qh)

    def softmax(hh):
        s = s_ref[hh]
        p_ref[hh] = jnp.exp2((s - jnp.max(s, axis=0, keepdims=True)).astype(BF16))

    def weighted_values(r, hh):
        kt0 = k_start(r) // MXU_TILE
        tiles = [vx_ref[hh, kt0 + i] for i in range(nk // MXU_TILE)]
        tiles += [vx_ref[hh, n_tok_tiles + i] for i in range(n_ctx_tiles)]
        return _dot(jnp.concatenate(tiles, axis=1), p_ref[hh])

    def emit(r, o_second):
        o_first = oa_ref[...]
        num = jnp.concatenate([o_first[0:DH_C], o_second[DH_C:]], axis=0)
        den = jnp.concatenate([o_first[DH_C:], o_second[0:DH_C]], axis=0)
        o_ref[q_rows(r), :] = (num / den).T.astype(BF16)

    p_ref[1] = jnp.ones(p_ref.shape[1:], BF16)
    oa_ref[...] = jnp.ones(oa_ref.shape, F32)
    logits(0, 0)

    def block(r, carry):
        prev = jnp.maximum(r - 1, 0)
        emit(prev, weighted_values(prev, 1))
        logits(r, 1)
        softmax(0)
        oa_ref[...] = weighted_values(r, 0)
        logits(jnp.minimum(r + 1, n_blk - 1), 0)
        softmax(1)
        return carry

    lax.fori_loop(0, n_blk, block, 0, unroll=2)
    emit(n_blk - 1, weighted_values(n_blk - 1, 1))


def _nbr_attn(q, k, v, cache_k, cache_v, cache_layer, rpb, n_batch):
    t, e = q.shape
    seq = t // n_batch
    rows = seq // GRID_W
    n_cached, past = cache_k.shape[1:3]
    ck = jnp.transpose(cache_k, (0, 1, 3, 4, 2)).reshape(n_batch * n_cached, e, past)
    cv = jnp.transpose(cache_v, (0, 1, 3, 4, 2)).reshape(n_batch * n_cached, e, past)
    tq, nk = ATT_Q_ROWS * GRID_W, ATT_K_ROWS * GRID_W
    assert nk % MXU_TILE == 0 and past % MXU_TILE == 0 and (ATT_Q_ROWS * GRID_W) % MXU_TILE == 0
    n_dr, n_dc = rpb.shape[1:]
    rpb_pad = jnp.pad(rpb.astype(F32)[:, :, ::-1], ((0, 0), (0, 2 * WIN_R - n_dr), (0, LANES - n_dc)))
    tok_spec = pl.BlockSpec((seq, LANES), lambda j, b: (b, j))
    ctx_spec = pl.BlockSpec((None, LANES, past), lambda j, b: (b * n_cached + cache_layer, j, 0))
    return pl.pallas_call(
        functools.partial(_nbr_attn_kernel, rows),
        grid=(e // LANES, n_batch),
        in_specs=[pl.BlockSpec((2, 2 * WIN_R, LANES), lambda j, b: (j, 0, 0)),
                  tok_spec, tok_spec, tok_spec, ctx_spec, ctx_spec],
        out_specs=tok_spec,
        out_shape=jax.ShapeDtypeStruct((t, e), BF16),
        scratch_shapes=[
            pltpu.VMEM((2, 3, nk, tq), F32),
            pltpu.VMEM((past, LANES), BF16),
            pltpu.VMEM((2, (seq + past) // MXU_TILE, LANES, MXU_TILE), BF16),
            pltpu.VMEM((2, nk + past, tq), F32),
            pltpu.VMEM((2, nk + past, tq), BF16),
            pltpu.VMEM((LANES, tq), F32),
        ],
        compiler_params=_params(2),
        name="nbr_attn",
    )(rpb_pad, q, k, v, ck, cv)


def _na_out_kernel(x_ref, a_ref, sg_ref, mod_ref, wout_ref, g_ref, b_ref, o_ref):
    tm, d = x_ref.shape
    _, _, gate = _split_mod(mod_ref, d)
    y = (a_ref[...].astype(F32) * sg_ref[...].astype(F32)).astype(BF16)
    o = _dot(y, wout_ref[...])
    o_ref[...] = _layer_norm(DEEPNORM_ALPHA * x_ref[...] + gate * o, g_ref[...], b_ref[...])


def _na_out(x2d, attn, sg, seq_len, per_sample, mods, layer, w_out, ln_g, ln_b):
    t, d = x2d.shape
    tm = ROW_TILE
    tps = max(seq_len // tm, 1)
    e = w_out.shape[0]
    return pl.pallas_call(
        _na_out_kernel,
        grid=(t // tm,),
        in_specs=[
            pl.BlockSpec((tm, d), lambda i: (i, 0)),
            pl.BlockSpec((tm, e), lambda i: (i, 0)),
            pl.BlockSpec((tm, e), lambda i: (i, 0)),
            _mod_spec(layer, d, tps, per_sample),
            _const_spec(w_out.shape),
            _const_spec((1, d)),
            _const_spec((1, d)),
        ],
        out_specs=pl.BlockSpec((tm, d), lambda i: (i, 0)),
        out_shape=jax.ShapeDtypeStruct((t, d), F32),
        compiler_params=_params(1),
        name="na_out",
    )(x2d, attn, sg, mods, w_out, ln_g.reshape(1, d), ln_b.reshape(1, d))


def kernel(x_prompt, x_sample, c, cache_k, cache_v, c_ctx, w_mod, b_mod, ln_g, ln_b, pool_w_in, pool_w_grp,
           pool_scale, pool_w_out, sgu_w_in, sgu_ln_g, sgu_ln_b, sgu_w_s, sgu_b_s, sgu_w_out, na_w_in, na_rpb,
           na_w_out):
    n_p, seq_p, d = x_prompt.shape
    n_s, seq_s, _ = x_sample.shape
    assert n_s + 1 <= N_COND_ROWS and (n_p * seq_p) % ROW_TILE == 0 and seq_s % ROW_TILE == 0
    assert seq_p % CHUNK == 0 and (seq_p % ROW_TILE == 0 or ROW_TILE % seq_p == 0)
    assert seq_s % (GRID_W * ATT_Q_ROWS) == 0 and seq_s // GRID_W >= ATT_K_ROWS
    conds = jnp.zeros((N_COND_ROWS, d), F32).at[0].set(c_ctx).at[1:1 + n_s].set(c)
    mods = _mods(conds, w_mod, b_mod)
    pool_w_fold, pool_w_gate = _pool_fold(pool_w_in, pool_w_grp)
    pool_w_out_b = pool_w_out.astype(BF16)
    pool_band = _pool_band()

    yp = x_prompt.reshape(n_p * seq_p, d)
    ys = x_sample.reshape(n_s * seq_s, d)
    streams = ((seq_p, False), (seq_s, True))
    ctx_k = ctx_v = None
    for i in range(DEPTH):
        kind, j = i % N_MIXERS, i // N_MIXERS
        ys_in = (yp, ys)
        outs = []
        if kind == 0:
            for x2d, (seq, per_sample) in zip(ys_in, streams):
                outs.append(_pool_layer(x2d, seq, per_sample, mods, i, j, pool_band, pool_w_fold, pool_w_gate,
                                        pool_scale, pool_w_out_b, ln_g[i], ln_b[i]))
        elif kind == 1:
            w_in, w_s, w_out = sgu_w_in[j].astype(BF16), sgu_w_s[j].astype(BF16), sgu_w_out[j].astype(BF16)
            for x2d, (seq, per_sample) in zip(ys_in, streams):
                outs.append(_sgu_layer(x2d, seq, per_sample, mods, i, w_in, sgu_ln_g[j], sgu_ln_b[j], w_s,
                                       sgu_b_s[j].T, w_out, ln_g[i], ln_b[i]))
        else:
            w_in, w_out = na_w_in[j].astype(BF16), na_w_out[j].astype(BF16)
            q, k, v, sg, kf, vf = _na_proj(yp, seq_p, False, mods, i, w_in, True, DH_C ** -0.5)
            ctx_k = kf.reshape(n_p, 1, seq_p, H_C, DH_C)
            ctx_v = vf.reshape(n_p, 1, seq_p, H_C, DH_C)
            attn = _ctx_attn(q, k, v, seq_p)
            outs.append(_na_out(yp, attn, sg, seq_p, False, mods, i, w_out, ln_g[i], ln_b[i]))
            q, k, v, sg = _na_proj(ys, seq_s, True, mods, i, w_in, False, DH_C ** -0.5 * LOG2_E)
            attn = _nbr_attn(q, k, v, cache_k, cache_v, j, na_rpb[j], n_s)
            outs.append(_na_out(ys, attn, sg, seq_s, True, mods, i, w_out, ln_g[i], ln_b[i]))
        yp, ys = outs
    return (yp.reshape(n_p, seq_p, d), ys.reshape(n_s, seq_s, d), ctx_k, ctx_v)
```

```python
import functools

import jax
import jax.numpy as jnp
import numpy as np
from jax import lax
from jax.experimental import pallas as pl
from jax.experimental.pallas import tpu as pltpu

F32 = jnp.float32
BF16 = jnp.bfloat16

DEPTH = 4
N_MIXERS = 3
POOL_WINDOWS = (2, 4, 8, 16)
POOL_HALO = 16
POOL_BLOCK = 128
CHUNK = 128
H_B = 8
H_C = 16
DH_C = 64
GRID_W = 64
WIN_R = 8
WIN_C = 16
DEEPNORM_ALPHA = (2 * DEPTH) ** 0.25
LN_EPS = 1e-5
LOG2_E = float(np.log2(np.e))

N_COND_ROWS = 8
ROW_TILE = 512
ATT_Q_ROWS = 4
ATT_K_ROWS = 12
LANES = 128
MXU_TILE = 256
VMEM_LIMIT = 56 * 1024 * 1024


def _const_spec(shape):
    nd = len(shape)
    return pl.BlockSpec(shape, lambda *_: (0,) * nd, pipeline_mode=pl.Buffered(1))


def _params(n_axes):
    return pltpu.CompilerParams(dimension_semantics=("arbitrary",) * n_axes, vmem_limit_bytes=VMEM_LIMIT)


def _mod_spec(layer, d, tiles_per_seq, per_sample):
    base = layer * N_COND_ROWS
    if per_sample:
        return pl.BlockSpec((1, 1, 3 * d), lambda i: (base + 1 + i // tiles_per_seq, 0, 0))
    return pl.BlockSpec((1, 1, 3 * d), lambda i: (base, 0, 0))


def _split_mod(mod_ref, d):
    m = mod_ref[0]
    return m[:, :d], m[:, d:2 * d], m[:, 2 * d:]


def _silu(x):
    return x / (1.0 + jnp.exp(-x))


def _gelu_tanh(x):
    c = np.float32(np.sqrt(2.0 / np.pi))
    return x * (0.5 * (1.0 + jnp.tanh(c * (x + 0.044715 * (x * x * x)))))


def _layer_norm(x, g, b):
    mu = jnp.mean(x, axis=-1, keepdims=True)
    d = x - mu
    var = jnp.mean(d * d, axis=-1, keepdims=True)
    return d * lax.rsqrt(var + LN_EPS) * g + b


def _dot(a, b):
    return jnp.dot(a, b, preferred_element_type=F32)


def _dot_nt(a, b):
    return lax.dot_general(a, b, (((1,), (1,)), ((), ())), preferred_element_type=F32)


def _mods_kernel(cond_ref, w_ref, b_ref, o_ref):
    a = _silu(cond_ref[...]).astype(BF16)
    o_ref[0] = _dot(a, w_ref[0].astype(BF16)) + b_ref[0]


def _mods(conds, w_mod, b_mod):
    depth, d, n = w_mod.shape
    tn = 1024
    out = pl.pallas_call(
        _mods_kernel,
        grid=(depth, n // tn),
        in_specs=[
            pl.BlockSpec((N_COND_ROWS, d), lambda l, j: (0, 0)),
            pl.BlockSpec((1, d, tn), lambda l, j: (l, 0, j)),
            pl.BlockSpec((1, 1, tn), lambda l, j: (l, 0, j)),
        ],
        out_specs=pl.BlockSpec((1, N_COND_ROWS, tn), lambda l, j: (l, 0, j)),
        out_shape=jax.ShapeDtypeStruct((depth, N_COND_ROWS, n), F32),
        compiler_params=_params(2),
        name="adaln_mods",
    )(conds, w_mod, b_mod.reshape(depth, 1, n))
    return out.reshape(depth * N_COND_ROWS, 1, n)


def _pool_fold_kernel(win_x_ref, wgrp_ref, fold_ref):
    fold_ref[0] = _dot(win_x_ref[0].astype(BF16), wgrp_ref[0, 0].astype(BF16)).astype(BF16)


def _pool_fold(w_in, w_grp):
    n_layers, d, two_e = w_in.shape
    n_grp, grp = w_grp.shape[1:3]
    e_dim = two_e // 2
    return pl.pallas_call(
        _pool_fold_kernel,
        grid=(n_layers, n_grp),
        in_specs=[
            pl.BlockSpec((1, d, grp), lambda l, g: (l, 0, g)),
            pl.BlockSpec((1, 1, grp, grp), lambda l, g: (l, g, 0, 0)),
        ],
        out_specs=pl.BlockSpec((1, d, grp), lambda l, g: (l, 0, g)),
        out_shape=jax.ShapeDtypeStruct((n_layers, d, e_dim), BF16),
        compiler_params=_params(2),
        name="pool_fold",
    )(w_in, w_grp)


def _pool_band():
    t = np.arange(POOL_BLOCK)[:, None] + POOL_HALO
    e = np.arange(POOL_BLOCK + 2 * POOL_HALO)[None, :]
    return jnp.asarray(np.stack([(e >= t - w // 2) & (e <= t + w // 2 - 1) for w in POOL_WINDOWS]), BF16)


def _pool_kernel(tiles_per_seq, seq_len, x_ref, xp_ref, xn_ref, mod_ref, band_ref, wfold_ref, wgate_ref, psc_ref,
                 wout_ref, g_ref, b_ref, o_ref, e_ref, u_ref):
    tm, d = x_ref.shape
    e_dim = psc_ref.shape[1]
    n_win = len(POOL_WINDOWS)
    grp = e_dim // n_win
    it = pl.program_id(0) % tiles_per_seq
    shift, scale, gate = _split_mod(mod_ref, d)
    x = x_ref[...]
    hb = (x * (1.0 + scale) + shift).astype(BF16)
    hp = (xp_ref[...] * (1.0 + scale) + shift).astype(BF16)
    hn = (xn_ref[...] * (1.0 + scale) + shift).astype(BF16)
    zeros = jnp.zeros((POOL_HALO, d), BF16)
    e_ref[0:POOL_HALO] = jnp.where(it != 0, hp, zeros)
    e_ref[POOL_HALO:POOL_HALO + tm] = hb
    e_ref[POOL_HALO + tm:2 * POOL_HALO + tm] = jnp.where(it != tiles_per_seq - 1, hn, zeros)
    u_ref[...] = _dot(e_ref[...], wfold_ref[...]).astype(BF16)

    t = it * tm + lax.broadcasted_iota(jnp.int32, (tm, 1), 0)
    mixed = []
    for gi, w in enumerate(POOL_WINDOWS):
        cols = slice(gi * grp, (gi + 1) * grp)
        lo = jnp.maximum(t - w // 2, 0)
        hi = jnp.minimum(t + w // 2 - 1, seq_len - 1)
        inv_cnt = 1.0 / (hi - lo + 1).astype(F32)
        sums = jnp.concatenate(
            [_dot(band_ref[gi], u_ref[rb * POOL_BLOCK:(rb + 1) * POOL_BLOCK + 2 * POOL_HALO, cols])
             for rb in range(tm // POOL_BLOCK)], axis=0)
        mixed.append(sums * inv_cnt - u_ref[POOL_HALO:POOL_HALO + tm, cols].astype(F32))
    mixed = jnp.concatenate(mixed, axis=1)
    gate_pre = _dot(hb, wgate_ref[...].astype(BF16))
    y = (mixed * psc_ref[...] * _silu(gate_pre)).astype(BF16)
    o = _dot(y, wout_ref[...].astype(BF16))
    o_ref[...] = _layer_norm(DEEPNORM_ALPHA * x + gate * o, g_ref[...], b_ref[...])


def _pool_layer(x2d, seq_len, per_sample, mods, layer, j, band, w_fold, w_in, p_scale, w_out, ln_g, ln_b):
    t, d = x2d.shape
    tm = min(ROW_TILE, seq_len)
    tps = seq_len // tm
    e_dim = w_out.shape[1]
    hb = tm // POOL_HALO
    last = t // POOL_HALO - 1

    def layer_spec(shape):
        return pl.BlockSpec((None,) + shape, lambda i: (j,) + (0,) * len(shape), pipeline_mode=pl.Buffered(1))

    return pl.pallas_call(
        functools.partial(_pool_kernel, tps, seq_len),
        grid=(t // tm,),
        in_specs=[
            pl.BlockSpec((tm, d), lambda i: (i, 0)),
            pl.BlockSpec((POOL_HALO, d), lambda i: (jnp.maximum(i * hb - 1, 0), 0)),
            pl.BlockSpec((POOL_HALO, d), lambda i: (jnp.minimum((i + 1) * hb, last), 0)),
            _mod_spec(layer, d, tps, per_sample),
            _const_spec(band.shape),
            layer_spec((d, e_dim)),
            pl.BlockSpec((None, d, e_dim), lambda i: (j, 0, 1), pipeline_mode=pl.Buffered(1)),
            layer_spec((1, e_dim)),
            layer_spec((e_dim, d)),
            _const_spec((1, d)),
            _const_spec((1, d)),
        ],
        out_specs=pl.BlockSpec((tm, d), lambda i: (i, 0)),
        out_shape=jax.ShapeDtypeStruct((t, d), F32),
        scratch_shapes=[pltpu.VMEM((tm + 2 * POOL_HALO, d), BF16),
                        pltpu.VMEM((tm + 2 * POOL_HALO, e_dim), BF16)],
        compiler_params=_params(1),
        name="pool_layer",
    )(x2d, x2d, x2d, mods, band, w_fold, w_in, p_scale.reshape(-1, 1, e_dim), w_out,
      ln_g.reshape(1, d), ln_b.reshape(1, d))


def _sgu_kernel(x_ref, mod_ref, win_ref, lg_ref, lb_ref, ws_ref, bs_ref, wout_ref, g_ref, b_ref, o_ref, y_ref):
    tm, d = x_ref.shape
    e_dim = lg_ref.shape[1]
    dh = e_dim // H_B
    shift, scale, gate = _split_mod(mod_ref, d)
    x = x_ref[...]
    hb = (x * (1.0 + scale) + shift).astype(BF16)
    u = _gelu_tanh(_dot(hb, win_ref[:, 0:e_dim]))
    v = _gelu_tanh(_dot(hb, win_ref[:, e_dim:2 * e_dim]))
    v = _layer_norm(v, lg_ref[...], lb_ref[...]).astype(BF16)
    ug = u * _silu(_dot(hb, win_ref[:, 2 * e_dim:3 * e_dim]))
    for c in range(tm // CHUNK):
        rows = slice(c * CHUNK, (c + 1) * CHUNK)
        for hh in range(H_B):
            cols = slice(hh * dh, (hh + 1) * dh)
            sv = _dot(ws_ref[hh], v[rows, cols]) + bs_ref[:, hh:hh + 1]
            y_ref[rows, cols] = (ug[rows, cols] * sv).astype(BF16)
    o = _dot(y_ref[...], wout_ref[...])
    o_ref[...] = _layer_norm(DEEPNORM_ALPHA * x + gate * o, g_ref[...], b_ref[...])


def _sgu_layer(x2d, seq_len, per_sample, mods, layer, w_in, sln_g, sln_b, w_s, b_s_t, w_out, ln_g, ln_b):
    t, d = x2d.shape
    tm = ROW_TILE
    tps = max(seq_len // tm, 1)
    e_dim = w_out.shape[0]
    return pl.pallas_call(
        _sgu_kernel,
        grid=(t // tm,),
        in_specs=[
            pl.BlockSpec((tm, d), lambda i: (i, 0)),
            _mod_spec(layer, d, tps, per_sample),
            _const_spec(w_in.shape),
            _const_spec((1, e_dim)),
            _const_spec((1, e_dim)),
            _const_spec(w_s.shape),
            _const_spec(b_s_t.shape),
            _const_spec(w_out.shape),
            _const_spec((1, d)),
            _const_spec((1, d)),
        ],
        out_specs=pl.BlockSpec((tm, d), lambda i: (i, 0)),
        out_shape=jax.ShapeDtypeStruct((t, d), F32),
        scratch_shapes=[pltpu.VMEM((tm, e_dim), BF16)],
        compiler_params=_params(1),
        name="sgu_layer",
    )(x2d, mods, w_in, sln_g.reshape(1, e_dim), sln_b.reshape(1, e_dim), w_s, b_s_t, w_out,
      ln_g.reshape(1, d), ln_b.reshape(1, d))


def _na_proj_kernel(emit_f32_kv, q_scale, x_ref, mod_ref, win_ref, *out_refs):
    tm, d = x_ref.shape
    e = win_ref.shape[1] // 4
    shift, scale, _ = _split_mod(mod_ref, d)
    hb = (x_ref[...] * (1.0 + scale) + shift).astype(BF16)
    q_ref, k_ref, v_ref, sg_ref = out_refs[:4]
    q_ref[...] = (_dot(hb, win_ref[:, 0:e].astype(BF16)) * q_scale).astype(BF16)
    k = _dot(hb, win_ref[:, e:2 * e].astype(BF16))
    v = _dot(hb, win_ref[:, 2 * e:3 * e].astype(BF16))
    k_ref[...] = k.astype(BF16)
    v_ref[...] = v.astype(BF16)
    sg_ref[...] = _silu(_dot(hb, win_ref[:, 3 * e:4 * e].astype(BF16))).astype(BF16)
    if emit_f32_kv:
        seq = out_refs[4].shape[2]
        for kv, ref in ((k, out_refs[4]), (v, out_refs[5])):
            for sq in range(tm // seq):
                ref[sq] = kv[sq * seq:(sq + 1) * seq, :].T


def _na_proj(x2d, seq_len, per_sample, mods, layer, w_in, emit_f32_kv, q_scale):
    t, d = x2d.shape
    tm = ROW_TILE
    tps = max(seq_len // tm, 1)
    e = w_in.shape[1] // 4
    row_spec = pl.BlockSpec((tm, e), lambda i: (i, 0))
    out_specs = [row_spec] * 4
    out_shape = [jax.ShapeDtypeStruct((t, e), BF16)] * 4
    if emit_f32_kv:
        out_specs += [pl.BlockSpec((tm // seq_len, e, seq_len), lambda i: (i, 0, 0))] * 2
        out_shape += [jax.ShapeDtypeStruct((t // seq_len, e, seq_len), F32)] * 2
    return pl.pallas_call(
        functools.partial(_na_proj_kernel, emit_f32_kv, q_scale),
        grid=(t // tm,),
        in_specs=[
            pl.BlockSpec((tm, d), lambda i: (i, 0)),
            _mod_spec(layer, d, tps, per_sample),
            _const_spec(w_in.shape),
        ],
        out_specs=out_specs,
        out_shape=out_shape,
        compiler_params=_params(1),
        name="na_proj",
    )(x2d, mods, w_in)


def _lane_is_first_head():
    return lax.broadcasted_iota(jnp.int32, (1, LANES), 1) < DH_C


def _ctx_attn_kernel(q_ref, k_ref, v_ref, o_ref):
    first = _lane_is_first_head()
    for j in range(q_ref.shape[1] // LANES):
        cols = slice(j * LANES, (j + 1) * LANES)
        q, k, v = q_ref[:, cols], k_ref[:, cols], v_ref[:, cols]
        outs = []
        for sel in (first, jnp.logical_not(first)):
            s = _dot_nt(jnp.where(sel, q, jnp.zeros_like(q)), k)
            p = jnp.exp(s - jnp.max(s, axis=-1, keepdims=True))
            l = jnp.sum(p, axis=-1, keepdims=True)
            outs.append(_dot(p.astype(BF16), v) / l)
        o_ref[:, cols] = jnp.where(first, outs[0], outs[1]).astype(BF16)


def _ctx_attn(q, k, v, seq_len):
    t, e = q.shape
    spec = pl.BlockSpec((seq_len, e), lambda b: (b, 0))
    return pl.pallas_call(
        _ctx_attn_kernel,
        grid=(t // seq_len,),
        in_specs=[spec, spec, spec],
        out_specs=spec,
        out_shape=jax.ShapeDtypeStruct((t, e), BF16),
        compiler_params=_params(1),
        name="ctx_attn",
    )(q, k, v)


def _key_row_start(r_blk, rows):
    return jnp.clip(r_blk * ATT_Q_ROWS - WIN_R // 2, 0, rows - ATT_K_ROWS)


def _nbr_bias_build(rows, rpb_ref, bias_ref):
    n_blk = rows // ATT_Q_ROWS
    kc = lax.broadcasted_iota(jnp.int32, (GRID_W, LANES), 0)
    lane = lax.broadcasted_iota(jnp.int32, (GRID_W, LANES), 1)
    qc = lane & (GRID_W - 1)
    c0 = jnp.clip(qc - WIN_C // 2, 0, GRID_W - WIN_C)
    col_in = (kc >= c0) & (kc < c0 + WIN_C)
    first = lane < GRID_W
    neg = jnp.full((GRID_W, LANES), -jnp.inf, F32)
    for hh in range(2):
        lo, hi = [], []
        for dr in range(2 * WIN_R - 1):
            row = jnp.broadcast_to(rpb_ref[hh, dr:dr + 1, :] * LOG2_E, (GRID_W, LANES))
            lo.append(pltpu.roll(row, LANES - (WIN_C - 1), 1, stride=1, stride_axis=0))
            hi.append(pltpu.roll(row, GRID_W - (WIN_C - 1), 1, stride=1, stride_axis=0))
        for var, r_blk in enumerate((0, 1, n_blk - 1)):
            ks = min(max(r_blk * ATT_Q_ROWS - WIN_R // 2, 0), rows - ATT_K_ROWS)
            for ki in range(ATT_K_ROWS):
                kr = ks + ki
                for qp in range(ATT_Q_ROWS // 2):
                    halves = []
                    for half, src in enumerate((lo, hi)):
                        r = r_blk * ATT_Q_ROWS + 2 * qp + half
                        r0 = min(max(r - WIN_R // 2, 0), rows - WIN_R)
                        halves.append(src[kr - r + WIN_R - 1] if r0 <= kr < r0 + WIN_R else neg)
                    blk = jnp.where(col_in, jnp.where(first, halves[0], halves[1]), neg)
                    bias_ref[hh, var, ki * GRID_W:(ki + 1) * GRID_W, qp * LANES:(qp + 1) * LANES] = blk


def _nbr_attn_kernel(rows, rpb_ref, q_ref, k_ref, v_ref, ck_ref, cv_ref, o_ref, bias_ref, kc_ref, vx_ref, s_ref,
                     p_ref, oa_ref):
    n_blk = rows // ATT_Q_ROWS
    tq = ATT_Q_ROWS * GRID_W
    nk = ATT_K_ROWS * GRID_W
    seq = q_ref.shape[0]
    n_tok_tiles = seq // MXU_TILE
    n_ctx_tiles = kc_ref.shape[0] // MXU_TILE

    @pl.when(pl.program_id(1) == 0)
    def _():
        _nbr_bias_build(rows, rpb_ref, bias_ref)

    first = _lane_is_first_head()
    head_lanes = (first, jnp.logical_not(first))
    kc_ref[...] = ck_ref[...].T.astype(BF16)
    v_t = v_ref[...].astype(F32).T
    top = lax.broadcasted_iota(jnp.int32, (LANES, 1), 0) < DH_C
    for hh, sel in enumerate((top, jnp.logical_not(top))):
        for kt in range(n_tok_tiles):
            vx_ref[hh, kt] = jnp.where(sel, v_t[:, kt * MXU_TILE:(kt + 1) * MXU_TILE], 1.0).astype(BF16)
        for kt in range(n_ctx_tiles):
            vx_ref[hh, n_tok_tiles + kt] = jnp.where(
                sel, cv_ref[:, kt * MXU_TILE:(kt + 1) * MXU_TILE], 1.0).astype(BF16)

    def q_rows(r):
        return pl.ds(pl.multiple_of(r * tq, tq), tq)

    def k_start(r):
        return pl.multiple_of(_key_row_start(r, rows) * GRID_W, MXU_TILE)

    def logits(r, hh):
        q = q_ref[q_rows(r), :]
        qh = jnp.where(head_lanes[hh], q, jnp.zeros_like(q))
        var = jnp.where(r == 0, 0, jnp.where(r == n_blk - 1, 2, 1))
        s_ref[hh, 0:nk, :] = _dot_nt(k_ref[pl.ds(k_start(r), nk), :], qh) + bias_ref[hh, var]
        s_ref[hh, nk:, :] = _dot_nt(kc_ref[...], qh)

    def softmax(hh):
        s = s_ref[hh]
        p_ref[hh] = jnp.exp2((s - jnp.max(s, axis=0, keepdims=True)).astype(BF16))

    def weighted_values(r, hh):
        kt0 = k_start(r) // MXU_TILE
        tiles = [vx_ref[hh, kt0 + i] for i in range(nk // MXU_TILE)]
        tiles += [vx_ref[hh, n_tok_tiles + i] for i in range(n_ctx_tiles)]
        return _dot(jnp.concatenate(tiles, axis=1), p_ref[hh])

    def emit(r, o_second):
        o_first = oa_ref[...]
        num = jnp.concatenate([o_first[0:DH_C], o_second[DH_C:]], axis=0)
        den = jnp.concatenate([o_first[DH_C:], o_second[0:DH_C]], axis=0)
        o_ref[q_rows(r), :] = (num / den).T.astype(BF16)

    p_ref[1] = jnp.ones(p_ref.shape[1:], BF16)
    oa_ref[...] = jnp.ones(oa_ref.shape, F32)
    logits(0, 0)

    def block(r, carry):
        prev = jnp.maximum(r - 1, 0)
        emit(prev, weighted_values(prev, 1))
        logits(r, 1)
        softmax(0)
        oa_ref[...] = weighted_values(r, 0)
        logits(jnp.minimum(r + 1, n_blk - 1), 0)
        softmax(1)
        return carry

    lax.fori_loop(0, n_blk, block, 0, unroll=2)
    emit(n_blk - 1, weighted_values(n_blk - 1, 1))


def _nbr_attn(q, k, v, cache_k, cache_v, cache_layer, rpb, n_batch):
    t, e = q.shape
    seq = t // n_batch
    rows = seq // GRID_W
    n_cached, past = cache_k.shape[1:3]
    ck = jnp.transpose(cache_k, (0, 1, 3, 4, 2)).reshape(n_batch * n_cached, e, past)
    cv = jnp.transpose(cache_v, (0, 1, 3, 4, 2)).reshape(n_batch * n_cached, e, past)
    tq, nk = ATT_Q_ROWS * GRID_W, ATT_K_ROWS * GRID_W
    assert nk % MXU_TILE == 0 and past % MXU_TILE == 0 and (ATT_Q_ROWS * GRID_W) % MXU_TILE == 0
    n_dr, n_dc = rpb.shape[1:]
    rpb_pad = jnp.pad(rpb.astype(F32)[:, :, ::-1], ((0, 0), (0, 2 * WIN_R - n_dr), (0, LANES - n_dc)))
    tok_spec = pl.BlockSpec((seq, LANES), lambda j, b: (b, j))
    ctx_spec = pl.BlockSpec((None, LANES, past), lambda j, b: (b * n_cached + cache_layer, j, 0))
    return pl.pallas_call(
        functools.partial(_nbr_attn_kernel, rows),
        grid=(e // LANES, n_batch),
        in_specs=[pl.BlockSpec((2, 2 * WIN_R, LANES), lambda j, b: (j, 0, 0)),
                  tok_spec, tok_spec, tok_spec, ctx_spec, ctx_spec],
        out_specs=tok_spec,
        out_shape=jax.ShapeDtypeStruct((t, e), BF16),
        scratch_shapes=[
            pltpu.VMEM((2, 3, nk, tq), F32),
            pltpu.VMEM((past, LANES), BF16),
            pltpu.VMEM((2, (seq + past) // MXU_TILE, LANES, MXU_TILE), BF16),
            pltpu.VMEM((2, nk + past, tq), F32),
            pltpu.VMEM((2, nk + past, tq), BF16),
            pltpu.VMEM((LANES, tq), F32),
        ],
        compiler_params=_params(2),
        name="nbr_attn",
    )(rpb_pad, q, k, v, ck, cv)


def _na_out_kernel(x_ref, a_ref, sg_ref, mod_ref, wout_ref, g_ref, b_ref, o_ref):
    tm, d = x_ref.shape
    _, _, gate = _split_mod(mod_ref, d)
    y = (a_ref[...].astype(F32) * sg_ref[...].astype(F32)).astype(BF16)
    o = _dot(y, wout_ref[...].astype(BF16))
    o_ref[...] = _layer_norm(DEEPNORM_ALPHA * x_ref[...] + gate * o, g_ref[...], b_ref[...])


def _na_out(x2d, attn, sg, seq_len, per_sample, mods, layer, w_out, ln_g, ln_b):
    t, d = x2d.shape
    tm = ROW_TILE
    tps = max(seq_len // tm, 1)
    e = w_out.shape[0]
    return pl.pallas_call(
        _na_out_kernel,
        grid=(t // tm,),
        in_specs=[
            pl.BlockSpec((tm, d), lambda i: (i, 0)),
            pl.BlockSpec((tm, e), lambda i: (i, 0)),
            pl.BlockSpec((tm, e), lambda i: (i, 0)),
            _mod_spec(layer, d, tps, per_sample),
            _const_spec(w_out.shape),
            _const_spec((1, d)),
            _const_spec((1, d)),
        ],
        out_specs=pl.BlockSpec((tm, d), lambda i: (i, 0)),
        out_shape=jax.ShapeDtypeStruct((t, d), F32),
        compiler_params=_params(1),
        name="na_out",
    )(x2d, attn, sg, mods, w_out, ln_g.reshape(1, d), ln_b.reshape(1, d))


def kernel(x_prompt, x_sample, c, cache_k, cache_v, c_ctx, w_mod, b_mod, ln_g, ln_b, pool_w_in, pool_w_grp,
           pool_scale, pool_w_out, sgu_w_in, sgu_ln_g, sgu_ln_b, sgu_w_s, sgu_b_s, sgu_w_out, na_w_in, na_rpb,
           na_w_out):
    n_p, seq_p, d = x_prompt.shape
    n_s, seq_s, _ = x_sample.shape
    assert n_s + 1 <= N_COND_ROWS and (n_p * seq_p) % ROW_TILE == 0 and seq_s % ROW_TILE == 0
    assert seq_p % CHUNK == 0 and (seq_p % ROW_TILE == 0 or ROW_TILE % seq_p == 0)
    assert seq_s % (GRID_W * ATT_Q_ROWS) == 0 and seq_s // GRID_W >= ATT_K_ROWS
    conds = jnp.zeros((N_COND_ROWS, d), F32).at[0].set(c_ctx).at[1:1 + n_s].set(c)
    mods = _mods(conds, w_mod, b_mod)
    pool_w_fold = _pool_fold(pool_w_in, pool_w_grp)
    pool_band = _pool_band()

    yp = x_prompt.reshape(n_p * seq_p, d)
    ys = x_sample.reshape(n_s * seq_s, d)
    streams = ((seq_p, False), (seq_s, True))
    ctx_k = ctx_v = None
    for i in range(DEPTH):
        kind, j = i % N_MIXERS, i // N_MIXERS
        ys_in = (yp, ys)
        outs = []
        if kind == 0:
            for x2d, (seq, per_sample) in zip(ys_in, streams):
                outs.append(_pool_layer(x2d, seq, per_sample, mods, i, j, pool_band, pool_w_fold, pool_w_in,
                                        pool_scale, pool_w_out, ln_g[i], ln_b[i]))
        elif kind == 1:
            w_in, w_s, w_out = sgu_w_in[j].astype(BF16), sgu_w_s[j].astype(BF16), sgu_w_out[j].astype(BF16)
            for x2d, (seq, per_sample) in zip(ys_in, streams):
                outs.append(_sgu_layer(x2d, seq, per_sample, mods, i, w_in, sgu_ln_g[j], sgu_ln_b[j], w_s,
                                       sgu_b_s[j].T, w_out, ln_g[i], ln_b[i]))
        else:
            w_in, w_out = na_w_in[j], na_w_out[j]
            q, k, v, sg, kf, vf = _na_proj(yp, seq_p, False, mods, i, w_in, True, DH_C ** -0.5)
            ctx_k = kf.reshape(n_p, 1, H_C, DH_C, seq_p).transpose(0, 1, 4, 2, 3)
            ctx_v = vf.reshape(n_p, 1, H_C, DH_C, seq_p).transpose(0, 1, 4, 2, 3)
            attn = _ctx_attn(q, k, v, seq_p)
            outs.append(_na_out(yp, attn, sg, seq_p, False, mods, i, w_out, ln_g[i], ln_b[i]))
            q, k, v, sg = _na_proj(ys, seq_s, True, mods, i, w_in, False, DH_C ** -0.5 * LOG2_E)
            attn = _nbr_attn(q, k, v, cache_k, cache_v, j, na_rpb[j], n_s)
            outs.append(_na_out(ys, attn, sg, seq_s, True, mods, i, w_out, ln_g[i], ln_b[i]))
        yp, ys = outs
    return (yp.reshape(n_p, seq_p, d), ys.reshape(n_s, seq_s, d), ctx_k, ctx_v)
```

```python
import functools

import jax
import jax.numpy as jnp
import numpy as np
from jax import lax
from jax.experimental import pallas as pl
from jax.experimental.pallas import tpu as pltpu

F32 = jnp.float32
BF16 = jnp.bfloat16

DEPTH = 4
N_MIXERS = 3
POOL_WINDOWS = (2, 4, 8, 16)
POOL_HALO = 16
POOL_BLOCK = 128
CHUNK = 128
H_B = 8
H_C = 16
DH_C = 64
GRID_W = 64
WIN_R = 8
WIN_C = 16
DEEPNORM_ALPHA = (2 * DEPTH) ** 0.25
LN_EPS = 1e-5
LOG2_E = float(np.log2(np.e))

N_COND_ROWS = 8
ROW_TILE = 512
ATT_Q_ROWS = 4
ATT_K_ROWS = 12
LANES = 128
MXU_TILE = 256
VMEM_LIMIT = 56 * 1024 * 1024


def _const_spec(shape):
    nd = len(shape)
    return pl.BlockSpec(shape, lambda *_: (0,) * nd, pipeline_mode=pl.Buffered(1))


def _params(n_axes):
    return pltpu.CompilerParams(dimension_semantics=("arbitrary",) * n_axes, vmem_limit_bytes=VMEM_LIMIT)


def _mod_spec(layer, d, tiles_per_seq, per_sample):
    base = layer * N_COND_ROWS
    if per_sample:
        return pl.BlockSpec((1, 1, 3 * d), lambda i: (base + 1 + i // tiles_per_seq, 0, 0))
    return pl.BlockSpec((1, 1, 3 * d), lambda i: (base, 0, 0))


def _split_mod(mod_ref, d):
    m = mod_ref[0]
    return m[:, :d], m[:, d:2 * d], m[:, 2 * d:]


def _silu(x):
    return x / (1.0 + jnp.exp(-x))


def _gelu_tanh_x2(x):
    c = np.float32(np.sqrt(2.0 / np.pi))
    return x * (1.0 + jnp.tanh(x * (c + (c * np.float32(0.044715)) * (x * x))))


def _layer_norm(x, g, b, eps=LN_EPS):
    mu = jnp.mean(x, axis=-1, keepdims=True)
    d = x - mu
    var = jnp.mean(d * d, axis=-1, keepdims=True)
    return d * lax.rsqrt(var + eps) * g + b


def _resid_layer_norm(x, gate, o, g, b):
    return _layer_norm(x + (gate * (1.0 / DEEPNORM_ALPHA)) * o, g, b, LN_EPS / DEEPNORM_ALPHA ** 2)


def _dot(a, b):
    return jnp.dot(a, b, preferred_element_type=F32)


def _dot_nt(a, b):
    return lax.dot_general(a, b, (((1,), (1,)), ((), ())), preferred_element_type=F32)


def _mods_kernel(cond_ref, w_ref, b_ref, o_ref):
    a = _silu(cond_ref[...]).astype(BF16)
    o_ref[0] = _dot(a, w_ref[0].astype(BF16)) + b_ref[0]


def _mods(conds, w_mod, b_mod):
    depth, d, n = w_mod.shape
    tn = n
    out = pl.pallas_call(
        _mods_kernel,
        grid=(depth, n // tn),
        in_specs=[
            pl.BlockSpec((N_COND_ROWS, d), lambda l, j: (0, 0)),
            pl.BlockSpec((1, d, tn), lambda l, j: (l, 0, j)),
            pl.BlockSpec((1, 1, tn), lambda l, j: (l, 0, j)),
        ],
        out_specs=pl.BlockSpec((1, N_COND_ROWS, tn), lambda l, j: (l, 0, j)),
        out_shape=jax.ShapeDtypeStruct((depth, N_COND_ROWS, n), F32),
        compiler_params=_params(2),
        name="adaln_mods",
    )(conds, w_mod, b_mod.reshape(depth, 1, n))
    return out.reshape(depth * N_COND_ROWS, 1, n)


def _pool_fold_kernel(win_x_ref, wgrp_ref, fold_ref):
    fold_ref[0] = _dot(win_x_ref[0].astype(BF16), wgrp_ref[0, 0].astype(BF16)).astype(BF16)


def _pool_fold(w_in, w_grp):
    n_layers, d, two_e = w_in.shape
    n_grp, grp = w_grp.shape[1:3]
    e_dim = two_e // 2
    return pl.pallas_call(
        _pool_fold_kernel,
        grid=(n_layers, n_grp),
        in_specs=[
            pl.BlockSpec((1, d, grp), lambda l, g: (l, 0, g)),
            pl.BlockSpec((1, 1, grp, grp), lambda l, g: (l, g, 0, 0)),
        ],
        out_specs=pl.BlockSpec((1, d, grp), lambda l, g: (l, 0, g)),
        out_shape=jax.ShapeDtypeStruct((n_layers, d, e_dim), BF16),
        compiler_params=_params(2),
        name="pool_fold",
    )(w_in, w_grp)


def _pool_band():
    t = np.arange(POOL_BLOCK)[:, None] + POOL_HALO
    e = np.arange(POOL_BLOCK + 2 * POOL_HALO)[None, :]
    return jnp.asarray(np.stack([(e >= t - w // 2) & (e <= t + w // 2 - 1) for w in POOL_WINDOWS]), BF16)


def _pool_kernel(tiles_per_seq, seq_len, x_ref, xp_ref, xn_ref, mod_ref, band_ref, wfold_ref, wgate_ref, psc_ref,
                 wout_ref, g_ref, b_ref, o_ref, e_ref, u_ref):
    tm, d = x_ref.shape
    e_dim = psc_ref.shape[1]
    n_win = len(POOL_WINDOWS)
    grp = e_dim // n_win
    it = pl.program_id(0) % tiles_per_seq
    shift, scale, gate = _split_mod(mod_ref, d)
    x = x_ref[...]
    hb = (x * (1.0 + scale) + shift).astype(BF16)
    hp = (xp_ref[...] * (1.0 + scale) + shift).astype(BF16)
    hn = (xn_ref[...] * (1.0 + scale) + shift).astype(BF16)
    zeros = jnp.zeros((POOL_HALO, d), BF16)
    e_ref[0:POOL_HALO] = jnp.where(it != 0, hp, zeros)
    e_ref[POOL_HALO:POOL_HALO + tm] = hb
    e_ref[POOL_HALO + tm:2 * POOL_HALO + tm] = jnp.where(it != tiles_per_seq - 1, hn, zeros)
    u_ref[...] = _dot(e_ref[...], wfold_ref[...]).astype(BF16)

    t = it * tm + lax.broadcasted_iota(jnp.int32, (tm, 1), 0)
    mixed = []
    for gi, w in enumerate(POOL_WINDOWS):
        cols = slice(gi * grp, (gi + 1) * grp)
        lo = jnp.maximum(t - w // 2, 0)
        hi = jnp.minimum(t + w // 2 - 1, seq_len - 1)
        inv_cnt = 1.0 / (hi - lo + 1).astype(F32)
        sums = jnp.concatenate(
            [_dot(band_ref[gi], u_ref[rb * POOL_BLOCK:(rb + 1) * POOL_BLOCK + 2 * POOL_HALO, cols])
             for rb in range(tm // POOL_BLOCK)], axis=0)
        mixed.append(sums * inv_cnt - u_ref[POOL_HALO:POOL_HALO + tm, cols].astype(F32))
    mixed = jnp.concatenate(mixed, axis=1)
    gate_pre = _dot(hb, wgate_ref[...].astype(BF16))
    y = (mixed * psc_ref[...] * _silu(gate_pre)).astype(BF16)
    o = _dot(y, wout_ref[...].astype(BF16))
    o_ref[...] = _resid_layer_norm(x, gate, o, g_ref[...], b_ref[...])


def _pool_layer(x2d, seq_len, per_sample, mods, layer, j, band, w_fold, w_in, p_scale, w_out, ln_g, ln_b):
    t, d = x2d.shape
    tm = min(ROW_TILE, seq_len)
    tps = seq_len // tm
    e_dim = w_out.shape[1]
    hb = tm // POOL_HALO
    last = t // POOL_HALO - 1

    def layer_spec(shape):
        return pl.BlockSpec((None,) + shape, lambda i: (j,) + (0,) * len(shape), pipeline_mode=pl.Buffered(1))

    return pl.pallas_call(
        functools.partial(_pool_kernel, tps, seq_len),
        grid=(t // tm,),
        in_specs=[
            pl.BlockSpec((tm, d), lambda i: (i, 0)),
            pl.BlockSpec((POOL_HALO, d), lambda i: (jnp.maximum(i * hb - 1, 0), 0)),
            pl.BlockSpec((POOL_HALO, d), lambda i: (jnp.minimum((i + 1) * hb, last), 0)),
            _mod_spec(layer, d, tps, per_sample),
            _const_spec(band.shape),
            layer_spec((d, e_dim)),
            pl.BlockSpec((None, d, e_dim), lambda i: (j, 0, 1), pipeline_mode=pl.Buffered(1)),
            layer_spec((1, e_dim)),
            layer_spec((e_dim, d)),
            _const_spec((1, d)),
            _const_spec((1, d)),
        ],
        out_specs=pl.BlockSpec((tm, d), lambda i: (i, 0)),
        out_shape=jax.ShapeDtypeStruct((t, d), F32),
        scratch_shapes=[pltpu.VMEM((tm + 2 * POOL_HALO, d), BF16),
                        pltpu.VMEM((tm + 2 * POOL_HALO, e_dim), BF16)],
        compiler_params=_params(1),
        name="pool_layer",
    )(x2d, x2d, x2d, mods, band, w_fold, w_in, p_scale.reshape(-1, 1, e_dim), w_out,
      ln_g.reshape(1, d), ln_b.reshape(1, d))


def _sgu_kernel(x_ref, mod_ref, win_ref, lg_ref, lb_ref, ws_ref, bs_ref, wout_ref, g_ref, b_ref, o_ref, y_ref):
    tm, d = x_ref.shape
    e_dim = lg_ref.shape[1]
    dh = e_dim // H_B
    shift, scale, gate = _split_mod(mod_ref, d)
    x = x_ref[...]
    hb = (x * (1.0 + scale) + shift).astype(BF16)
    v2 = _gelu_tanh_x2(_dot(hb, win_ref[:, e_dim:2 * e_dim]))
    v = _layer_norm(v2, lg_ref[...], lb_ref[...], 4.0 * LN_EPS).astype(BF16)
    u2 = _gelu_tanh_x2(_dot(hb, win_ref[:, 0:e_dim]))
    ug = u2 * _silu(_dot(hb, win_ref[:, 2 * e_dim:3 * e_dim]))
    for c in range(tm // CHUNK):
        rows = slice(c * CHUNK, (c + 1) * CHUNK)
        for hh in range(H_B):
            cols = slice(hh * dh, (hh + 1) * dh)
            sv = _dot(ws_ref[hh], v[rows, cols]) + bs_ref[:, hh:hh + 1]
            y_ref[rows, cols] = (ug[rows, cols] * sv).astype(BF16)
    o2 = _dot(y_ref[...], wout_ref[...])
    o_ref[...] = _resid_layer_norm(x, 0.5 * gate, o2, g_ref[...], b_ref[...])


def _sgu_layer(x2d, seq_len, per_sample, mods, layer, w_in, sln_g, sln_b, w_s, b_s_t, w_out, ln_g, ln_b):
    t, d = x2d.shape
    tm = ROW_TILE
    tps = max(seq_len // tm, 1)
    e_dim = w_out.shape[0]
    return pl.pallas_call(
        _sgu_kernel,
        grid=(t // tm,),
        in_specs=[
            pl.BlockSpec((tm, d), lambda i: (i, 0)),
            _mod_spec(layer, d, tps, per_sample),
            _const_spec(w_in.shape),
            _const_spec((1, e_dim)),
            _const_spec((1, e_dim)),
            _const_spec(w_s.shape),
            _const_spec(b_s_t.shape),
            _const_spec(w_out.shape),
            _const_spec((1, d)),
            _const_spec((1, d)),
        ],
        out_specs=pl.BlockSpec((tm, d), lambda i: (i, 0)),
        out_shape=jax.ShapeDtypeStruct((t, d), F32),
        scratch_shapes=[pltpu.VMEM((tm, e_dim), BF16)],
        compiler_params=_params(1),
        name="sgu_layer",
    )(x2d, mods, w_in, sln_g.reshape(1, e_dim), sln_b.reshape(1, e_dim), w_s, b_s_t, w_out,
      ln_g.reshape(1, d), ln_b.reshape(1, d))


def _na_proj_kernel(emit_f32_kv, q_scale, x_ref, mod_ref, win_ref, *out_refs):
    tm, d = x_ref.shape
    e = win_ref.shape[1] // 4
    shift, scale, _ = _split_mod(mod_ref, d)
    hb = (x_ref[...] * (1.0 + scale) + shift).astype(BF16)
    q_ref, k_ref, v_ref, sg_ref = out_refs[:4]
    q_ref[...] = (_dot(hb, win_ref[:, 0:e].astype(BF16)) * q_scale).astype(BF16)
    k = _dot(hb, win_ref[:, e:2 * e].astype(BF16))
    v = _dot(hb, win_ref[:, 2 * e:3 * e].astype(BF16))
    k_ref[...] = k.astype(BF16)
    v_ref[...] = v.astype(BF16)
    sg_ref[...] = _silu(_dot(hb, win_ref[:, 3 * e:4 * e].astype(BF16))).astype(BF16)
    if emit_f32_kv:
        seq = out_refs[4].shape[2]
        for kv, ref in ((k, out_refs[4]), (v, out_refs[5])):
            for sq in range(tm // seq):
                ref[sq] = kv[sq * seq:(sq + 1) * seq, :].T


def _na_proj(x2d, seq_len, per_sample, mods, layer, w_in, emit_f32_kv, q_scale):
    t, d = x2d.shape
    tm = ROW_TILE
    tps = max(seq_len // tm, 1)
    e = w_in.shape[1] // 4
    row_spec = pl.BlockSpec((tm, e), lambda i: (i, 0))
    out_specs = [row_spec] * 4
    out_shape = [jax.ShapeDtypeStruct((t, e), BF16)] * 4
    if emit_f32_kv:
        out_specs += [pl.BlockSpec((tm // seq_len, e, seq_len), lambda i: (i, 0, 0))] * 2
        out_shape += [jax.ShapeDtypeStruct((t // seq_len, e, seq_len), F32)] * 2
    return pl.pallas_call(
        functools.partial(_na_proj_kernel, emit_f32_kv, q_scale),
        grid=(t // tm,),
        in_specs=[
            pl.BlockSpec((tm, d), lambda i: (i, 0)),
            _mod_spec(layer, d, tps, per_sample),
            _const_spec(w_in.shape),
        ],
        out_specs=out_specs,
        out_shape=out_shape,
        compiler_params=_params(1),
        name="na_proj",
    )(x2d, mods, w_in)


def _lane_is_first_head():
    return lax.broadcasted_iota(jnp.int32, (1, LANES), 1) < DH_C


def _ctx_attn_kernel(q_ref, k_ref, v_ref, o_ref):
    first = _lane_is_first_head()
    for j in range(q_ref.shape[1] // LANES):
        cols = slice(j * LANES, (j + 1) * LANES)
        q, k, v = q_ref[:, cols], k_ref[:, cols], v_ref[:, cols]
        outs = []
        for sel in (first, jnp.logical_not(first)):
            s = _dot_nt(jnp.where(sel, q, jnp.zeros_like(q)), k)
            p = jnp.exp(s - jnp.max(s, axis=-1, keepdims=True))
            l = jnp.sum(p, axis=-1, keepdims=True)
            outs.append(_dot(p.astype(BF16), v) / l)
        o_ref[:, cols] = jnp.where(first, outs[0], outs[1]).astype(BF16)


def _ctx_attn(q, k, v, seq_len):
    t, e = q.shape
    spec = pl.BlockSpec((seq_len, e), lambda b: (b, 0))
    return pl.pallas_call(
        _ctx_attn_kernel,
        grid=(t // seq_len,),
        in_specs=[spec, spec, spec],
        out_specs=spec,
        out_shape=jax.ShapeDtypeStruct((t, e), BF16),
        compiler_params=_params(1),
        name="ctx_attn",
    )(q, k, v)


def _key_row_start(r_blk, rows):
    return jnp.clip(r_blk * ATT_Q_ROWS - WIN_R // 2, 0, rows - ATT_K_ROWS)


def _nbr_bias_build(rows, rpb_ref, bias_ref):
    n_blk = rows // ATT_Q_ROWS
    kc = lax.broadcasted_iota(jnp.int32, (GRID_W, LANES), 0)
    lane = lax.broadcasted_iota(jnp.int32, (GRID_W, LANES), 1)
    qc = lane & (GRID_W - 1)
    c0 = jnp.clip(qc - WIN_C // 2, 0, GRID_W - WIN_C)
    col_in = (kc >= c0) & (kc < c0 + WIN_C)
    first = lane < GRID_W
    neg = jnp.full((GRID_W, LANES), -jnp.inf, F32)
    for hh in range(2):
        lo, hi = [], []
        for dr in range(2 * WIN_R - 1):
            row = jnp.broadcast_to(rpb_ref[hh, dr:dr + 1, :] * LOG2_E, (GRID_W, LANES))
            lo.append(pltpu.roll(row, LANES - (WIN_C - 1), 1, stride=1, stride_axis=0))
            hi.append(pltpu.roll(row, GRID_W - (WIN_C - 1), 1, stride=1, stride_axis=0))
        for var, r_blk in enumerate((0, 1, n_blk - 1)):
            ks = min(max(r_blk * ATT_Q_ROWS - WIN_R // 2, 0), rows - ATT_K_ROWS)
            for ki in range(ATT_K_ROWS):
                kr = ks + ki
                for qp in range(ATT_Q_ROWS // 2):
                    halves = []
                    for half, src in enumerate((lo, hi)):
                        r = r_blk * ATT_Q_ROWS + 2 * qp + half
                        r0 = min(max(r - WIN_R // 2, 0), rows - WIN_R)
                        halves.append(src[kr - r + WIN_R - 1] if r0 <= kr < r0 + WIN_R else neg)
                    blk = jnp.where(col_in, jnp.where(first, halves[0], halves[1]), neg)
                    bias_ref[hh, var, ki * GRID_W:(ki + 1) * GRID_W, qp * LANES:(qp + 1) * LANES] = blk


def _nbr_attn_kernel(rows, rpb_ref, q_ref, k_ref, v_ref, ck_ref, cv_ref, o_ref, bias_ref, kc_ref, vx_ref, s_ref,
                     p_ref, oa_ref):
    n_blk = rows // ATT_Q_ROWS
    tq = ATT_Q_ROWS * GRID_W
    nk = ATT_K_ROWS * GRID_W
    seq = q_ref.shape[0]
    n_tok_tiles = seq // MXU_TILE
    n_ctx_tiles = kc_ref.shape[0] // MXU_TILE

    @pl.when(pl.program_id(1) == 0)
    def _():
        _nbr_bias_build(rows, rpb_ref, bias_ref)

    first = _lane_is_first_head()
    head_lanes = (first, jnp.logical_not(first))
    kc_ref[...] = ck_ref[...].T.astype(BF16)
    v_t = v_ref[...].astype(F32).T
    top = lax.broadcasted_iota(jnp.int32, (LANES, 1), 0) < DH_C
    for hh, sel in enumerate((top, jnp.logical_not(top))):
        for kt in range(n_tok_tiles):
            vx_ref[hh, kt] = jnp.where(sel, v_t[:, kt * MXU_TILE:(kt + 1) * MXU_TILE], 1.0).astype(BF16)
        for kt in range(n_ctx_tiles):
            vx_ref[hh, n_tok_tiles + kt] = jnp.where(
                sel, cv_ref[:, kt * MXU_TILE:(kt + 1) * MXU_TILE], 1.0).astype(BF16)

    def q_rows(r):
        return pl.ds(pl.multiple_of(r * tq, tq), tq)

    def k_start(r):
        return pl.multiple_of(_key_row_start(r, rows) * GRID_W, MXU_TILE)

    def logits(r, hh):
        q = q_ref[q_rows(r), :]
        qh = jnp.where(head_lanes[hh], q, jnp.zeros_like(q))
        var = jnp.where(r == 0, 0, jnp.where(r == n_blk - 1, 2, 1))
        s_ref[hh, 0:nk, :] = _dot_nt(k_ref[pl.ds(k_start(r), nk), :], qh) + bias_ref[hh, var]
        s_ref[hh, nk:, :] = _dot_nt(kc_ref[...], qh)

    def softmax(hh):
        s = s_ref[hh]
        p_ref[hh] = jnp.exp2((s - jnp.max(s, axis=0, keepdims=True)).astype(BF16))

    def weighted_values(r, hh):
        kt0 = k_start(r) // MXU_TILE
        tiles = [vx_ref[hh, kt0 + i] for i in range(nk // MXU_TILE)]
        tiles += [vx_ref[hh, n_tok_tiles + i] for i in range(n_ctx_tiles)]
        return _dot(jnp.concatenate(tiles, axis=1), p_ref[hh])

    def emit(r, o_second):
        o_first = oa_ref[...]
        num = jnp.concatenate([o_first[0:DH_C], o_second[DH_C:]], axis=0)
        den = jnp.concatenate([o_first[DH_C:], o_second[0:DH_C]], axis=0)
        o_ref[q_rows(r), :] = (num / den).T.astype(BF16)

    p_ref[1] = jnp.ones(p_ref.shape[1:], BF16)
    oa_ref[...] = jnp.ones(oa_ref.shape, F32)
    logits(0, 0)

    def block(r, carry):
        prev = jnp.maximum(r - 1, 0)
        emit(prev, weighted_values(prev, 1))
        logits(r, 1)
        softmax(0)
        oa_ref[...] = weighted_values(r, 0)
        logits(jnp.minimum(r + 1, n_blk - 1), 0)
        softmax(1)
        return carry

    lax.fori_loop(0, n_blk, block, 0, unroll=2)
    emit(n_blk - 1, weighted_values(n_blk - 1, 1))


def _nbr_attn(q, k, v, cache_k, cache_v, cache_layer, rpb, n_batch):
    t, e = q.shape
    seq = t // n_batch
    rows = seq // GRID_W
    n_cached, past = cache_k.shape[1:3]
    ck = jnp.transpose(cache_k, (0, 1, 3, 4, 2)).reshape(n_batch * n_cached, e, past)
    cv = jnp.transpose(cache_v, (0, 1, 3, 4, 2)).reshape(n_batch * n_cached, e, past)
    tq, nk = ATT_Q_ROWS * GRID_W, ATT_K_ROWS * GRID_W
    assert nk % MXU_TILE == 0 and past % MXU_TILE == 0 and (ATT_Q_ROWS * GRID_W) % MXU_TILE == 0
    n_dr, n_dc = rpb.shape[1:]
    rpb_pad = jnp.pad(rpb.astype(F32)[:, :, ::-1], ((0, 0), (0, 2 * WIN_R - n_dr), (0, LANES - n_dc)))
    tok_spec = pl.BlockSpec((seq, LANES), lambda j, b: (b, j))
    ctx_spec = pl.BlockSpec((None, LANES, past), lambda j, b: (b * n_cached + cache_layer, j, 0))
    return pl.pallas_call(
        functools.partial(_nbr_attn_kernel, rows),
        grid=(e // LANES, n_batch),
        in_specs=[pl.BlockSpec((2, 2 * WIN_R, LANES), lambda j, b: (j, 0, 0)),
                  tok_spec, tok_spec, tok_spec, ctx_spec, ctx_spec],
        out_specs=tok_spec,
        out_shape=jax.ShapeDtypeStruct((t, e), BF16),
        scratch_shapes=[
            pltpu.VMEM((2, 3, nk, tq), F32),
            pltpu.VMEM((past, LANES), BF16),
            pltpu.VMEM((2, (seq + past) // MXU_TILE, LANES, MXU_TILE), BF16),
            pltpu.VMEM((2, nk + past, tq), F32),
            pltpu.VMEM((2, nk + past, tq), BF16),
            pltpu.VMEM((LANES, tq), F32),
        ],
        compiler_params=_params(2),
        name="nbr_attn",
    )(rpb_pad, q, k, v, ck, cv)


def _na_out_kernel(x_ref, a_ref, sg_ref, mod_ref, wout_ref, g_ref, b_ref, o_ref):
    tm, d = x_ref.shape
    _, _, gate = _split_mod(mod_ref, d)
    y = (a_ref[...].astype(F32) * sg_ref[...].astype(F32)).astype(BF16)
    o = _dot(y, wout_ref[...].astype(BF16))
    o_ref[...] = _resid_layer_norm(x_ref[...], gate, o, g_ref[...], b_ref[...])


def _na_out(x2d, attn, sg, seq_len, per_sample, mods, layer, w_out, ln_g, ln_b):
    t, d = x2d.shape
    tm = ROW_TILE
    tps = max(seq_len // tm, 1)
    e = w_out.shape[0]
    return pl.pallas_call(
        _na_out_kernel,
        grid=(t // tm,),
        in_specs=[
            pl.BlockSpec((tm, d), lambda i: (i, 0)),
            pl.BlockSpec((tm, e), lambda i: (i, 0)),
            pl.BlockSpec((tm, e), lambda i: (i, 0)),
            _mod_spec(layer, d, tps, per_sample),
            _const_spec(w_out.shape),
            _const_spec((1, d)),
            _const_spec((1, d)),
        ],
        out_specs=pl.BlockSpec((tm, d), lambda i: (i, 0)),
        out_shape=jax.ShapeDtypeStruct((t, d), F32),
        compiler_params=_params(1),
        name="na_out",
    )(x2d, attn, sg, mods, w_out, ln_g.reshape(1, d), ln_b.reshape(1, d))


def kernel(x_prompt, x_sample, c, cache_k, cache_v, c_ctx, w_mod, b_mod, ln_g, ln_b, pool_w_in, pool_w_grp,
           pool_scale, pool_w_out, sgu_w_in, sgu_ln_g, sgu_ln_b, sgu_w_s, sgu_b_s, sgu_w_out, na_w_in, na_rpb,
           na_w_out):
    n_p, seq_p, d = x_prompt.shape
    n_s, seq_s, _ = x_sample.shape
    assert n_s + 1 <= N_COND_ROWS and (n_p * seq_p) % ROW_TILE == 0 and seq_s % ROW_TILE == 0
    assert seq_p % CHUNK == 0 and (seq_p % ROW_TILE == 0 or ROW_TILE % seq_p == 0)
    assert seq_s % (GRID_W * ATT_Q_ROWS) == 0 and seq_s // GRID_W >= ATT_K_ROWS
    conds = jnp.zeros((N_COND_ROWS, d), F32).at[0].set(c_ctx).at[1:1 + n_s].set(c)
    mods = _mods(conds, w_mod, b_mod)
    pool_w_fold = _pool_fold(pool_w_in, pool_w_grp)
    pool_band = _pool_band()

    yp = x_prompt.reshape(n_p * seq_p, d)
    ys = x_sample.reshape(n_s * seq_s, d)
    streams = ((seq_p, False), (seq_s, True))
    ctx_k = ctx_v = None
    for i in range(DEPTH):
        kind, j = i % N_MIXERS, i // N_MIXERS
        ys_in = (yp, ys)
        outs = []
        if kind == 0:
            for x2d, (seq, per_sample) in zip(ys_in, streams):
                outs.append(_pool_layer(x2d, seq, per_sample, mods, i, j, pool_band, pool_w_fold, pool_w_in,
                                        pool_scale, pool_w_out, ln_g[i], ln_b[i]))
        elif kind == 1:
            w_in, w_s, w_out = sgu_w_in[j].astype(BF16), sgu_w_s[j].astype(BF16), sgu_w_out[j].astype(BF16)
            for x2d, (seq, per_sample) in zip(ys_in, streams):
                outs.append(_sgu_layer(x2d, seq, per_sample, mods, i, w_in, sgu_ln_g[j], sgu_ln_b[j], w_s,
                                       sgu_b_s[j].T, w_out, ln_g[i], ln_b[i]))
        else:
            w_in, w_out = na_w_in[j], na_w_out[j]
            q, k, v, sg, kf, vf = _na_proj(yp, seq_p, False, mods, i, w_in, True, DH_C ** -0.5)
            ctx_k = kf.reshape(n_p, 1, H_C, DH_C, seq_p).transpose(0, 1, 4, 2, 3)
            ctx_v = vf.reshape(n_p, 1, H_C, DH_C, seq_p).transpose(0, 1, 4, 2, 3)
            attn = _ctx_attn(q, k, v, seq_p)
            outs.append(_na_out(yp, attn, sg, seq_p, False, mods, i, w_out, ln_g[i], ln_b[i]))
            q, k, v, sg = _na_proj(ys, seq_s, True, mods, i, w_in, False, DH_C ** -0.5 * LOG2_E)
            attn = _nbr_attn(q, k, v, cache_k, cache_v, j, na_rpb[j], n_s)
            outs.append(_na_out(ys, attn, sg, seq_s, True, mods, i, w_out, ln_g[i], ln_b[i]))
        yp, ys = outs
    return (yp.reshape(n_p, seq_p, d), ys.reshape(n_s, seq_s, d), ctx_k, ctx_v)
```

```python
import functools

import jax
import jax.numpy as jnp
import numpy as np
from jax import lax
from jax.experimental import pallas as pl
from jax.experimental.pallas import tpu as pltpu

F32 = jnp.float32
BF16 = jnp.bfloat16

DEPTH = 4
N_MIXERS = 3
POOL_WINDOWS = (2, 4, 8, 16)
POOL_HALO = 16
POOL_BLOCK = 128
CHUNK = 128
H_B = 8
H_C = 16
DH_C = 64
GRID_W = 64
WIN_R = 8
WIN_C = 16
DEEPNORM_ALPHA = (2 * DEPTH) ** 0.25
LN_EPS = 1e-5
LOG2_E = float(np.log2(np.e))

N_COND_ROWS = 8
ROW_TILE = 512
ATT_Q_ROWS = 4
ATT_K_ROWS = 12
LANES = 128
MXU_TILE = 256
VMEM_LIMIT = 56 * 1024 * 1024


def _const_spec(shape):
    nd = len(shape)
    return pl.BlockSpec(shape, lambda *_: (0,) * nd, pipeline_mode=pl.Buffered(1))


def _params(n_axes):
    return pltpu.CompilerParams(dimension_semantics=("arbitrary",) * n_axes, vmem_limit_bytes=VMEM_LIMIT)


def _mod_spec(layer, d, tiles_per_seq, per_sample):
    base = layer * N_COND_ROWS
    if per_sample:
        return pl.BlockSpec((1, 1, 3 * d), lambda i: (base + 1 + i // tiles_per_seq, 0, 0))
    return pl.BlockSpec((1, 1, 3 * d), lambda i: (base, 0, 0))


def _split_mod(mod_ref, d):
    m = mod_ref[0]
    return m[:, :d], m[:, d:2 * d], m[:, 2 * d:]


def _silu(x):
    return x / (1.0 + jnp.exp(-x))


def _gelu_tanh_x2(x):
    c = np.float32(np.sqrt(2.0 / np.pi))
    return x * (1.0 + jnp.tanh(x * (c + (c * np.float32(0.044715)) * (x * x))))


def _layer_norm(x, g, b, eps=LN_EPS):
    mu = jnp.mean(x, axis=-1, keepdims=True)
    d = x - mu
    var = jnp.mean(d * d, axis=-1, keepdims=True)
    return d * lax.rsqrt(var + eps) * g + b


def _resid_layer_norm(x, gate, o, g, b):
    return _layer_norm(x + (gate * (1.0 / DEEPNORM_ALPHA)) * o, g, b, LN_EPS / DEEPNORM_ALPHA ** 2)


def _dot(a, b):
    return jnp.dot(a, b, preferred_element_type=F32)


def _dot_nt(a, b):
    return lax.dot_general(a, b, (((1,), (1,)), ((), ())), preferred_element_type=F32)


def _mods_kernel(cond_ref, w_ref, b_ref, o_ref):
    a = _silu(cond_ref[...]).astype(BF16)
    o_ref[0] = _dot(a, w_ref[0].astype(BF16)) + b_ref[0]


def _mods(conds, w_mod, b_mod):
    depth, d, n = w_mod.shape
    tn = n
    out = pl.pallas_call(
        _mods_kernel,
        grid=(depth, n // tn),
        in_specs=[
            pl.BlockSpec((N_COND_ROWS, d), lambda l, j: (0, 0)),
            pl.BlockSpec((1, d, tn), lambda l, j: (l, 0, j)),
            pl.BlockSpec((1, 1, tn), lambda l, j: (l, 0, j)),
        ],
        out_specs=pl.BlockSpec((1, N_COND_ROWS, tn), lambda l, j: (l, 0, j)),
        out_shape=jax.ShapeDtypeStruct((depth, N_COND_ROWS, n), F32),
        compiler_params=_params(2),
        name="adaln_mods",
    )(conds, w_mod, b_mod.reshape(depth, 1, n))
    return out.reshape(depth * N_COND_ROWS, 1, n)


def _pool_fold_kernel(win_x_ref, wgrp_ref, fold_ref):
    fold_ref[0] = _dot(win_x_ref[0].astype(BF16), wgrp_ref[0, 0].astype(BF16)).astype(BF16)


def _pool_fold(w_in, w_grp):
    n_layers, d, two_e = w_in.shape
    n_grp, grp = w_grp.shape[1:3]
    e_dim = two_e // 2
    return pl.pallas_call(
        _pool_fold_kernel,
        grid=(n_layers, n_grp),
        in_specs=[
            pl.BlockSpec((1, d, grp), lambda l, g: (l, 0, g)),
            pl.BlockSpec((1, 1, grp, grp), lambda l, g: (l, g, 0, 0)),
        ],
        out_specs=pl.BlockSpec((1, d, grp), lambda l, g: (l, 0, g)),
        out_shape=jax.ShapeDtypeStruct((n_layers, d, e_dim), BF16),
        compiler_params=_params(2),
        name="pool_fold",
    )(w_in, w_grp)


def _pool_band():
    t = np.arange(POOL_BLOCK)[:, None] + POOL_HALO
    e = np.arange(POOL_BLOCK + 2 * POOL_HALO)[None, :]
    return jnp.asarray(np.stack([(e >= t - w // 2) & (e <= t + w // 2 - 1) for w in POOL_WINDOWS]), BF16)


def _pool_kernel(tiles_per_seq, seq_len, x_ref, xp_ref, xn_ref, mod_ref, band_ref, wfold_ref, wgate_ref, psc_ref,
                 wout_ref, g_ref, b_ref, o_ref, e_ref, u_ref):
    tm, d = x_ref.shape
    e_dim = psc_ref.shape[1]
    n_win = len(POOL_WINDOWS)
    grp = e_dim // n_win
    it = pl.program_id(0) % tiles_per_seq
    shift, scale, gate = _split_mod(mod_ref, d)
    x = x_ref[...]
    hb = (x * (1.0 + scale) + shift).astype(BF16)
    hp = (xp_ref[...] * (1.0 + scale) + shift).astype(BF16)
    hn = (xn_ref[...] * (1.0 + scale) + shift).astype(BF16)
    zeros = jnp.zeros((POOL_HALO, d), BF16)
    e_ref[0:POOL_HALO] = jnp.where(it != 0, hp, zeros)
    e_ref[POOL_HALO:POOL_HALO + tm] = hb
    e_ref[POOL_HALO + tm:2 * POOL_HALO + tm] = jnp.where(it != tiles_per_seq - 1, hn, zeros)
    u_ref[...] = _dot(e_ref[...], wfold_ref[...]).astype(BF16)

    t = it * tm + lax.broadcasted_iota(jnp.int32, (tm, 1), 0)
    mixed = []
    for gi, w in enumerate(POOL_WINDOWS):
        cols = slice(gi * grp, (gi + 1) * grp)
        lo = jnp.maximum(t - w // 2, 0)
        hi = jnp.minimum(t + w // 2 - 1, seq_len - 1)
        inv_cnt = 1.0 / (hi - lo + 1).astype(F32)
        sums = jnp.concatenate(
            [_dot(band_ref[gi], u_ref[rb * POOL_BLOCK:(rb + 1) * POOL_BLOCK + 2 * POOL_HALO, cols])
             for rb in range(tm // POOL_BLOCK)], axis=0)
        mixed.append(sums * inv_cnt - u_ref[POOL_HALO:POOL_HALO + tm, cols].astype(F32))
    mixed = jnp.concatenate(mixed, axis=1)
    gate_pre = _dot(hb, wgate_ref[...].astype(BF16))
    y = (mixed * psc_ref[...] * _silu(gate_pre)).astype(BF16)
    o = _dot(y, wout_ref[...].astype(BF16))
    o_ref[...] = _resid_layer_norm(x, gate, o, g_ref[...], b_ref[...])


def _pool_layer(x2d, seq_len, per_sample, mods, layer, j, band, w_fold, w_in, p_scale, w_out, ln_g, ln_b):
    t, d = x2d.shape
    tm = min(ROW_TILE, seq_len)
    tps = seq_len // tm
    e_dim = w_out.shape[1]
    hb = tm // POOL_HALO
    last = t // POOL_HALO - 1

    def layer_spec(shape):
        return pl.BlockSpec((None,) + shape, lambda i: (j,) + (0,) * len(shape), pipeline_mode=pl.Buffered(1))

    return pl.pallas_call(
        functools.partial(_pool_kernel, tps, seq_len),
        grid=(t // tm,),
        in_specs=[
            pl.BlockSpec((tm, d), lambda i: (i, 0)),
            pl.BlockSpec((POOL_HALO, d), lambda i: (jnp.maximum(i * hb - 1, 0), 0)),
            pl.BlockSpec((POOL_HALO, d), lambda i: (jnp.minimum((i + 1) * hb, last), 0)),
            _mod_spec(layer, d, tps, per_sample),
            _const_spec(band.shape),
            layer_spec((d, e_dim)),
            pl.BlockSpec((None, d, e_dim), lambda i: (j, 0, 1), pipeline_mode=pl.Buffered(1)),
            layer_spec((1, e_dim)),
            layer_spec((e_dim, d)),
            _const_spec((1, d)),
            _const_spec((1, d)),
        ],
        out_specs=pl.BlockSpec((tm, d), lambda i: (i, 0)),
        out_shape=jax.ShapeDtypeStruct((t, d), F32),
        scratch_shapes=[pltpu.VMEM((tm + 2 * POOL_HALO, d), BF16),
                        pltpu.VMEM((tm + 2 * POOL_HALO, e_dim), BF16)],
        compiler_params=_params(1),
        name="pool_layer",
    )(x2d, x2d, x2d, mods, band, w_fold, w_in, p_scale.reshape(-1, 1, e_dim), w_out,
      ln_g.reshape(1, d), ln_b.reshape(1, d))


def _sgu_kernel(x_ref, mod_ref, win_ref, lg_ref, lb_ref, ws_ref, bs_ref, wout_ref, g_ref, b_ref, o_ref, y_ref):
    tm, d = x_ref.shape
    e_dim = lg_ref.shape[1]
    dh = e_dim // H_B
    shift, scale, gate = _split_mod(mod_ref, d)
    x = x_ref[...]
    hb = (x * (1.0 + scale) + shift).astype(BF16)
    v2 = _gelu_tanh_x2(_dot(hb, win_ref[:, e_dim:2 * e_dim]))
    v = _layer_norm(v2, lg_ref[...], lb_ref[...], 4.0 * LN_EPS).astype(BF16)
    u2 = _gelu_tanh_x2(_dot(hb, win_ref[:, 0:e_dim]))
    ug = u2 * _silu(_dot(hb, win_ref[:, 2 * e_dim:3 * e_dim]))
    for c in range(tm // CHUNK):
        rows = slice(c * CHUNK, (c + 1) * CHUNK)
        for hh in range(H_B):
            cols = slice(hh * dh, (hh + 1) * dh)
            sv = _dot(ws_ref[hh], v[rows, cols]) + bs_ref[:, hh:hh + 1]
            y_ref[rows, cols] = (ug[rows, cols] * sv).astype(BF16)
    o2 = _dot(y_ref[...], wout_ref[...])
    o_ref[...] = _resid_layer_norm(x, 0.5 * gate, o2, g_ref[...], b_ref[...])


def _sgu_layer(x2d, seq_len, per_sample, mods, layer, w_in, sln_g, sln_b, w_s, b_s_t, w_out, ln_g, ln_b):
    t, d = x2d.shape
    tm = ROW_TILE
    tps = max(seq_len // tm, 1)
    e_dim = w_out.shape[0]
    return pl.pallas_call(
        _sgu_kernel,
        grid=(t // tm,),
        in_specs=[
            pl.BlockSpec((tm, d), lambda i: (i, 0)),
            _mod_spec(layer, d, tps, per_sample),
            _const_spec(w_in.shape),
            _const_spec((1, e_dim)),
            _const_spec((1, e_dim)),
            _const_spec(w_s.shape),
            _const_spec(b_s_t.shape),
            _const_spec(w_out.shape),
            _const_spec((1, d)),
            _const_spec((1, d)),
        ],
        out_specs=pl.BlockSpec((tm, d), lambda i: (i, 0)),
        out_shape=jax.ShapeDtypeStruct((t, d), F32),
        scratch_shapes=[pltpu.VMEM((tm, e_dim), BF16)],
        compiler_params=_params(1),
        name="sgu_layer",
    )(x2d, mods, w_in, sln_g.reshape(1, e_dim), sln_b.reshape(1, e_dim), w_s, b_s_t, w_out,
      ln_g.reshape(1, d), ln_b.reshape(1, d))


def _na_proj_kernel(emit_f32_kv, q_scale, x_ref, mod_ref, win_ref, *out_refs):
    tm, d = x_ref.shape
    e = win_ref.shape[1] // 4
    shift, scale, _ = _split_mod(mod_ref, d)
    hb = (x_ref[...] * (1.0 + scale) + shift).astype(BF16)
    q_ref, k_ref, v_ref, sg_ref = out_refs[:4]
    q_ref[...] = (_dot(hb, win_ref[:, 0:e].astype(BF16)) * q_scale).astype(BF16)
    k = _dot(hb, win_ref[:, e:2 * e].astype(BF16))
    v = _dot(hb, win_ref[:, 2 * e:3 * e].astype(BF16))
    k_ref[...] = k.astype(BF16)
    v_ref[...] = v.astype(BF16)
    sg_ref[...] = _silu(_dot(hb, win_ref[:, 3 * e:4 * e].astype(BF16))).astype(BF16)
    if emit_f32_kv:
        seq = out_refs[4].shape[2]
        for kv, ref in ((k, out_refs[4]), (v, out_refs[5])):
            for sq in range(tm // seq):
                ref[sq] = kv[sq * seq:(sq + 1) * seq, :].T


def _na_proj(x2d, seq_len, per_sample, mods, layer, w_in, emit_f32_kv, q_scale):
    t, d = x2d.shape
    tm = ROW_TILE
    tps = max(seq_len // tm, 1)
    e = w_in.shape[1] // 4
    row_spec = pl.BlockSpec((tm, e), lambda i: (i, 0))
    out_specs = [row_spec] * 4
    out_shape = [jax.ShapeDtypeStruct((t, e), BF16)] * 4
    if emit_f32_kv:
        out_specs += [pl.BlockSpec((tm // seq_len, e, seq_len), lambda i: (i, 0, 0))] * 2
        out_shape += [jax.ShapeDtypeStruct((t // seq_len, e, seq_len), F32)] * 2
    return pl.pallas_call(
        functools.partial(_na_proj_kernel, emit_f32_kv, q_scale),
        grid=(t // tm,),
        in_specs=[
            pl.BlockSpec((tm, d), lambda i: (i, 0)),
            _mod_spec(layer, d, tps, per_sample),
            _const_spec(w_in.shape),
        ],
        out_specs=out_specs,
        out_shape=out_shape,
        compiler_params=_params(1),
        name="na_proj",
    )(x2d, mods, w_in)


def _lane_is_first_head():
    return lax.broadcasted_iota(jnp.int32, (1, LANES), 1) < DH_C


def _ctx_attn_kernel(q_ref, k_ref, v_ref, sg_ref, o_ref):
    first = _lane_is_first_head()
    for j in range(q_ref.shape[1] // LANES):
        cols = slice(j * LANES, (j + 1) * LANES)
        q, k, v = q_ref[:, cols], k_ref[:, cols], v_ref[:, cols]
        outs = []
        for sel in (first, jnp.logical_not(first)):
            s = _dot_nt(jnp.where(sel, q, jnp.zeros_like(q)), k)
            p = jnp.exp(s - jnp.max(s, axis=-1, keepdims=True))
            l = jnp.sum(p, axis=-1, keepdims=True)
            outs.append(_dot(p.astype(BF16), v) / l)
        o_ref[:, cols] = (jnp.where(first, outs[0], outs[1]) * sg_ref[:, cols].astype(F32)).astype(BF16)


def _ctx_attn(q, k, v, sg, seq_len):
    t, e = q.shape
    spec = pl.BlockSpec((seq_len, e), lambda b: (b, 0))
    return pl.pallas_call(
        _ctx_attn_kernel,
        grid=(t // seq_len,),
        in_specs=[spec, spec, spec, spec],
        out_specs=spec,
        out_shape=jax.ShapeDtypeStruct((t, e), BF16),
        compiler_params=_params(1),
        name="ctx_attn",
    )(q, k, v, sg)


def _key_row_start(r_blk, rows):
    return jnp.clip(r_blk * ATT_Q_ROWS - WIN_R // 2, 0, rows - ATT_K_ROWS)


def _nbr_bias_build(rows, rpb_ref, bias_ref):
    n_blk = rows // ATT_Q_ROWS
    kc = lax.broadcasted_iota(jnp.int32, (GRID_W, LANES), 0)
    lane = lax.broadcasted_iota(jnp.int32, (GRID_W, LANES), 1)
    qc = lane & (GRID_W - 1)
    c0 = jnp.clip(qc - WIN_C // 2, 0, GRID_W - WIN_C)
    col_in = (kc >= c0) & (kc < c0 + WIN_C)
    first = lane < GRID_W
    neg = jnp.full((GRID_W, LANES), -jnp.inf, F32)
    for hh in range(2):
        lo, hi = [], []
        for dr in range(2 * WIN_R - 1):
            row = jnp.broadcast_to(rpb_ref[hh, dr:dr + 1, :] * LOG2_E, (GRID_W, LANES))
            lo.append(pltpu.roll(row, LANES - (WIN_C - 1), 1, stride=1, stride_axis=0))
            hi.append(pltpu.roll(row, GRID_W - (WIN_C - 1), 1, stride=1, stride_axis=0))
        for var, r_blk in enumerate((0, 1, n_blk - 1)):
            ks = min(max(r_blk * ATT_Q_ROWS - WIN_R // 2, 0), rows - ATT_K_ROWS)
            for ki in range(ATT_K_ROWS):
                kr = ks + ki
                for qp in range(ATT_Q_ROWS // 2):
                    halves = []
                    for half, src in enumerate((lo, hi)):
                        r = r_blk * ATT_Q_ROWS + 2 * qp + half
                        r0 = min(max(r - WIN_R // 2, 0), rows - WIN_R)
                        halves.append(src[kr - r + WIN_R - 1] if r0 <= kr < r0 + WIN_R else neg)
                    blk = jnp.where(col_in, jnp.where(first, halves[0], halves[1]), neg)
                    bias_ref[hh, var, ki * GRID_W:(ki + 1) * GRID_W, qp * LANES:(qp + 1) * LANES] = blk


def _nbr_attn_kernel(rows, rpb_ref, q_ref, k_ref, v_ref, sg_ref, ck_ref, cv_ref, o_ref, bias_ref, kc_ref, vx_ref,
                     s_ref, p_ref, oa_ref):
    n_blk = rows // ATT_Q_ROWS
    tq = ATT_Q_ROWS * GRID_W
    nk = ATT_K_ROWS * GRID_W
    seq = q_ref.shape[0]
    n_tok_tiles = seq // MXU_TILE
    n_ctx_tiles = kc_ref.shape[0] // MXU_TILE

    @pl.when(pl.program_id(1) == 0)
    def _():
        _nbr_bias_build(rows, rpb_ref, bias_ref)

    first = _lane_is_first_head()
    head_lanes = (first, jnp.logical_not(first))
    kc_ref[...] = ck_ref[...].T.astype(BF16)
    v_t = v_ref[...].astype(F32).T
    top = lax.broadcasted_iota(jnp.int32, (LANES, 1), 0) < DH_C
    for hh, sel in enumerate((top, jnp.logical_not(top))):
        for kt in range(n_tok_tiles):
            vx_ref[hh, kt] = jnp.where(sel, v_t[:, kt * MXU_TILE:(kt + 1) * MXU_TILE], 1.0).astype(BF16)
        for kt in range(n_ctx_tiles):
            vx_ref[hh, n_tok_tiles + kt] = jnp.where(
                sel, cv_ref[:, kt * MXU_TILE:(kt + 1) * MXU_TILE], 1.0).astype(BF16)

    def q_rows(r):
        return pl.ds(pl.multiple_of(r * tq, tq), tq)

    def k_start(r):
        return pl.multiple_of(_key_row_start(r, rows) * GRID_W, MXU_TILE)

    def logits(r, hh):
        q = q_ref[q_rows(r), :]
        qh = jnp.where(head_lanes[hh], q, jnp.zeros_like(q))
        var = jnp.where(r == 0, 0, jnp.where(r == n_blk - 1, 2, 1))
        s_ref[hh, 0:nk, :] = _dot_nt(k_ref[pl.ds(k_start(r), nk), :], qh) + bias_ref[hh, var]
        s_ref[hh, nk:, :] = _dot_nt(kc_ref[...], qh)

    def softmax(hh):
        s = s_ref[hh]
        p_ref[hh] = jnp.exp2((s - jnp.max(s, axis=0, keepdims=True)).astype(BF16))

    def weighted_values(r, hh):
        kt0 = k_start(r) // MXU_TILE
        tiles = [vx_ref[hh, kt0 + i] for i in range(nk // MXU_TILE)]
        tiles += [vx_ref[hh, n_tok_tiles + i] for i in range(n_ctx_tiles)]
        return _dot(jnp.concatenate(tiles, axis=1), p_ref[hh])

    def emit(r, o_second):
        o_first = oa_ref[...]
        num = jnp.concatenate([o_first[0:DH_C], o_second[DH_C:]], axis=0)
        den = jnp.concatenate([o_first[DH_C:], o_second[0:DH_C]], axis=0)
        o_ref[q_rows(r), :] = ((num / den).T * sg_ref[q_rows(r), :].astype(F32)).astype(BF16)

    p_ref[1] = jnp.ones(p_ref.shape[1:], BF16)
    oa_ref[...] = jnp.ones(oa_ref.shape, F32)
    logits(0, 0)

    def block(r, carry):
        prev = jnp.maximum(r - 1, 0)
        emit(prev, weighted_values(prev, 1))
        logits(r, 1)
        softmax(0)
        oa_ref[...] = weighted_values(r, 0)
        logits(jnp.minimum(r + 1, n_blk - 1), 0)
        softmax(1)
        return carry

    lax.fori_loop(0, n_blk, block, 0, unroll=2)
    emit(n_blk - 1, weighted_values(n_blk - 1, 1))


def _nbr_attn(q, k, v, sg, cache_k, cache_v, cache_layer, rpb, n_batch):
    t, e = q.shape
    seq = t // n_batch
    rows = seq // GRID_W
    n_cached, past = cache_k.shape[1:3]
    ck = jnp.transpose(cache_k, (0, 1, 3, 4, 2)).reshape(n_batch * n_cached, e, past)
    cv = jnp.transpose(cache_v, (0, 1, 3, 4, 2)).reshape(n_batch * n_cached, e, past)
    tq, nk = ATT_Q_ROWS * GRID_W, ATT_K_ROWS * GRID_W
    assert nk % MXU_TILE == 0 and past % MXU_TILE == 0 and (ATT_Q_ROWS * GRID_W) % MXU_TILE == 0
    n_dr, n_dc = rpb.shape[1:]
    rpb_pad = jnp.pad(rpb.astype(F32)[:, :, ::-1], ((0, 0), (0, 2 * WIN_R - n_dr), (0, LANES - n_dc)))
    tok_spec = pl.BlockSpec((seq, LANES), lambda j, b: (b, j))
    ctx_spec = pl.BlockSpec((None, LANES, past), lambda j, b: (b * n_cached + cache_layer, j, 0))
    return pl.pallas_call(
        functools.partial(_nbr_attn_kernel, rows),
        grid=(e // LANES, n_batch),
        in_specs=[pl.BlockSpec((2, 2 * WIN_R, LANES), lambda j, b: (j, 0, 0)),
                  tok_spec, tok_spec, tok_spec, tok_spec, ctx_spec, ctx_spec],
        out_specs=tok_spec,
        out_shape=jax.ShapeDtypeStruct((t, e), BF16),
        scratch_shapes=[
            pltpu.VMEM((2, 3, nk, tq), F32),
            pltpu.VMEM((past, LANES), BF16),
            pltpu.VMEM((2, (seq + past) // MXU_TILE, LANES, MXU_TILE), BF16),
            pltpu.VMEM((2, nk + past, tq), F32),
            pltpu.VMEM((2, nk + past, tq), BF16),
            pltpu.VMEM((LANES, tq), F32),
        ],
        compiler_params=_params(2),
        name="nbr_attn",
    )(rpb_pad, q, k, v, sg, ck, cv)


def _na_out_kernel(x_ref, y_ref, mod_ref, wout_ref, g_ref, b_ref, o_ref):
    tm, d = x_ref.shape
    _, _, gate = _split_mod(mod_ref, d)
    o = _dot(y_ref[...], wout_ref[...].astype(BF16))
    o_ref[...] = _resid_layer_norm(x_ref[...], gate, o, g_ref[...], b_ref[...])


def _na_out(x2d, gated, seq_len, per_sample, mods, layer, w_out, ln_g, ln_b):
    t, d = x2d.shape
    tm = ROW_TILE
    tps = max(seq_len // tm, 1)
    e = w_out.shape[0]
    return pl.pallas_call(
        _na_out_kernel,
        grid=(t // tm,),
        in_specs=[
            pl.BlockSpec((tm, d), lambda i: (i, 0)),
            pl.BlockSpec((tm, e), lambda i: (i, 0)),
            _mod_spec(layer, d, tps, per_sample),
            _const_spec(w_out.shape),
            _const_spec((1, d)),
            _const_spec((1, d)),
        ],
        out_specs=pl.BlockSpec((tm, d), lambda i: (i, 0)),
        out_shape=jax.ShapeDtypeStruct((t, d), F32),
        compiler_params=_params(1),
        name="na_out",
    )(x2d, gated, mods, w_out, ln_g.reshape(1, d), ln_b.reshape(1, d))


def kernel(x_prompt, x_sample, c, cache_k, cache_v, c_ctx, w_mod, b_mod, ln_g, ln_b, pool_w_in, pool_w_grp,
           pool_scale, pool_w_out, sgu_w_in, sgu_ln_g, sgu_ln_b, sgu_w_s, sgu_b_s, sgu_w_out, na_w_in, na_rpb,
           na_w_out):
    n_p, seq_p, d = x_prompt.shape
    n_s, seq_s, _ = x_sample.shape
    assert n_s + 1 <= N_COND_ROWS and (n_p * seq_p) % ROW_TILE == 0 and seq_s % ROW_TILE == 0
    assert seq_p % CHUNK == 0 and (seq_p % ROW_TILE == 0 or ROW_TILE % seq_p == 0)
    assert seq_s % (GRID_W * ATT_Q_ROWS) == 0 and seq_s // GRID_W >= ATT_K_ROWS
    conds = jnp.zeros((N_COND_ROWS, d), F32).at[0].set(c_ctx).at[1:1 + n_s].set(c)
    mods = _mods(conds, w_mod, b_mod)
    pool_w_fold = _pool_fold(pool_w_in, pool_w_grp)
    pool_band = _pool_band()

    yp = x_prompt.reshape(n_p * seq_p, d)
    ys = x_sample.reshape(n_s * seq_s, d)
    streams = ((seq_p, False), (seq_s, True))
    ctx_k = ctx_v = None
    for i in range(DEPTH):
        kind, j = i % N_MIXERS, i // N_MIXERS
        ys_in = (yp, ys)
        outs = []
        if kind == 0:
            for x2d, (seq, per_sample) in zip(ys_in, streams):
                outs.append(_pool_layer(x2d, seq, per_sample, mods, i, j, pool_band, pool_w_fold, pool_w_in,
                                        pool_scale, pool_w_out, ln_g[i], ln_b[i]))
        elif kind == 1:
            w_in, w_s, w_out = sgu_w_in[j].astype(BF16), sgu_w_s[j].astype(BF16), sgu_w_out[j].astype(BF16)
            for x2d, (seq, per_sample) in zip(ys_in, streams):
                outs.append(_sgu_layer(x2d, seq, per_sample, mods, i, w_in, sgu_ln_g[j], sgu_ln_b[j], w_s,
                                       sgu_b_s[j].T, w_out, ln_g[i], ln_b[i]))
        else:
            w_in, w_out = na_w_in[j], na_w_out[j]
            q, k, v, sg, kf, vf = _na_proj(yp, seq_p, False, mods, i, w_in, True, DH_C ** -0.5)
            ctx_k = kf.reshape(n_p, 1, H_C, DH_C, seq_p).transpose(0, 1, 4, 2, 3)
            ctx_v = vf.reshape(n_p, 1, H_C, DH_C, seq_p).transpose(0, 1, 4, 2, 3)
            gated = _ctx_attn(q, k, v, sg, seq_p)
            outs.append(_na_out(yp, gated, seq_p, False, mods, i, w_out, ln_g[i], ln_b[i]))
            q, k, v, sg = _na_proj(ys, seq_s, True, mods, i, w_in, False, DH_C ** -0.5 * LOG2_E)
            gated = _nbr_attn(q, k, v, sg, cache_k, cache_v, j, na_rpb[j], n_s)
            outs.append(_na_out(ys, gated, seq_s, True, mods, i, w_out, ln_g[i], ln_b[i]))
        yp, ys = outs
    return (yp.reshape(n_p, seq_p, d), ys.reshape(n_s, seq_s, d), ctx_k, ctx_v)
```

```python
import functools

import jax
import jax.numpy as jnp
import numpy as np
from jax import lax
from jax.experimental import pallas as pl
from jax.experimental.pallas import tpu as pltpu

F32 = jnp.float32
BF16 = jnp.bfloat16

DEPTH = 4
N_MIXERS = 3
POOL_WINDOWS = (2, 4, 8, 16)
POOL_HALO = 16
POOL_BLOCK = 128
CHUNK = 128
H_B = 8
H_C = 16
DH_C = 64
GRID_W = 64
WIN_R = 8
WIN_C = 16
DEEPNORM_ALPHA = (2 * DEPTH) ** 0.25
LN_EPS = 1e-5
LOG2_E = float(np.log2(np.e))

N_COND_ROWS = 8
ROW_TILE = 512
ATT_Q_ROWS = 4
ATT_K_ROWS = 12
LANES = 128
MXU_TILE = 256
VMEM_LIMIT = 56 * 1024 * 1024


def _const_spec(shape):
    nd = len(shape)
    return pl.BlockSpec(shape, lambda *_: (0,) * nd, pipeline_mode=pl.Buffered(1))


def _params(n_axes):
    return pltpu.CompilerParams(dimension_semantics=("arbitrary",) * n_axes, vmem_limit_bytes=VMEM_LIMIT)


def _mod_spec(layer, d, tiles_per_seq, per_sample):
    base = layer * N_COND_ROWS
    if per_sample:
        return pl.BlockSpec((1, 1, 3 * d), lambda i: (base + 1 + i // tiles_per_seq, 0, 0))
    return pl.BlockSpec((1, 1, 3 * d), lambda i: (base, 0, 0))


def _split_mod(mod_ref, d):
    m = mod_ref[0]
    return m[:, :d], m[:, d:2 * d], m[:, 2 * d:]


def _silu(x):
    return x / (1.0 + jnp.exp(-x))


def _gelu_tanh_x2(x):
    c = np.float32(np.sqrt(2.0 / np.pi))
    return x * (1.0 + jnp.tanh(x * (c + (c * np.float32(0.044715)) * (x * x))))


def _layer_norm(x, g, b, eps=LN_EPS):
    mu = jnp.mean(x, axis=-1, keepdims=True)
    d = x - mu
    var = jnp.mean(d * d, axis=-1, keepdims=True)
    return d * lax.rsqrt(var + eps) * g + b


def _resid_layer_norm(x, gate, o, g, b):
    return _layer_norm(x + (gate * (1.0 / DEEPNORM_ALPHA)) * o, g, b, LN_EPS / DEEPNORM_ALPHA ** 2)


def _dot(a, b):
    return jnp.dot(a, b, preferred_element_type=F32)


def _dot_nt(a, b):
    return lax.dot_general(a, b, (((1,), (1,)), ((), ())), preferred_element_type=F32)


def _mods_kernel(cond_ref, w_ref, b_ref, o_ref):
    a = _silu(cond_ref[...]).astype(BF16)
    o_ref[0] = _dot(a, w_ref[0].astype(BF16)) + b_ref[0]


def _mods(conds, w_mod, b_mod):
    depth, d, n = w_mod.shape
    tn = n
    out = pl.pallas_call(
        _mods_kernel,
        grid=(depth, n // tn),
        in_specs=[
            pl.BlockSpec((N_COND_ROWS, d), lambda l, j: (0, 0)),
            pl.BlockSpec((1, d, tn), lambda l, j: (l, 0, j)),
            pl.BlockSpec((1, 1, tn), lambda l, j: (l, 0, j)),
        ],
        out_specs=pl.BlockSpec((1, N_COND_ROWS, tn), lambda l, j: (l, 0, j)),
        out_shape=jax.ShapeDtypeStruct((depth, N_COND_ROWS, n), F32),
        compiler_params=_params(2),
        name="adaln_mods",
    )(conds, w_mod, b_mod.reshape(depth, 1, n))
    return out.reshape(depth * N_COND_ROWS, 1, n)


def _pool_fold_kernel(win_x_ref, wgrp_ref, fold_ref):
    fold_ref[0] = _dot(win_x_ref[0].astype(BF16), wgrp_ref[0, 0].astype(BF16)).astype(BF16)


def _pool_fold(w_in, w_grp):
    n_layers, d, two_e = w_in.shape
    n_grp, grp = w_grp.shape[1:3]
    e_dim = two_e // 2
    return pl.pallas_call(
        _pool_fold_kernel,
        grid=(n_layers, n_grp),
        in_specs=[
            pl.BlockSpec((1, d, grp), lambda l, g: (l, 0, g)),
            pl.BlockSpec((1, 1, grp, grp), lambda l, g: (l, g, 0, 0)),
        ],
        out_specs=pl.BlockSpec((1, d, grp), lambda l, g: (l, 0, g)),
        out_shape=jax.ShapeDtypeStruct((n_layers, d, e_dim), BF16),
        compiler_params=_params(2),
        name="pool_fold",
    )(w_in, w_grp)


def _pool_band():
    t = np.arange(POOL_BLOCK)[:, None] + POOL_HALO
    e = np.arange(POOL_BLOCK + 2 * POOL_HALO)[None, :]
    return jnp.asarray(np.stack([(e >= t - w // 2) & (e <= t + w // 2 - 1) for w in POOL_WINDOWS]), BF16)


def _pool_kernel(tiles_per_seq, seq_len, x_ref, xp_ref, xn_ref, mod_ref, band_ref, wfold_ref, wgate_ref, psc_ref,
                 wout_ref, g_ref, b_ref, o_ref, e_ref, u_ref):
    tm, d = x_ref.shape
    e_dim = psc_ref.shape[1]
    n_win = len(POOL_WINDOWS)
    grp = e_dim // n_win
    it = pl.program_id(0) % tiles_per_seq
    shift, scale, gate = _split_mod(mod_ref, d)
    x = x_ref[...]
    hb = (x * (1.0 + scale) + shift).astype(BF16)
    hp = (xp_ref[...] * (1.0 + scale) + shift).astype(BF16)
    hn = (xn_ref[...] * (1.0 + scale) + shift).astype(BF16)
    zeros = jnp.zeros((POOL_HALO, d), BF16)
    e_ref[0:POOL_HALO] = jnp.where(it != 0, hp, zeros)
    e_ref[POOL_HALO:POOL_HALO + tm] = hb
    e_ref[POOL_HALO + tm:2 * POOL_HALO + tm] = jnp.where(it != tiles_per_seq - 1, hn, zeros)
    u_ref[...] = _dot(e_ref[...], wfold_ref[...]).astype(BF16)

    t = it * tm + lax.broadcasted_iota(jnp.int32, (tm, 1), 0)
    mixed = []
    for gi, w in enumerate(POOL_WINDOWS):
        cols = slice(gi * grp, (gi + 1) * grp)
        lo = jnp.maximum(t - w // 2, 0)
        hi = jnp.minimum(t + w // 2 - 1, seq_len - 1)
        inv_cnt = 1.0 / (hi - lo + 1).astype(F32)
        sums = jnp.concatenate(
            [_dot(band_ref[gi], u_ref[rb * POOL_BLOCK:(rb + 1) * POOL_BLOCK + 2 * POOL_HALO, cols])
             for rb in range(tm // POOL_BLOCK)], axis=0)
        mixed.append(sums * inv_cnt - u_ref[POOL_HALO:POOL_HALO + tm, cols].astype(F32))
    mixed = jnp.concatenate(mixed, axis=1)
    gate_pre = _dot(hb, wgate_ref[...].astype(BF16))
    y = (mixed * psc_ref[...] * _silu(gate_pre)).astype(BF16)
    o = _dot(y, wout_ref[...].astype(BF16))
    o_ref[...] = _resid_layer_norm(x, gate, o, g_ref[...], b_ref[...])


def _pool_layer(x2d, seq_len, per_sample, mods, layer, j, band, w_fold, w_in, p_scale, w_out, ln_g, ln_b):
    t, d = x2d.shape
    tm = min(ROW_TILE, seq_len)
    tps = seq_len // tm
    e_dim = w_out.shape[1]
    hb = tm // POOL_HALO
    last = t // POOL_HALO - 1

    def layer_spec(shape):
        return pl.BlockSpec((None,) + shape, lambda i: (j,) + (0,) * len(shape), pipeline_mode=pl.Buffered(1))

    return pl.pallas_call(
        functools.partial(_pool_kernel, tps, seq_len),
        grid=(t // tm,),
        in_specs=[
            pl.BlockSpec((tm, d), lambda i: (i, 0)),
            pl.BlockSpec((POOL_HALO, d), lambda i: (jnp.maximum(i * hb - 1, 0), 0)),
            pl.BlockSpec((POOL_HALO, d), lambda i: (jnp.minimum((i + 1) * hb, last), 0)),
            _mod_spec(layer, d, tps, per_sample),
            _const_spec(band.shape),
            layer_spec((d, e_dim)),
            pl.BlockSpec((None, d, e_dim), lambda i: (j, 0, 1), pipeline_mode=pl.Buffered(1)),
            layer_spec((1, e_dim)),
            layer_spec((e_dim, d)),
            _const_spec((1, d)),
            _const_spec((1, d)),
        ],
        out_specs=pl.BlockSpec((tm, d), lambda i: (i, 0)),
        out_shape=jax.ShapeDtypeStruct((t, d), F32),
        scratch_shapes=[pltpu.VMEM((tm + 2 * POOL_HALO, d), BF16),
                        pltpu.VMEM((tm + 2 * POOL_HALO, e_dim), BF16)],
        compiler_params=_params(1),
        name="pool_layer",
    )(x2d, x2d, x2d, mods, band, w_fold, w_in, p_scale.reshape(-1, 1, e_dim), w_out,
      ln_g.reshape(1, d), ln_b.reshape(1, d))


def _sgu_kernel(x_ref, mod_ref, win_ref, lg_ref, lb_ref, ws_ref, bs_ref, wout_ref, g_ref, b_ref, o_ref, y_ref):
    tm, d = x_ref.shape
    e_dim = lg_ref.shape[1]
    dh = e_dim // H_B
    shift, scale, gate = _split_mod(mod_ref, d)
    x = x_ref[...]
    hb = (x * (1.0 + scale) + shift).astype(BF16)
    v2 = _gelu_tanh_x2(_dot(hb, win_ref[:, e_dim:2 * e_dim]))
    v = _layer_norm(v2, lg_ref[...], lb_ref[...], 4.0 * LN_EPS).astype(BF16)
    u2 = _gelu_tanh_x2(_dot(hb, win_ref[:, 0:e_dim]))
    ug = u2 * _silu(_dot(hb, win_ref[:, 2 * e_dim:3 * e_dim]))
    for c in range(tm // CHUNK):
        rows = slice(c * CHUNK, (c + 1) * CHUNK)
        for hh in range(H_B):
            cols = slice(hh * dh, (hh + 1) * dh)
            sv = _dot(ws_ref[hh], v[rows, cols]) + bs_ref[:, hh:hh + 1]
            y_ref[rows, cols] = (ug[rows, cols] * sv).astype(BF16)
    o2 = _dot(y_ref[...], wout_ref[...])
    o_ref[...] = _resid_layer_norm(x, 0.5 * gate, o2, g_ref[...], b_ref[...])


def _sgu_layer(x2d, seq_len, per_sample, mods, layer, w_in, sln_g, sln_b, w_s, b_s_t, w_out, ln_g, ln_b):
    t, d = x2d.shape
    tm = ROW_TILE
    tps = max(seq_len // tm, 1)
    e_dim = w_out.shape[0]
    return pl.pallas_call(
        _sgu_kernel,
        grid=(t // tm,),
        in_specs=[
            pl.BlockSpec((tm, d), lambda i: (i, 0)),
            _mod_spec(layer, d, tps, per_sample),
            _const_spec(w_in.shape),
            _const_spec((1, e_dim)),
            _const_spec((1, e_dim)),
            _const_spec(w_s.shape),
            _const_spec(b_s_t.shape),
            _const_spec(w_out.shape),
            _const_spec((1, d)),
            _const_spec((1, d)),
        ],
        out_specs=pl.BlockSpec((tm, d), lambda i: (i, 0)),
        out_shape=jax.ShapeDtypeStruct((t, d), F32),
        scratch_shapes=[pltpu.VMEM((tm, e_dim), BF16)],
        compiler_params=_params(1),
        name="sgu_layer",
    )(x2d, mods, w_in, sln_g.reshape(1, e_dim), sln_b.reshape(1, e_dim), w_s, b_s_t, w_out,
      ln_g.reshape(1, d), ln_b.reshape(1, d))


def _na_proj_kernel(emit_f32_kv, pair_major, q_scale, x_ref, mod_ref, win_ref, *out_refs):
    tm, d = x_ref.shape
    e = win_ref.shape[1] // 4
    shift, scale, _ = _split_mod(mod_ref, d)
    hb = (x_ref[...] * (1.0 + scale) + shift).astype(BF16)
    def put(ref, val):
        if pair_major:
            for j in range(e // LANES):
                ref[j] = val[:, j * LANES:(j + 1) * LANES]
        else:
            ref[...] = val

    q_ref, k_ref, v_ref, sg_ref = out_refs[:4]
    put(q_ref, (_dot(hb, win_ref[:, 0:e].astype(BF16)) * q_scale).astype(BF16))
    k = _dot(hb, win_ref[:, e:2 * e].astype(BF16))
    v = _dot(hb, win_ref[:, 2 * e:3 * e].astype(BF16))
    put(k_ref, k.astype(BF16))
    put(v_ref, v.astype(BF16))
    put(sg_ref, _silu(_dot(hb, win_ref[:, 3 * e:4 * e].astype(BF16))).astype(BF16))
    if emit_f32_kv:
        seq = out_refs[4].shape[2]
        for kv, ref in ((k, out_refs[4]), (v, out_refs[5])):
            for sq in range(tm // seq):
                ref[sq] = kv[sq * seq:(sq + 1) * seq, :].T


def _na_proj(x2d, seq_len, per_sample, mods, layer, w_in, emit_f32_kv, pair_major, q_scale):
    t, d = x2d.shape
    tm = ROW_TILE
    tps = max(seq_len // tm, 1)
    e = w_in.shape[1] // 4
    if pair_major:
        out_specs = [pl.BlockSpec((e // LANES, tm, LANES), lambda i: (0, i, 0))] * 4
        out_shape = [jax.ShapeDtypeStruct((e // LANES, t, LANES), BF16)] * 4
    else:
        out_specs = [pl.BlockSpec((tm, e), lambda i: (i, 0))] * 4
        out_shape = [jax.ShapeDtypeStruct((t, e), BF16)] * 4
    if emit_f32_kv:
        out_specs += [pl.BlockSpec((tm // seq_len, e, seq_len), lambda i: (i, 0, 0))] * 2
        out_shape += [jax.ShapeDtypeStruct((t // seq_len, e, seq_len), F32)] * 2
    return pl.pallas_call(
        functools.partial(_na_proj_kernel, emit_f32_kv, pair_major, q_scale),
        grid=(t // tm,),
        in_specs=[
            pl.BlockSpec((tm, d), lambda i: (i, 0)),
            _mod_spec(layer, d, tps, per_sample),
            _const_spec(w_in.shape),
        ],
        out_specs=out_specs,
        out_shape=out_shape,
        compiler_params=_params(1),
        name="na_proj",
    )(x2d, mods, w_in)


def _lane_is_first_head():
    return lax.broadcasted_iota(jnp.int32, (1, LANES), 1) < DH_C


def _ctx_attn_kernel(q_ref, k_ref, v_ref, sg_ref, o_ref):
    first = _lane_is_first_head()
    for j in range(q_ref.shape[1] // LANES):
        cols = slice(j * LANES, (j + 1) * LANES)
        q, k, v = q_ref[:, cols], k_ref[:, cols], v_ref[:, cols]
        outs = []
        for sel in (first, jnp.logical_not(first)):
            s = _dot_nt(jnp.where(sel, q, jnp.zeros_like(q)), k)
            p = jnp.exp(s - jnp.max(s, axis=-1, keepdims=True))
            l = jnp.sum(p, axis=-1, keepdims=True)
            outs.append(_dot(p.astype(BF16), v) / l)
        o_ref[:, cols] = (jnp.where(first, outs[0], outs[1]) * sg_ref[:, cols].astype(F32)).astype(BF16)


def _ctx_attn(q, k, v, sg, seq_len):
    t, e = q.shape
    spec = pl.BlockSpec((seq_len, e), lambda b: (b, 0))
    return pl.pallas_call(
        _ctx_attn_kernel,
        grid=(t // seq_len,),
        in_specs=[spec, spec, spec, spec],
        out_specs=spec,
        out_shape=jax.ShapeDtypeStruct((t, e), BF16),
        compiler_params=_params(1),
        name="ctx_attn",
    )(q, k, v, sg)


def _key_row_start(r_blk, rows):
    return jnp.clip(r_blk * ATT_Q_ROWS - WIN_R // 2, 0, rows - ATT_K_ROWS)


def _nbr_bias_build(rows, rpb_ref, bias_ref):
    n_blk = rows // ATT_Q_ROWS
    kc = lax.broadcasted_iota(jnp.int32, (GRID_W, LANES), 0)
    lane = lax.broadcasted_iota(jnp.int32, (GRID_W, LANES), 1)
    qc = lane & (GRID_W - 1)
    c0 = jnp.clip(qc - WIN_C // 2, 0, GRID_W - WIN_C)
    col_in = (kc >= c0) & (kc < c0 + WIN_C)
    first = lane < GRID_W
    neg = jnp.full((GRID_W, LANES), -jnp.inf, F32)
    for hh in range(2):
        lo, hi = [], []
        for dr in range(2 * WIN_R - 1):
            row = jnp.broadcast_to(rpb_ref[hh, dr:dr + 1, :] * LOG2_E, (GRID_W, LANES))
            lo.append(pltpu.roll(row, LANES - (WIN_C - 1), 1, stride=1, stride_axis=0))
            hi.append(pltpu.roll(row, GRID_W - (WIN_C - 1), 1, stride=1, stride_axis=0))
        for var, r_blk in enumerate((0, 1, n_blk - 1)):
            ks = min(max(r_blk * ATT_Q_ROWS - WIN_R // 2, 0), rows - ATT_K_ROWS)
            for ki in range(ATT_K_ROWS):
                kr = ks + ki
                for qp in range(ATT_Q_ROWS // 2):
                    halves = []
                    for half, src in enumerate((lo, hi)):
                        r = r_blk * ATT_Q_ROWS + 2 * qp + half
                        r0 = min(max(r - WIN_R // 2, 0), rows - WIN_R)
                        halves.append(src[kr - r + WIN_R - 1] if r0 <= kr < r0 + WIN_R else neg)
                    blk = jnp.where(col_in, jnp.where(first, halves[0], halves[1]), neg)
                    bias_ref[hh, var, ki * GRID_W:(ki + 1) * GRID_W, qp * LANES:(qp + 1) * LANES] = blk


def _nbr_attn_kernel(rows, rpb_ref, q_ref, k_ref, v_ref, sg_ref, ck_ref, cv_ref, o_ref, bias_ref, kc_ref, vx_ref,
                     s_ref, p_ref, oa_ref):
    n_blk = rows // ATT_Q_ROWS
    tq = ATT_Q_ROWS * GRID_W
    nk = ATT_K_ROWS * GRID_W
    seq = q_ref.shape[0]
    n_tok_tiles = seq // MXU_TILE
    n_ctx_tiles = kc_ref.shape[0] // MXU_TILE

    @pl.when(pl.program_id(1) == 0)
    def _():
        _nbr_bias_build(rows, rpb_ref, bias_ref)

    first = _lane_is_first_head()
    head_lanes = (first, jnp.logical_not(first))
    kc_ref[...] = ck_ref[...].T.astype(BF16)
    v_t = v_ref[...].astype(F32).T
    top = lax.broadcasted_iota(jnp.int32, (LANES, 1), 0) < DH_C
    for hh, sel in enumerate((top, jnp.logical_not(top))):
        for kt in range(n_tok_tiles):
            vx_ref[hh, kt] = jnp.where(sel, v_t[:, kt * MXU_TILE:(kt + 1) * MXU_TILE], 1.0).astype(BF16)
        for kt in range(n_ctx_tiles):
            vx_ref[hh, n_tok_tiles + kt] = jnp.where(
                sel, cv_ref[:, kt * MXU_TILE:(kt + 1) * MXU_TILE], 1.0).astype(BF16)

    def q_rows(r):
        return pl.ds(pl.multiple_of(r * tq, tq), tq)

    def k_start(r):
        return pl.multiple_of(_key_row_start(r, rows) * GRID_W, MXU_TILE)

    def logits(r, hh):
        q = q_ref[q_rows(r), :]
        qh = jnp.where(head_lanes[hh], q, jnp.zeros_like(q))
        var = jnp.where(r == 0, 0, jnp.where(r == n_blk - 1, 2, 1))
        s_ref[hh, 0:nk, :] = _dot_nt(k_ref[pl.ds(k_start(r), nk), :], qh) + bias_ref[hh, var]
        s_ref[hh, nk:, :] = _dot_nt(kc_ref[...], qh)

    def softmax(hh):
        s = s_ref[hh]
        p_ref[hh] = jnp.exp2((s - jnp.max(s, axis=0, keepdims=True)).astype(BF16))

    def weighted_values(r, hh):
        kt0 = k_start(r) // MXU_TILE
        tiles = [vx_ref[hh, kt0 + i] for i in range(nk // MXU_TILE)]
        tiles += [vx_ref[hh, n_tok_tiles + i] for i in range(n_ctx_tiles)]
        return _dot(jnp.concatenate(tiles, axis=1), p_ref[hh])

    def emit(r, o_second):
        o_first = oa_ref[...]
        num = jnp.concatenate([o_first[0:DH_C], o_second[DH_C:]], axis=0)
        den = jnp.concatenate([o_first[DH_C:], o_second[0:DH_C]], axis=0)
        o_ref[q_rows(r), :] = ((num / den).T * sg_ref[q_rows(r), :].astype(F32)).astype(BF16)

    p_ref[1] = jnp.ones(p_ref.shape[1:], BF16)
    oa_ref[...] = jnp.ones(oa_ref.shape, F32)
    logits(0, 0)

    def block(r, carry):
        prev = jnp.maximum(r - 1, 0)
        emit(prev, weighted_values(prev, 1))
        logits(r, 1)
        softmax(0)
        oa_ref[...] = weighted_values(r, 0)
        logits(jnp.minimum(r + 1, n_blk - 1), 0)
        softmax(1)
        return carry

    lax.fori_loop(0, n_blk, block, 0, unroll=2)
    emit(n_blk - 1, weighted_values(n_blk - 1, 1))


def _nbr_attn(q, k, v, sg, cache_k, cache_v, cache_layer, rpb, n_batch):
    n_pairs, t, _ = q.shape
    e = n_pairs * LANES
    seq = t // n_batch
    rows = seq // GRID_W
    n_cached, past = cache_k.shape[1:3]
    ck = jnp.transpose(cache_k, (0, 1, 3, 4, 2)).reshape(n_batch * n_cached, e, past)
    cv = jnp.transpose(cache_v, (0, 1, 3, 4, 2)).reshape(n_batch * n_cached, e, past)
    tq, nk = ATT_Q_ROWS * GRID_W, ATT_K_ROWS * GRID_W
    assert nk % MXU_TILE == 0 and past % MXU_TILE == 0 and (ATT_Q_ROWS * GRID_W) % MXU_TILE == 0
    n_dr, n_dc = rpb.shape[1:]
    rpb_pad = jnp.pad(rpb.astype(F32)[:, :, ::-1], ((0, 0), (0, 2 * WIN_R - n_dr), (0, LANES - n_dc)))
    tok_spec = pl.BlockSpec((None, seq, LANES), lambda j, b: (j, b, 0))
    ctx_spec = pl.BlockSpec((None, LANES, past), lambda j, b: (b * n_cached + cache_layer, j, 0))
    return pl.pallas_call(
        functools.partial(_nbr_attn_kernel, rows),
        grid=(n_pairs, n_batch),
        in_specs=[pl.BlockSpec((2, 2 * WIN_R, LANES), lambda j, b: (j, 0, 0)),
                  tok_spec, tok_spec, tok_spec, tok_spec, ctx_spec, ctx_spec],
        out_specs=tok_spec,
        out_shape=jax.ShapeDtypeStruct((n_pairs, t, LANES), BF16),
        scratch_shapes=[
            pltpu.VMEM((2, 3, nk, tq), F32),
            pltpu.VMEM((past, LANES), BF16),
            pltpu.VMEM((2, (seq + past) // MXU_TILE, LANES, MXU_TILE), BF16),
            pltpu.VMEM((2, nk + past, tq), F32),
            pltpu.VMEM((2, nk + past, tq), BF16),
            pltpu.VMEM((LANES, tq), F32),
        ],
        compiler_params=_params(2),
        name="nbr_attn",
    )(rpb_pad, q, k, v, sg, ck, cv)


def _na_out_kernel(pair_major, x_ref, y_ref, mod_ref, wout_ref, g_ref, b_ref, o_ref):
    tm, d = x_ref.shape
    _, _, gate = _split_mod(mod_ref, d)
    y = jnp.concatenate([y_ref[j] for j in range(y_ref.shape[0])], axis=1) if pair_major else y_ref[...]
    o = _dot(y, wout_ref[...].astype(BF16))
    o_ref[...] = _resid_layer_norm(x_ref[...], gate, o, g_ref[...], b_ref[...])


def _na_out(x2d, gated, seq_len, per_sample, mods, layer, w_out, ln_g, ln_b):
    t, d = x2d.shape
    tm = ROW_TILE
    tps = max(seq_len // tm, 1)
    e = w_out.shape[0]
    pair_major = gated.ndim == 3
    if pair_major:
        gated_spec = pl.BlockSpec((e // LANES, tm, LANES), lambda i: (0, i, 0))
    else:
        gated_spec = pl.BlockSpec((tm, e), lambda i: (i, 0))
    return pl.pallas_call(
        functools.partial(_na_out_kernel, pair_major),
        grid=(t // tm,),
        in_specs=[
            pl.BlockSpec((tm, d), lambda i: (i, 0)),
            gated_spec,
            _mod_spec(layer, d, tps, per_sample),
            _const_spec(w_out.shape),
            _const_spec((1, d)),
            _const_spec((1, d)),
        ],
        out_specs=pl.BlockSpec((tm, d), lambda i: (i, 0)),
        out_shape=jax.ShapeDtypeStruct((t, d), F32),
        compiler_params=_params(1),
        name="na_out",
    )(x2d, gated, mods, w_out, ln_g.reshape(1, d), ln_b.reshape(1, d))


def kernel(x_prompt, x_sample, c, cache_k, cache_v, c_ctx, w_mod, b_mod, ln_g, ln_b, pool_w_in, pool_w_grp,
           pool_scale, pool_w_out, sgu_w_in, sgu_ln_g, sgu_ln_b, sgu_w_s, sgu_b_s, sgu_w_out, na_w_in, na_rpb,
           na_w_out):
    n_p, seq_p, d = x_prompt.shape
    n_s, seq_s, _ = x_sample.shape
    assert n_s + 1 <= N_COND_ROWS and (n_p * seq_p) % ROW_TILE == 0 and seq_s % ROW_TILE == 0
    assert seq_p % CHUNK == 0 and (seq_p % ROW_TILE == 0 or ROW_TILE % seq_p == 0)
    assert seq_s % (GRID_W * ATT_Q_ROWS) == 0 and seq_s // GRID_W >= ATT_K_ROWS
    conds = jnp.zeros((N_COND_ROWS, d), F32).at[0].set(c_ctx).at[1:1 + n_s].set(c)
    mods = _mods(conds, w_mod, b_mod)
    pool_w_fold = _pool_fold(pool_w_in, pool_w_grp)
    pool_band = _pool_band()

    yp = x_prompt.reshape(n_p * seq_p, d)
    ys = x_sample.reshape(n_s * seq_s, d)
    streams = ((seq_p, False), (seq_s, True))
    ctx_k = ctx_v = None
    for i in range(DEPTH):
        kind, j = i % N_MIXERS, i // N_MIXERS
        ys_in = (yp, ys)
        outs = []
        if kind == 0:
            for x2d, (seq, per_sample) in zip(ys_in, streams):
                outs.append(_pool_layer(x2d, seq, per_sample, mods, i, j, pool_band, pool_w_fold, pool_w_in,
                                        pool_scale, pool_w_out, ln_g[i], ln_b[i]))
        elif kind == 1:
            w_in, w_s, w_out = sgu_w_in[j].astype(BF16), sgu_w_s[j].astype(BF16), sgu_w_out[j].astype(BF16)
            for x2d, (seq, per_sample) in zip(ys_in, streams):
                outs.append(_sgu_layer(x2d, seq, per_sample, mods, i, w_in, sgu_ln_g[j], sgu_ln_b[j], w_s,
                                       sgu_b_s[j].T, w_out, ln_g[i], ln_b[i]))
        else:
            w_in, w_out = na_w_in[j], na_w_out[j]
            q, k, v, sg, kf, vf = _na_proj(yp, seq_p, False, mods, i, w_in, True, False, DH_C ** -0.5)
            ctx_k = kf.reshape(n_p, 1, H_C, DH_C, seq_p).transpose(0, 1, 4, 2, 3)
            ctx_v = vf.reshape(n_p, 1, H_C, DH_C, seq_p).transpose(0, 1, 4, 2, 3)
            gated = _ctx_attn(q, k, v, sg, seq_p)
            outs.append(_na_out(yp, gated, seq_p, False, mods, i, w_out, ln_g[i], ln_b[i]))
            q, k, v, sg = _na_proj(ys, seq_s, True, mods, i, w_in, False, True, DH_C ** -0.5 * LOG2_E)
            gated = _nbr_attn(q, k, v, sg, cache_k, cache_v, j, na_rpb[j], n_s)
            outs.append(_na_out(ys, gated, seq_s, True, mods, i, w_out, ln_g[i], ln_b[i]))
        yp, ys = outs
    return (yp.reshape(n_p, seq_p, d), ys.reshape(n_s, seq_s, d), ctx_k, ctx_v)
```

```python
import functools

import jax
import jax.numpy as jnp
import numpy as np
from jax import lax
from jax.experimental import pallas as pl
from jax.experimental.pallas import tpu as pltpu

F32 = jnp.float32
BF16 = jnp.bfloat16

DEPTH = 4
N_MIXERS = 3
POOL_WINDOWS = (2, 4, 8, 16)
POOL_HALO = 16
POOL_BLOCK = 128
CHUNK = 128
H_B = 8
H_C = 16
DH_C = 64
GRID_W = 64
WIN_R = 8
WIN_C = 16
DEEPNORM_ALPHA = (2 * DEPTH) ** 0.25
LN_EPS = 1e-5
LOG2_E = float(np.log2(np.e))

N_COND_ROWS = 8
ROW_TILE = 512
ATT_Q_ROWS = 4
ATT_K_ROWS = 12
LANES = 128
MXU_TILE = 256
VMEM_LIMIT = 56 * 1024 * 1024


def _const_spec(shape):
    nd = len(shape)
    return pl.BlockSpec(shape, lambda *_: (0,) * nd, pipeline_mode=pl.Buffered(1))


def _params(n_axes):
    return pltpu.CompilerParams(dimension_semantics=("arbitrary",) * n_axes, vmem_limit_bytes=VMEM_LIMIT)


def _mod_spec(layer, d, tiles_per_seq, per_sample):
    base = layer * N_COND_ROWS
    if per_sample:
        return pl.BlockSpec((1, 1, 3 * d), lambda i: (base + 1 + i // tiles_per_seq, 0, 0))
    return pl.BlockSpec((1, 1, 3 * d), lambda i: (base, 0, 0))


def _split_mod(mod_ref, d):
    m = mod_ref[0]
    return m[:, :d], m[:, d:2 * d], m[:, 2 * d:]


def _silu(x):
    return x / (1.0 + jnp.exp(-x))


def _gelu_tanh_x2(x):
    c = np.float32(np.sqrt(2.0 / np.pi))
    return x * (1.0 + jnp.tanh(x * (c + (c * np.float32(0.044715)) * (x * x))))


def _layer_norm(x, g, b, eps=LN_EPS):
    mu = jnp.mean(x, axis=-1, keepdims=True)
    d = x - mu
    var = jnp.mean(d * d, axis=-1, keepdims=True)
    return d * lax.rsqrt(var + eps) * g + b


def _resid_layer_norm(x, gate, o, g, b):
    return _layer_norm(x + (gate * (1.0 / DEEPNORM_ALPHA)) * o, g, b, LN_EPS / DEEPNORM_ALPHA ** 2)


def _dot(a, b):
    return jnp.dot(a, b, preferred_element_type=F32)


def _dot_nt(a, b):
    return lax.dot_general(a, b, (((1,), (1,)), ((), ())), preferred_element_type=F32)


def _mods_kernel(cond_ref, w_ref, b_ref, o_ref):
    a = _silu(cond_ref[...]).astype(BF16)
    o_ref[0] = _dot(a, w_ref[0].astype(BF16)) + b_ref[0]


def _mods(conds, w_mod, b_mod):
    depth, d, n = w_mod.shape
    tn = n
    out = pl.pallas_call(
        _mods_kernel,
        grid=(depth, n // tn),
        in_specs=[
            pl.BlockSpec((N_COND_ROWS, d), lambda l, j: (0, 0)),
            pl.BlockSpec((1, d, tn), lambda l, j: (l, 0, j)),
            pl.BlockSpec((1, 1, tn), lambda l, j: (l, 0, j)),
        ],
        out_specs=pl.BlockSpec((1, N_COND_ROWS, tn), lambda l, j: (l, 0, j)),
        out_shape=jax.ShapeDtypeStruct((depth, N_COND_ROWS, n), F32),
        compiler_params=_params(2),
        name="adaln_mods",
    )(conds, w_mod, b_mod.reshape(depth, 1, n))
    return out.reshape(depth * N_COND_ROWS, 1, n)


def _pool_fold_kernel(win_x_ref, wgrp_ref, fold_ref):
    fold_ref[0] = _dot(win_x_ref[0].astype(BF16), wgrp_ref[0, 0].astype(BF16)).astype(BF16)


def _pool_fold(w_in, w_grp):
    n_layers, d, two_e = w_in.shape
    n_grp, grp = w_grp.shape[1:3]
    e_dim = two_e // 2
    return pl.pallas_call(
        _pool_fold_kernel,
        grid=(n_layers, n_grp),
        in_specs=[
            pl.BlockSpec((1, d, grp), lambda l, g: (l, 0, g)),
            pl.BlockSpec((1, 1, grp, grp), lambda l, g: (l, g, 0, 0)),
        ],
        out_specs=pl.BlockSpec((1, d, grp), lambda l, g: (l, 0, g)),
        out_shape=jax.ShapeDtypeStruct((n_layers, d, e_dim), BF16),
        compiler_params=_params(2),
        name="pool_fold",
    )(w_in, w_grp)


def _pool_band():
    t = np.arange(POOL_BLOCK)[:, None] + POOL_HALO
    e = np.arange(POOL_BLOCK + 2 * POOL_HALO)[None, :]
    return jnp.asarray(np.stack([(e >= t - w // 2) & (e <= t + w // 2 - 1) for w in POOL_WINDOWS]), BF16)


def _pool_kernel(seq_len, x_ref, xp_ref, xn_ref, mod_ref, band_ref, wfold_ref, wgate_ref, psc_ref, wout_ref,
                 g_ref, b_ref, o_ref, e_ref, u_ref):
    tm, d = x_ref.shape
    e_dim = psc_ref.shape[1]
    grp = e_dim // len(POOL_WINDOWS)
    seg = min(tm, seq_len)
    stride = seg + 2 * POOL_HALO
    shift, scale, gate = _split_mod(mod_ref, d)
    x = x_ref[...]
    hb = (x * (1.0 + scale) + shift).astype(BF16)
    zeros = jnp.zeros((POOL_HALO, d), BF16)
    pos = (pl.program_id(0) * tm + lax.broadcasted_iota(jnp.int32, (tm, 1), 0)) % seq_len
    if seg == tm:
        first_pos = (pl.program_id(0) * tm) % seq_len
        hp = (xp_ref[...] * (1.0 + scale) + shift).astype(BF16)
        hn = (xn_ref[...] * (1.0 + scale) + shift).astype(BF16)
        halos = [(jnp.where(first_pos != 0, hp, zeros), jnp.where(first_pos + tm != seq_len, hn, zeros))]
    else:
        halos = [(zeros, zeros)] * (tm // seg)
    for sg, (before, after) in enumerate(halos):
        e_ref[sg * stride:sg * stride + POOL_HALO] = before
        e_ref[sg * stride + POOL_HALO:sg * stride + POOL_HALO + seg] = hb[sg * seg:(sg + 1) * seg]
        e_ref[sg * stride + POOL_HALO + seg:(sg + 1) * stride] = after
    u_ref[...] = _dot(e_ref[...], wfold_ref[...]).astype(BF16)

    mixed = []
    for gi, w in enumerate(POOL_WINDOWS):
        cols = slice(gi * grp, (gi + 1) * grp)
        lo = jnp.maximum(pos - w // 2, 0)
        hi = jnp.minimum(pos + w // 2 - 1, seq_len - 1)
        inv_cnt = 1.0 / (hi - lo + 1).astype(F32)
        sums, own = [], []
        for sg in range(len(halos)):
            for rb in range(seg // POOL_BLOCK):
                r0 = sg * stride + rb * POOL_BLOCK
                sums.append(_dot(band_ref[gi], u_ref[r0:r0 + POOL_BLOCK + 2 * POOL_HALO, cols]))
            own.append(u_ref[sg * stride + POOL_HALO:sg * stride + POOL_HALO + seg, cols])
        sums = jnp.concatenate(sums, axis=0)
        own = own[0] if len(own) == 1 else jnp.concatenate(own, axis=0)
        mixed.append(sums * inv_cnt - own.astype(F32))
    mixed = jnp.concatenate(mixed, axis=1)
    gate_pre = _dot(hb, wgate_ref[...].astype(BF16))
    y = (mixed * psc_ref[...] * _silu(gate_pre)).astype(BF16)
    o = _dot(y, wout_ref[...].astype(BF16))
    o_ref[...] = _resid_layer_norm(x, gate, o, g_ref[...], b_ref[...])


def _pool_layer(x2d, seq_len, per_sample, mods, layer, j, band, w_fold, w_in, p_scale, w_out, ln_g, ln_b):
    t, d = x2d.shape
    tm = ROW_TILE
    tps = max(seq_len // tm, 1)
    e_dim = w_out.shape[1]
    hb = tm // POOL_HALO
    last = t // POOL_HALO - 1
    n_seg = max(tm // seq_len, 1)
    e_rows = tm + 2 * POOL_HALO * n_seg

    def layer_spec(shape):
        return pl.BlockSpec((None,) + shape, lambda i: (j,) + (0,) * len(shape), pipeline_mode=pl.Buffered(1))

    return pl.pallas_call(
        functools.partial(_pool_kernel, seq_len),
        grid=(t // tm,),
        in_specs=[
            pl.BlockSpec((tm, d), lambda i: (i, 0)),
            pl.BlockSpec((POOL_HALO, d), lambda i: (jnp.maximum(i * hb - 1, 0), 0)),
            pl.BlockSpec((POOL_HALO, d), lambda i: (jnp.minimum((i + 1) * hb, last), 0)),
            _mod_spec(layer, d, tps, per_sample),
            _const_spec(band.shape),
            layer_spec((d, e_dim)),
            pl.BlockSpec((None, d, e_dim), lambda i: (j, 0, 1), pipeline_mode=pl.Buffered(1)),
            layer_spec((1, e_dim)),
            layer_spec((e_dim, d)),
            _const_spec((1, d)),
            _const_spec((1, d)),
        ],
        out_specs=pl.BlockSpec((tm, d), lambda i: (i, 0)),
        out_shape=jax.ShapeDtypeStruct((t, d), F32),
        scratch_shapes=[pltpu.VMEM((e_rows, d), BF16), pltpu.VMEM((e_rows, e_dim), BF16)],
        compiler_params=_params(1),
        name="pool_layer",
    )(x2d, x2d, x2d, mods, band, w_fold, w_in, p_scale.reshape(-1, 1, e_dim), w_out,
      ln_g.reshape(1, d), ln_b.reshape(1, d))


def _sgu_kernel(x_ref, mod_ref, win_ref, lg_ref, lb_ref, ws_ref, bs_ref, wout_ref, g_ref, b_ref, o_ref, y_ref):
    tm, d = x_ref.shape
    e_dim = lg_ref.shape[1]
    dh = e_dim // H_B
    shift, scale, gate = _split_mod(mod_ref, d)
    x = x_ref[...]
    hb = (x * (1.0 + scale) + shift).astype(BF16)
    v2 = _gelu_tanh_x2(_dot(hb, win_ref[:, e_dim:2 * e_dim]))
    v = _layer_norm(v2, lg_ref[...], lb_ref[...], 4.0 * LN_EPS).astype(BF16)
    u2 = _gelu_tanh_x2(_dot(hb, win_ref[:, 0:e_dim]))
    ug = u2 * _silu(_dot(hb, win_ref[:, 2 * e_dim:3 * e_dim]))
    for c in range(tm // CHUNK):
        rows = slice(c * CHUNK, (c + 1) * CHUNK)
        for hh in range(H_B):
            cols = slice(hh * dh, (hh + 1) * dh)
            sv = _dot(ws_ref[hh], v[rows, cols]) + bs_ref[:, hh:hh + 1]
            y_ref[rows, cols] = (ug[rows, cols] * sv).astype(BF16)
    o2 = _dot(y_ref[...], wout_ref[...])
    o_ref[...] = _resid_layer_norm(x, 0.5 * gate, o2, g_ref[...], b_ref[...])


def _sgu_layer(x2d, seq_len, per_sample, mods, layer, w_in, sln_g, sln_b, w_s, b_s_t, w_out, ln_g, ln_b):
    t, d = x2d.shape
    tm = ROW_TILE
    tps = max(seq_len // tm, 1)
    e_dim = w_out.shape[0]
    return pl.pallas_call(
        _sgu_kernel,
        grid=(t // tm,),
        in_specs=[
            pl.BlockSpec((tm, d), lambda i: (i, 0)),
            _mod_spec(layer, d, tps, per_sample),
            _const_spec(w_in.shape),
            _const_spec((1, e_dim)),
            _const_spec((1, e_dim)),
            _const_spec(w_s.shape),
            _const_spec(b_s_t.shape),
            _const_spec(w_out.shape),
            _const_spec((1, d)),
            _const_spec((1, d)),
        ],
        out_specs=pl.BlockSpec((tm, d), lambda i: (i, 0)),
        out_shape=jax.ShapeDtypeStruct((t, d), F32),
        scratch_shapes=[pltpu.VMEM((tm, e_dim), BF16)],
        compiler_params=_params(1),
        name="sgu_layer",
    )(x2d, mods, w_in, sln_g.reshape(1, e_dim), sln_b.reshape(1, e_dim), w_s, b_s_t, w_out,
      ln_g.reshape(1, d), ln_b.reshape(1, d))


def _na_proj_kernel(emit_f32_kv, q_scale, x_ref, mod_ref, win_ref, *out_refs):
    tm, d = x_ref.shape
    e = win_ref.shape[1] // 4
    shift, scale, _ = _split_mod(mod_ref, d)
    hb = (x_ref[...] * (1.0 + scale) + shift).astype(BF16)
    q_ref, k_ref, v_ref, sg_ref = out_refs[:4]
    q_ref[...] = (_dot(hb, win_ref[:, 0:e].astype(BF16)) * q_scale).astype(BF16)
    k = _dot(hb, win_ref[:, e:2 * e].astype(BF16))
    v = _dot(hb, win_ref[:, 2 * e:3 * e].astype(BF16))
    k_ref[...] = k.astype(BF16)
    v_ref[...] = v.astype(BF16)
    sg_ref[...] = _silu(_dot(hb, win_ref[:, 3 * e:4 * e].astype(BF16))).astype(BF16)
    if emit_f32_kv:
        seq = out_refs[4].shape[2]
        for kv, ref in ((k, out_refs[4]), (v, out_refs[5])):
            for sq in range(tm // seq):
                ref[sq] = kv[sq * seq:(sq + 1) * seq, :].T


def _na_proj(x2d, seq_len, per_sample, mods, layer, w_in, emit_f32_kv, q_scale):
    t, d = x2d.shape
    tm = ROW_TILE
    tps = max(seq_len // tm, 1)
    e = w_in.shape[1] // 4
    row_spec = pl.BlockSpec((tm, e), lambda i: (i, 0))
    out_specs = [row_spec] * 4
    out_shape = [jax.ShapeDtypeStruct((t, e), BF16)] * 4
    if emit_f32_kv:
        out_specs += [pl.BlockSpec((tm // seq_len, e, seq_len), lambda i: (i, 0, 0))] * 2
        out_shape += [jax.ShapeDtypeStruct((t // seq_len, e, seq_len), F32)] * 2
    return pl.pallas_call(
        functools.partial(_na_proj_kernel, emit_f32_kv, q_scale),
        grid=(t // tm,),
        in_specs=[
            pl.BlockSpec((tm, d), lambda i: (i, 0)),
            _mod_spec(layer, d, tps, per_sample),
            _const_spec(w_in.shape),
        ],
        out_specs=out_specs,
        out_shape=out_shape,
        compiler_params=_params(1),
        name="na_proj",
    )(x2d, mods, w_in)


def _lane_is_first_head():
    return lax.broadcasted_iota(jnp.int32, (1, LANES), 1) < DH_C


def _ctx_attn_kernel(q_ref, k_ref, v_ref, sg_ref, o_ref):
    first = _lane_is_first_head()
    for j in range(q_ref.shape[1] // LANES):
        cols = slice(j * LANES, (j + 1) * LANES)
        q, k, v = q_ref[:, cols], k_ref[:, cols], v_ref[:, cols]
        outs = []
        for sel in (first, jnp.logical_not(first)):
            s = _dot_nt(jnp.where(sel, q, jnp.zeros_like(q)), k)
            p = jnp.exp(s - jnp.max(s, axis=-1, keepdims=True))
            l = jnp.sum(p, axis=-1, keepdims=True)
            outs.append(_dot(p.astype(BF16), v) / l)
        o_ref[:, cols] = (jnp.where(first, outs[0], outs[1]) * sg_ref[:, cols].astype(F32)).astype(BF16)


def _ctx_attn(q, k, v, sg, seq_len):
    t, e = q.shape
    spec = pl.BlockSpec((seq_len, e), lambda b: (b, 0))
    return pl.pallas_call(
        _ctx_attn_kernel,
        grid=(t // seq_len,),
        in_specs=[spec, spec, spec, spec],
        out_specs=spec,
        out_shape=jax.ShapeDtypeStruct((t, e), BF16),
        compiler_params=_params(1),
        name="ctx_attn",
    )(q, k, v, sg)


def _key_row_start(r_blk, rows):
    return jnp.clip(r_blk * ATT_Q_ROWS - WIN_R // 2, 0, rows - ATT_K_ROWS)


def _nbr_bias_build(rows, rpb_ref, bias_ref):
    n_blk = rows // ATT_Q_ROWS
    kc = lax.broadcasted_iota(jnp.int32, (GRID_W, LANES), 0)
    lane = lax.broadcasted_iota(jnp.int32, (GRID_W, LANES), 1)
    qc = lane & (GRID_W - 1)
    c0 = jnp.clip(qc - WIN_C // 2, 0, GRID_W - WIN_C)
    col_in = (kc >= c0) & (kc < c0 + WIN_C)
    first = lane < GRID_W
    neg = jnp.full((GRID_W, LANES), -jnp.inf, F32)
    for hh in range(2):
        lo, hi = [], []
        for dr in range(2 * WIN_R - 1):
            row = jnp.broadcast_to(rpb_ref[hh, dr:dr + 1, :] * LOG2_E, (GRID_W, LANES))
            lo.append(pltpu.roll(row, LANES - (WIN_C - 1), 1, stride=1, stride_axis=0))
            hi.append(pltpu.roll(row, GRID_W - (WIN_C - 1), 1, stride=1, stride_axis=0))
        for var, r_blk in enumerate((0, 1, n_blk - 1)):
            ks = min(max(r_blk * ATT_Q_ROWS - WIN_R // 2, 0), rows - ATT_K_ROWS)
            for ki in range(ATT_K_ROWS):
                kr = ks + ki
                for qp in range(ATT_Q_ROWS // 2):
                    halves = []
                    for half, src in enumerate((lo, hi)):
                        r = r_blk * ATT_Q_ROWS + 2 * qp + half
                        r0 = min(max(r - WIN_R // 2, 0), rows - WIN_R)
                        halves.append(src[kr - r + WIN_R - 1] if r0 <= kr < r0 + WIN_R else neg)
                    blk = jnp.where(col_in, jnp.where(first, halves[0], halves[1]), neg)
                    bias_ref[hh, var, ki * GRID_W:(ki + 1) * GRID_W, qp * LANES:(qp + 1) * LANES] = blk


def _nbr_attn_kernel(rows, rpb_ref, q_ref, k_ref, v_ref, sg_ref, ck_ref, cv_ref, o_ref, bias_ref, kc_ref, vx_ref,
                     s_ref, p_ref, oa_ref):
    n_blk = rows // ATT_Q_ROWS
    tq = ATT_Q_ROWS * GRID_W
    nk = ATT_K_ROWS * GRID_W
    seq = q_ref.shape[0]
    n_tok_tiles = seq // MXU_TILE
    n_ctx_tiles = kc_ref.shape[0] // MXU_TILE

    @pl.when(pl.program_id(1) == 0)
    def _():
        _nbr_bias_build(rows, rpb_ref, bias_ref)

    first = _lane_is_first_head()
    head_lanes = (first, jnp.logical_not(first))
    kc_ref[...] = ck_ref[...].T.astype(BF16)
    v_t = v_ref[...].astype(F32).T
    top = lax.broadcasted_iota(jnp.int32, (LANES, 1), 0) < DH_C
    for hh, sel in enumerate((top, jnp.logical_not(top))):
        for kt in range(n_tok_tiles):
            vx_ref[hh, kt] = jnp.where(sel, v_t[:, kt * MXU_TILE:(kt + 1) * MXU_TILE], 1.0).astype(BF16)
        for kt in range(n_ctx_tiles):
            vx_ref[hh, n_tok_tiles + kt] = jnp.where(
                sel, cv_ref[:, kt * MXU_TILE:(kt + 1) * MXU_TILE], 1.0).astype(BF16)

    def q_rows(r):
        return pl.ds(pl.multiple_of(r * tq, tq), tq)

    def k_start(r):
        return pl.multiple_of(_key_row_start(r, rows) * GRID_W, MXU_TILE)

    def logits(r, hh):
        q = q_ref[q_rows(r), :]
        qh = jnp.where(head_lanes[hh], q, jnp.zeros_like(q))
        var = jnp.where(r == 0, 0, jnp.where(r == n_blk - 1, 2, 1))
        s_ref[hh, 0:nk, :] = _dot_nt(k_ref[pl.ds(k_start(r), nk), :], qh) + bias_ref[hh, var]
        s_ref[hh, nk:, :] = _dot_nt(kc_ref[...], qh)

    def softmax(hh):
        s = s_ref[hh]
        p_ref[hh] = jnp.exp2((s - jnp.max(s, axis=0, keepdims=True)).astype(BF16))

    def weighted_values(r, hh):
        kt0 = k_start(r) // MXU_TILE
        tiles = [vx_ref[hh, kt0 + i] for i in range(nk // MXU_TILE)]
        tiles += [vx_ref[hh, n_tok_tiles + i] for i in range(n_ctx_tiles)]
        return _dot(jnp.concatenate(tiles, axis=1), p_ref[hh])

    def emit(r, o_second):
        o_first = oa_ref[...]
        num = jnp.concatenate([o_first[0:DH_C], o_second[DH_C:]], axis=0)
        den = jnp.concatenate([o_first[DH_C:], o_second[0:DH_C]], axis=0)
        o_ref[q_rows(r), :] = ((num / den).T * sg_ref[q_rows(r), :].astype(F32)).astype(BF16)

    p_ref[1] = jnp.ones(p_ref.shape[1:], BF16)
    oa_ref[...] = jnp.ones(oa_ref.shape, F32)
    logits(0, 0)

    def block(r, carry):
        prev = jnp.maximum(r - 1, 0)
        emit(prev, weighted_values(prev, 1))
        logits(r, 1)
        softmax(0)
        oa_ref[...] = weighted_values(r, 0)
        logits(jnp.minimum(r + 1, n_blk - 1), 0)
        softmax(1)
        return carry

    lax.fori_loop(0, n_blk, block, 0, unroll=2)
    emit(n_blk - 1, weighted_values(n_blk - 1, 1))


def _nbr_attn(q, k, v, sg, cache_k, cache_v, cache_layer, rpb, n_batch):
    t, e = q.shape
    seq = t // n_batch
    rows = seq // GRID_W
    n_cached, past = cache_k.shape[1:3]
    ck = jnp.transpose(cache_k, (0, 1, 3, 4, 2)).reshape(n_batch * n_cached, e, past)
    cv = jnp.transpose(cache_v, (0, 1, 3, 4, 2)).reshape(n_batch * n_cached, e, past)
    tq, nk = ATT_Q_ROWS * GRID_W, ATT_K_ROWS * GRID_W
    assert nk % MXU_TILE == 0 and past % MXU_TILE == 0 and (ATT_Q_ROWS * GRID_W) % MXU_TILE == 0
    n_dr, n_dc = rpb.shape[1:]
    rpb_pad = jnp.pad(rpb.astype(F32)[:, :, ::-1], ((0, 0), (0, 2 * WIN_R - n_dr), (0, LANES - n_dc)))
    tok_spec = pl.BlockSpec((seq, LANES), lambda j, b: (b, j))
    ctx_spec = pl.BlockSpec((None, LANES, past), lambda j, b: (b * n_cached + cache_layer, j, 0))
    return pl.pallas_call(
        functools.partial(_nbr_attn_kernel, rows),
        grid=(e // LANES, n_batch),
        in_specs=[pl.BlockSpec((2, 2 * WIN_R, LANES), lambda j, b: (j, 0, 0)),
                  tok_spec, tok_spec, tok_spec, tok_spec, ctx_spec, ctx_spec],
        out_specs=tok_spec,
        out_shape=jax.ShapeDtypeStruct((t, e), BF16),
        scratch_shapes=[
            pltpu.VMEM((2, 3, nk, tq), F32),
            pltpu.VMEM((past, LANES), BF16),
            pltpu.VMEM((2, (seq + past) // MXU_TILE, LANES, MXU_TILE), BF16),
            pltpu.VMEM((2, nk + past, tq), F32),
            pltpu.VMEM((2, nk + past, tq), BF16),
            pltpu.VMEM((LANES, tq), F32),
        ],
        compiler_params=_params(2),
        name="nbr_attn",
    )(rpb_pad, q, k, v, sg, ck, cv)


def _na_out_kernel(x_ref, y_ref, mod_ref, wout_ref, g_ref, b_ref, o_ref):
    tm, d = x_ref.shape
    _, _, gate = _split_mod(mod_ref, d)
    o = _dot(y_ref[...], wout_ref[...].astype(BF16))
    o_ref[...] = _resid_layer_norm(x_ref[...], gate, o, g_ref[...], b_ref[...])


def _na_out(x2d, gated, seq_len, per_sample, mods, layer, w_out, ln_g, ln_b):
    t, d = x2d.shape
    tm = ROW_TILE
    tps = max(seq_len // tm, 1)
    e = w_out.shape[0]
    return pl.pallas_call(
        _na_out_kernel,
        grid=(t // tm,),
        in_specs=[
            pl.BlockSpec((tm, d), lambda i: (i, 0)),
            pl.BlockSpec((tm, e), lambda i: (i, 0)),
            _mod_spec(layer, d, tps, per_sample),
            _const_spec(w_out.shape),
            _const_spec((1, d)),
            _const_spec((1, d)),
        ],
        out_specs=pl.BlockSpec((tm, d), lambda i: (i, 0)),
        out_shape=jax.ShapeDtypeStruct((t, d), F32),
        compiler_params=_params(1),
        name="na_out",
    )(x2d, gated, mods, w_out, ln_g.reshape(1, d), ln_b.reshape(1, d))


def kernel(x_prompt, x_sample, c, cache_k, cache_v, c_ctx, w_mod, b_mod, ln_g, ln_b, pool_w_in, pool_w_grp,
           pool_scale, pool_w_out, sgu_w_in, sgu_ln_g, sgu_ln_b, sgu_w_s, sgu_b_s, sgu_w_out, na_w_in, na_rpb,
           na_w_out):
    n_p, seq_p, d = x_prompt.shape
    n_s, seq_s, _ = x_sample.shape
    assert n_s + 1 <= N_COND_ROWS and (n_p * seq_p) % ROW_TILE == 0 and seq_s % ROW_TILE == 0
    assert seq_p % CHUNK == 0 and (seq_p % ROW_TILE == 0 or ROW_TILE % seq_p == 0) and seq_p % POOL_BLOCK == 0
    assert seq_s % (GRID_W * ATT_Q_ROWS) == 0 and seq_s // GRID_W >= ATT_K_ROWS
    conds = jnp.zeros((N_COND_ROWS, d), F32).at[0].set(c_ctx).at[1:1 + n_s].set(c)
    mods = _mods(conds, w_mod, b_mod)
    pool_w_fold = _pool_fold(pool_w_in, pool_w_grp)
    pool_band = _pool_band()

    yp = x_prompt.reshape(n_p * seq_p, d)
    ys = x_sample.reshape(n_s * seq_s, d)
    streams = ((seq_p, False), (seq_s, True))
    ctx_k = ctx_v = None
    for i in range(DEPTH):
        kind, j = i % N_MIXERS, i // N_MIXERS
        ys_in = (yp, ys)
        outs = []
        if kind == 0:
            for x2d, (seq, per_sample) in zip(ys_in, streams):
                outs.append(_pool_layer(x2d, seq, per_sample, mods, i, j, pool_band, pool_w_fold, pool_w_in,
                                        pool_scale, pool_w_out, ln_g[i], ln_b[i]))
        elif kind == 1:
            w_in, w_s, w_out = sgu_w_in[j].astype(BF16), sgu_w_s[j].astype(BF16), sgu_w_out[j].astype(BF16)
            for x2d, (seq, per_sample) in zip(ys_in, streams):
                outs.append(_sgu_layer(x2d, seq, per_sample, mods, i, w_in, sgu_ln_g[j], sgu_ln_b[j], w_s,
                                       sgu_b_s[j].T, w_out, ln_g[i], ln_b[i]))
        else:
            w_in, w_out = na_w_in[j], na_w_out[j]
            q, k, v, sg, kf, vf = _na_proj(yp, seq_p, False, mods, i, w_in, True, DH_C ** -0.5)
            ctx_k = kf.reshape(n_p, 1, H_C, DH_C, seq_p).transpose(0, 1, 4, 2, 3)
            ctx_v = vf.reshape(n_p, 1, H_C, DH_C, seq_p).transpose(0, 1, 4, 2, 3)
            gated = _ctx_attn(q, k, v, sg, seq_p)
            outs.append(_na_out(yp, gated, seq_p, False, mods, i, w_out, ln_g[i], ln_b[i]))
            q, k, v, sg = _na_proj(ys, seq_s, True, mods, i, w_in, False, DH_C ** -0.5 * LOG2_E)
            gated = _nbr_attn(q, k, v, sg, cache_k, cache_v, j, na_rpb[j], n_s)
            outs.append(_na_out(ys, gated, seq_s, True, mods, i, w_out, ln_g[i], ln_b[i]))
        yp, ys = outs
    return (yp.reshape(n_p, seq_p, d), ys.reshape(n_s, seq_s, d), ctx_k, ctx_v)
```

```python
import functools

import jax
import jax.numpy as jnp
import numpy as np
from jax import lax
from jax.experimental import pallas as pl
from jax.experimental.pallas import tpu as pltpu

F32 = jnp.float32
BF16 = jnp.bfloat16

DEPTH = 4
N_MIXERS = 3
POOL_WINDOWS = (2, 4, 8, 16)
POOL_HALO = 16
POOL_BLOCK = 128
CHUNK = 128
H_B = 8
H_C = 16
DH_C = 64
GRID_W = 64
WIN_R = 8
WIN_C = 16
DEEPNORM_ALPHA = (2 * DEPTH) ** 0.25
LN_EPS = 1e-5
LOG2_E = float(np.log2(np.e))

N_COND_ROWS = 8
ROW_TILE = 512
ATT_Q_ROWS = 4
ATT_K_ROWS = 12
LANES = 128
MXU_TILE = 256
VMEM_LIMIT = 56 * 1024 * 1024


def _const_spec(shape):
    nd = len(shape)
    return pl.BlockSpec(shape, lambda *_: (0,) * nd, pipeline_mode=pl.Buffered(1))


def _params(n_axes):
    return pltpu.CompilerParams(dimension_semantics=("arbitrary",) * n_axes, vmem_limit_bytes=VMEM_LIMIT)


def _mod_spec(layer, d, tiles_per_seq, per_sample):
    base = layer * N_COND_ROWS
    if per_sample:
        return pl.BlockSpec((1, 1, 3 * d), lambda i: (base + 1 + i // tiles_per_seq, 0, 0))
    return pl.BlockSpec((1, 1, 3 * d), lambda i: (base, 0, 0))


def _split_mod(mod_ref, d):
    m = mod_ref[0]
    return m[:, :d], m[:, d:2 * d], m[:, 2 * d:]


def _silu(x):
    return x / (1.0 + jnp.exp(-x))


def _gelu_tanh_x2(x):
    c = np.float32(np.sqrt(2.0 / np.pi))
    return x * (1.0 + jnp.tanh(x * (c + (c * np.float32(0.044715)) * (x * x))))


def _layer_norm(x, g, b, eps=LN_EPS):
    mu = jnp.mean(x, axis=-1, keepdims=True)
    d = x - mu
    var = jnp.mean(d * d, axis=-1, keepdims=True)
    return d * lax.rsqrt(var + eps) * g + b


def _resid_layer_norm(x, gate, o, g, b):
    return _layer_norm(x + (gate * (1.0 / DEEPNORM_ALPHA)) * o, g, b, LN_EPS / DEEPNORM_ALPHA ** 2)


def _dot(a, b):
    return jnp.dot(a, b, preferred_element_type=F32)


def _dot_nt(a, b):
    return lax.dot_general(a, b, (((1,), (1,)), ((), ())), preferred_element_type=F32)


def _mods_kernel(cond_ref, w_ref, b_ref, o_ref):
    a = _silu(cond_ref[...]).astype(BF16)
    o_ref[0] = _dot(a, w_ref[0].astype(BF16)) + b_ref[0]


def _mods(conds, w_mod, b_mod):
    depth, d, n = w_mod.shape
    tn = n
    out = pl.pallas_call(
        _mods_kernel,
        grid=(depth, n // tn),
        in_specs=[
            pl.BlockSpec((N_COND_ROWS, d), lambda l, j: (0, 0)),
            pl.BlockSpec((1, d, tn), lambda l, j: (l, 0, j)),
            pl.BlockSpec((1, 1, tn), lambda l, j: (l, 0, j)),
        ],
        out_specs=pl.BlockSpec((1, N_COND_ROWS, tn), lambda l, j: (l, 0, j)),
        out_shape=jax.ShapeDtypeStruct((depth, N_COND_ROWS, n), F32),
        compiler_params=_params(2),
        name="adaln_mods",
    )(conds, w_mod, b_mod.reshape(depth, 1, n))
    return out.reshape(depth * N_COND_ROWS, 1, n)


def _pool_fold_kernel(win_x_ref, wgrp_ref, fold_ref):
    fold_ref[0] = _dot(win_x_ref[0].astype(BF16), wgrp_ref[0, 0].astype(BF16)).astype(BF16)


def _pool_fold(w_in, w_grp):
    n_layers, d, two_e = w_in.shape
    n_grp, grp = w_grp.shape[1:3]
    e_dim = two_e // 2
    return pl.pallas_call(
        _pool_fold_kernel,
        grid=(n_layers, n_grp),
        in_specs=[
            pl.BlockSpec((1, d, grp), lambda l, g: (l, 0, g)),
            pl.BlockSpec((1, 1, grp, grp), lambda l, g: (l, g, 0, 0)),
        ],
        out_specs=pl.BlockSpec((1, d, grp), lambda l, g: (l, 0, g)),
        out_shape=jax.ShapeDtypeStruct((n_layers, d, e_dim), BF16),
        compiler_params=_params(2),
        name="pool_fold",
    )(w_in, w_grp)


def _pool_band():
    t = np.arange(POOL_BLOCK)[:, None] + POOL_HALO
    e = np.arange(POOL_BLOCK + 2 * POOL_HALO)[None, :]
    return jnp.asarray(np.stack([(e >= t - w // 2) & (e <= t + w // 2 - 1) for w in POOL_WINDOWS]), BF16)


def _pool_kernel(seq_len, x_ref, xp_ref, xn_ref, mod_ref, band_ref, wfold_ref, wgate_ref, psc_ref, wout_ref,
                 g_ref, b_ref, o_ref, e_ref, u_ref):
    tm, d = x_ref.shape
    e_dim = psc_ref.shape[1]
    grp = e_dim // len(POOL_WINDOWS)
    seg = min(tm, seq_len)
    stride = seg + 2 * POOL_HALO
    shift, scale, gate = _split_mod(mod_ref, d)
    x = x_ref[...]
    hb = (x * (1.0 + scale) + shift).astype(BF16)
    zeros = jnp.zeros((POOL_HALO, d), BF16)
    seg_pos = lax.broadcasted_iota(jnp.int32, (seg, 1), 0)
    if seg == tm:
        first_pos = (pl.program_id(0) * tm) % seq_len
        pos = first_pos + seg_pos
        hp = (xp_ref[...] * (1.0 + scale) + shift).astype(BF16)
        hn = (xn_ref[...] * (1.0 + scale) + shift).astype(BF16)
        halos = [(jnp.where(first_pos != 0, hp, zeros), jnp.where(first_pos + tm != seq_len, hn, zeros))]
    else:
        pos = jnp.concatenate([seg_pos] * (tm // seg), axis=0)
        halos = [(zeros, zeros)] * (tm // seg)
    for sg, (before, after) in enumerate(halos):
        e_ref[sg * stride:sg * stride + POOL_HALO] = before
        e_ref[sg * stride + POOL_HALO:sg * stride + POOL_HALO + seg] = hb[sg * seg:(sg + 1) * seg]
        e_ref[sg * stride + POOL_HALO + seg:(sg + 1) * stride] = after
    u_ref[...] = _dot(e_ref[...], wfold_ref[...]).astype(BF16)

    mixed = []
    for gi, w in enumerate(POOL_WINDOWS):
        cols = slice(gi * grp, (gi + 1) * grp)
        lo = jnp.maximum(pos - w // 2, 0)
        hi = jnp.minimum(pos + w // 2 - 1, seq_len - 1)
        inv_cnt = 1.0 / (hi - lo + 1).astype(F32)
        sums, own = [], []
        for sg in range(len(halos)):
            for rb in range(seg // POOL_BLOCK):
                r0 = sg * stride + rb * POOL_BLOCK
                sums.append(_dot(band_ref[gi], u_ref[r0:r0 + POOL_BLOCK + 2 * POOL_HALO, cols]))
            own.append(u_ref[sg * stride + POOL_HALO:sg * stride + POOL_HALO + seg, cols])
        sums = jnp.concatenate(sums, axis=0)
        own = own[0] if len(own) == 1 else jnp.concatenate(own, axis=0)
        mixed.append(sums * inv_cnt - own.astype(F32))
    mixed = jnp.concatenate(mixed, axis=1)
    gate_pre = _dot(hb, wgate_ref[...].astype(BF16))
    y = (mixed * psc_ref[...] * _silu(gate_pre)).astype(BF16)
    o = _dot(y, wout_ref[...].astype(BF16))
    o_ref[...] = _resid_layer_norm(x, gate, o, g_ref[...], b_ref[...])


def _pool_layer(x2d, seq_len, per_sample, mods, layer, j, band, w_fold, w_in, p_scale, w_out, ln_g, ln_b):
    t, d = x2d.shape
    tm = ROW_TILE
    tps = max(seq_len // tm, 1)
    e_dim = w_out.shape[1]
    hb = tm // POOL_HALO
    last = t // POOL_HALO - 1
    n_seg = max(tm // seq_len, 1)
    e_rows = tm + 2 * POOL_HALO * n_seg

    def layer_spec(shape):
        return pl.BlockSpec((None,) + shape, lambda i: (j,) + (0,) * len(shape), pipeline_mode=pl.Buffered(1))

    return pl.pallas_call(
        functools.partial(_pool_kernel, seq_len),
        grid=(t // tm,),
        in_specs=[
            pl.BlockSpec((tm, d), lambda i: (i, 0)),
            pl.BlockSpec((POOL_HALO, d), lambda i: (jnp.maximum(i * hb - 1, 0), 0)),
            pl.BlockSpec((POOL_HALO, d), lambda i: (jnp.minimum((i + 1) * hb, last), 0)),
            _mod_spec(layer, d, tps, per_sample),
            _const_spec(band.shape),
            layer_spec((d, e_dim)),
            pl.BlockSpec((None, d, e_dim), lambda i: (j, 0, 1), pipeline_mode=pl.Buffered(1)),
            layer_spec((1, e_dim)),
            layer_spec((e_dim, d)),
            _const_spec((1, d)),
            _const_spec((1, d)),
        ],
        out_specs=pl.BlockSpec((tm, d), lambda i: (i, 0)),
        out_shape=jax.ShapeDtypeStruct((t, d), F32),
        scratch_shapes=[pltpu.VMEM((e_rows, d), BF16), pltpu.VMEM((e_rows, e_dim), BF16)],
        compiler_params=_params(1),
        name="pool_layer",
    )(x2d, x2d, x2d, mods, band, w_fold, w_in, p_scale.reshape(-1, 1, e_dim), w_out,
      ln_g.reshape(1, d), ln_b.reshape(1, d))


def _sgu_kernel(x_ref, mod_ref, win_ref, lg_ref, lb_ref, ws_ref, bs_ref, wout_ref, g_ref, b_ref, o_ref, y_ref):
    tm, d = x_ref.shape
    e_dim = lg_ref.shape[1]
    dh = e_dim // H_B
    shift, scale, gate = _split_mod(mod_ref, d)
    x = x_ref[...]
    hb = (x * (1.0 + scale) + shift).astype(BF16)
    v2 = _gelu_tanh_x2(_dot(hb, win_ref[:, e_dim:2 * e_dim]))
    v = _layer_norm(v2, lg_ref[...], lb_ref[...], 4.0 * LN_EPS).astype(BF16)
    u2 = _gelu_tanh_x2(_dot(hb, win_ref[:, 0:e_dim]))
    ug = u2 * _silu(_dot(hb, win_ref[:, 2 * e_dim:3 * e_dim]))
    for c in range(tm // CHUNK):
        rows = slice(c * CHUNK, (c + 1) * CHUNK)
        for hh in range(H_B):
            cols = slice(hh * dh, (hh + 1) * dh)
            sv = _dot(ws_ref[hh], v[rows, cols]) + bs_ref[:, hh:hh + 1]
            y_ref[rows, cols] = (ug[rows, cols] * sv).astype(BF16)
    o2 = _dot(y_ref[...], wout_ref[...])
    o_ref[...] = _resid_layer_norm(x, 0.5 * gate, o2, g_ref[...], b_ref[...])


def _sgu_layer(x2d, seq_len, per_sample, mods, layer, w_in, sln_g, sln_b, w_s, b_s_t, w_out, ln_g, ln_b):
    t, d = x2d.shape
    tm = ROW_TILE
    tps = max(seq_len // tm, 1)
    e_dim = w_out.shape[0]
    return pl.pallas_call(
        _sgu_kernel,
        grid=(t // tm,),
        in_specs=[
            pl.BlockSpec((tm, d), lambda i: (i, 0)),
            _mod_spec(layer, d, tps, per_sample),
            _const_spec(w_in.shape),
            _const_spec((1, e_dim)),
            _const_spec((1, e_dim)),
            _const_spec(w_s.shape),
            _const_spec(b_s_t.shape),
            _const_spec(w_out.shape),
            _const_spec((1, d)),
            _const_spec((1, d)),
        ],
        out_specs=pl.BlockSpec((tm, d), lambda i: (i, 0)),
        out_shape=jax.ShapeDtypeStruct((t, d), F32),
        scratch_shapes=[pltpu.VMEM((tm, e_dim), BF16)],
        compiler_params=_params(1),
        name="sgu_layer",
    )(x2d, mods, w_in, sln_g.reshape(1, e_dim), sln_b.reshape(1, e_dim), w_s, b_s_t, w_out,
      ln_g.reshape(1, d), ln_b.reshape(1, d))


def _na_proj_kernel(emit_f32_kv, q_scale, x_ref, mod_ref, win_ref, *out_refs):
    tm, d = x_ref.shape
    e = win_ref.shape[1] // 4
    shift, scale, _ = _split_mod(mod_ref, d)
    hb = (x_ref[...] * (1.0 + scale) + shift).astype(BF16)
    q_ref, k_ref, v_ref, sg_ref = out_refs[:4]
    q_ref[...] = (_dot(hb, win_ref[:, 0:e].astype(BF16)) * q_scale).astype(BF16)
    k = _dot(hb, win_ref[:, e:2 * e].astype(BF16))
    v = _dot(hb, win_ref[:, 2 * e:3 * e].astype(BF16))
    k_ref[...] = k.astype(BF16)
    v_ref[...] = v.astype(BF16)
    sg_ref[...] = _silu(_dot(hb, win_ref[:, 3 * e:4 * e].astype(BF16))).astype(BF16)
    if emit_f32_kv:
        seq = out_refs[4].shape[2]
        for kv, ref in ((k, out_refs[4]), (v, out_refs[5])):
            for sq in range(tm // seq):
                ref[sq] = kv[sq * seq:(sq + 1) * seq, :].T


def _na_proj(x2d, seq_len, per_sample, mods, layer, w_in, emit_f32_kv, q_scale):
    t, d = x2d.shape
    tm = ROW_TILE
    tps = max(seq_len // tm, 1)
    e = w_in.shape[1] // 4
    row_spec = pl.BlockSpec((tm, e), lambda i: (i, 0))
    out_specs = [row_spec] * 4
    out_shape = [jax.ShapeDtypeStruct((t, e), BF16)] * 4
    if emit_f32_kv:
        out_specs += [pl.BlockSpec((tm // seq_len, e, seq_len), lambda i: (i, 0, 0))] * 2
        out_shape += [jax.ShapeDtypeStruct((t // seq_len, e, seq_len), F32)] * 2
    return pl.pallas_call(
        functools.partial(_na_proj_kernel, emit_f32_kv, q_scale),
        grid=(t // tm,),
        in_specs=[
            pl.BlockSpec((tm, d), lambda i: (i, 0)),
            _mod_spec(layer, d, tps, per_sample),
            _const_spec(w_in.shape),
        ],
        out_specs=out_specs,
        out_shape=out_shape,
        compiler_params=_params(1),
        name="na_proj",
    )(x2d, mods, w_in)


def _lane_is_first_head():
    return lax.broadcasted_iota(jnp.int32, (1, LANES), 1) < DH_C


def _ctx_attn_kernel(q_ref, k_ref, v_ref, sg_ref, o_ref):
    first = _lane_is_first_head()
    for j in range(q_ref.shape[1] // LANES):
        cols = slice(j * LANES, (j + 1) * LANES)
        q, k, v = q_ref[:, cols], k_ref[:, cols], v_ref[:, cols]
        outs = []
        for sel in (first, jnp.logical_not(first)):
            s = _dot_nt(jnp.where(sel, q, jnp.zeros_like(q)), k)
            p = jnp.exp(s - jnp.max(s, axis=-1, keepdims=True))
            l = jnp.sum(p, axis=-1, keepdims=True)
            outs.append(_dot(p.astype(BF16), v) / l)
        o_ref[:, cols] = (jnp.where(first, outs[0], outs[1]) * sg_ref[:, cols].astype(F32)).astype(BF16)


def _ctx_attn(q, k, v, sg, seq_len):
    t, e = q.shape
    spec = pl.BlockSpec((seq_len, e), lambda b: (b, 0))
    return pl.pallas_call(
        _ctx_attn_kernel,
        grid=(t // seq_len,),
        in_specs=[spec, spec, spec, spec],
        out_specs=spec,
        out_shape=jax.ShapeDtypeStruct((t, e), BF16),
        compiler_params=_params(1),
        name="ctx_attn",
    )(q, k, v, sg)


def _key_row_start(r_blk, rows):
    return jnp.clip(r_blk * ATT_Q_ROWS - WIN_R // 2, 0, rows - ATT_K_ROWS)


def _nbr_bias_build(rows, rpb_ref, bias_ref):
    n_blk = rows // ATT_Q_ROWS
    kc = lax.broadcasted_iota(jnp.int32, (GRID_W, LANES), 0)
    lane = lax.broadcasted_iota(jnp.int32, (GRID_W, LANES), 1)
    qc = lane & (GRID_W - 1)
    c0 = jnp.clip(qc - WIN_C // 2, 0, GRID_W - WIN_C)
    col_in = (kc >= c0) & (kc < c0 + WIN_C)
    first = lane < GRID_W
    neg = jnp.full((GRID_W, LANES), -jnp.inf, F32)
    for hh in range(2):
        lo, hi = [], []
        for dr in range(2 * WIN_R - 1):
            row = jnp.broadcast_to(rpb_ref[hh, dr:dr + 1, :] * LOG2_E, (GRID_W, LANES))
            lo.append(pltpu.roll(row, LANES - (WIN_C - 1), 1, stride=1, stride_axis=0))
            hi.append(pltpu.roll(row, GRID_W - (WIN_C - 1), 1, stride=1, stride_axis=0))
        for var, r_blk in enumerate((0, 1, n_blk - 1)):
            ks = min(max(r_blk * ATT_Q_ROWS - WIN_R // 2, 0), rows - ATT_K_ROWS)
            for ki in range(ATT_K_ROWS):
                kr = ks + ki
                for qp in range(ATT_Q_ROWS // 2):
                    halves = []
                    for half, src in enumerate((lo, hi)):
                        r = r_blk * ATT_Q_ROWS + 2 * qp + half
                        r0 = min(max(r - WIN_R // 2, 0), rows - WIN_R)
                        halves.append(src[kr - r + WIN_R - 1] if r0 <= kr < r0 + WIN_R else neg)
                    blk = jnp.where(col_in, jnp.where(first, halves[0], halves[1]), neg)
                    bias_ref[hh, var, ki * GRID_W:(ki + 1) * GRID_W, qp * LANES:(qp + 1) * LANES] = blk


def _nbr_attn_kernel(rows, rpb_ref, q_ref, k_ref, v_ref, sg_ref, ck_ref, cv_ref, o_ref, bias_ref, kc_ref, vx_ref,
                     s_ref, p_ref, oa_ref):
    n_blk = rows // ATT_Q_ROWS
    tq = ATT_Q_ROWS * GRID_W
    nk = ATT_K_ROWS * GRID_W
    seq = q_ref.shape[0]
    n_tok_tiles = seq // MXU_TILE
    n_ctx_tiles = kc_ref.shape[0] // MXU_TILE

    @pl.when(pl.program_id(1) == 0)
    def _():
        _nbr_bias_build(rows, rpb_ref, bias_ref)

    first = _lane_is_first_head()
    head_lanes = (first, jnp.logical_not(first))
    kc_ref[...] = ck_ref[...].T.astype(BF16)
    v_t = v_ref[...].astype(F32).T
    top = lax.broadcasted_iota(jnp.int32, (LANES, 1), 0) < DH_C
    for hh, sel in enumerate((top, jnp.logical_not(top))):
        for kt in range(n_tok_tiles):
            vx_ref[hh, kt] = jnp.where(sel, v_t[:, kt * MXU_TILE:(kt + 1) * MXU_TILE], 1.0).astype(BF16)
        for kt in range(n_ctx_tiles):
            vx_ref[hh, n_tok_tiles + kt] = jnp.where(
                sel, cv_ref[:, kt * MXU_TILE:(kt + 1) * MXU_TILE], 1.0).astype(BF16)

    def q_rows(r):
        return pl.ds(pl.multiple_of(r * tq, tq), tq)

    def k_start(r):
        return pl.multiple_of(_key_row_start(r, rows) * GRID_W, MXU_TILE)

    def logits(r, hh):
        q = q_ref[q_rows(r), :]
        qh = jnp.where(head_lanes[hh], q, jnp.zeros_like(q))
        var = jnp.where(r == 0, 0, jnp.where(r == n_blk - 1, 2, 1))
        s_ref[hh, 0:nk, :] = _dot_nt(k_ref[pl.ds(k_start(r), nk), :], qh) + bias_ref[hh, var]
        s_ref[hh, nk:, :] = _dot_nt(kc_ref[...], qh)

    def softmax(hh):
        s = s_ref[hh]
        p_ref[hh] = jnp.exp2(s - jnp.max(s, axis=0, keepdims=True)).astype(BF16)

    def weighted_values(r, hh):
        kt0 = k_start(r) // MXU_TILE
        tiles = [vx_ref[hh, kt0 + i] for i in range(nk // MXU_TILE)]
        tiles += [vx_ref[hh, n_tok_tiles + i] for i in range(n_ctx_tiles)]
        return _dot(jnp.concatenate(tiles, axis=1), p_ref[hh])

    def emit(r, o_second):
        o_first = oa_ref[...]
        num = jnp.concatenate([o_first[0:DH_C], o_second[DH_C:]], axis=0)
        den = jnp.concatenate([o_first[DH_C:], o_second[0:DH_C]], axis=0)
        o_ref[q_rows(r), :] = ((num / den).T * sg_ref[q_rows(r), :].astype(F32)).astype(BF16)

    p_ref[1] = jnp.ones(p_ref.shape[1:], BF16)
    oa_ref[...] = jnp.ones(oa_ref.shape, F32)
    logits(0, 0)

    def block(r, carry):
        prev = jnp.maximum(r - 1, 0)
        emit(prev, weighted_values(prev, 1))
        logits(r, 1)
        softmax(0)
        oa_ref[...] = weighted_values(r, 0)
        logits(jnp.minimum(r + 1, n_blk - 1), 0)
        softmax(1)
        return carry

    lax.fori_loop(0, n_blk, block, 0, unroll=2)
    emit(n_blk - 1, weighted_values(n_blk - 1, 1))


def _nbr_attn(q, k, v, sg, cache_k, cache_v, cache_layer, rpb, n_batch):
    t, e = q.shape
    seq = t // n_batch
    rows = seq // GRID_W
    n_cached, past = cache_k.shape[1:3]
    ck = jnp.transpose(cache_k, (0, 1, 3, 4, 2)).reshape(n_batch * n_cached, e, past)
    cv = jnp.transpose(cache_v, (0, 1, 3, 4, 2)).reshape(n_batch * n_cached, e, past)
    tq, nk = ATT_Q_ROWS * GRID_W, ATT_K_ROWS * GRID_W
    assert nk % MXU_TILE == 0 and past % MXU_TILE == 0 and (ATT_Q_ROWS * GRID_W) % MXU_TILE == 0
    n_dr, n_dc = rpb.shape[1:]
    rpb_pad = jnp.pad(rpb.astype(F32)[:, :, ::-1], ((0, 0), (0, 2 * WIN_R - n_dr), (0, LANES - n_dc)))
    tok_spec = pl.BlockSpec((seq, LANES), lambda j, b: (b, j))
    ctx_spec = pl.BlockSpec((None, LANES, past), lambda j, b: (b * n_cached + cache_layer, j, 0))
    return pl.pallas_call(
        functools.partial(_nbr_attn_kernel, rows),
        grid=(e // LANES, n_batch),
        in_specs=[pl.BlockSpec((2, 2 * WIN_R, LANES), lambda j, b: (j, 0, 0)),
                  tok_spec, tok_spec, tok_spec, tok_spec, ctx_spec, ctx_spec],
        out_specs=tok_spec,
        out_shape=jax.ShapeDtypeStruct((t, e), BF16),
        scratch_shapes=[
            pltpu.VMEM((2, 3, nk, tq), F32),
            pltpu.VMEM((past, LANES), BF16),
            pltpu.VMEM((2, (seq + past) // MXU_TILE, LANES, MXU_TILE), BF16),
            pltpu.VMEM((2, nk + past, tq), F32),
            pltpu.VMEM((2, nk + past, tq), BF16),
            pltpu.VMEM((LANES, tq), F32),
        ],
        compiler_params=_params(2),
        name="nbr_attn",
    )(rpb_pad, q, k, v, sg, ck, cv)


def _na_out_kernel(x_ref, y_ref, mod_ref, wout_ref, g_ref, b_ref, o_ref):
    tm, d = x_ref.shape
    _, _, gate = _split_mod(mod_ref, d)
    o = _dot(y_ref[...], wout_ref[...].astype(BF16))
    o_ref[...] = _resid_layer_norm(x_ref[...], gate, o, g_ref[...], b_ref[...])


def _na_out(x2d, gated, seq_len, per_sample, mods, layer, w_out, ln_g, ln_b):
    t, d = x2d.shape
    tm = ROW_TILE
    tps = max(seq_len // tm, 1)
    e = w_out.shape[0]
    return pl.pallas_call(
        _na_out_kernel,
        grid=(t // tm,),
        in_specs=[
            pl.BlockSpec((tm, d), lambda i: (i, 0)),
            pl.BlockSpec((tm, e), lambda i: (i, 0)),
            _mod_spec(layer, d, tps, per_sample),
            _const_spec(w_out.shape),
            _const_spec((1, d)),
            _const_spec((1, d)),
        ],
        out_specs=pl.BlockSpec((tm, d), lambda i: (i, 0)),
        out_shape=jax.ShapeDtypeStruct((t, d), F32),
        compiler_params=_params(1),
        name="na_out",
    )(x2d, gated, mods, w_out, ln_g.reshape(1, d), ln_b.reshape(1, d))


def kernel(x_prompt, x_sample, c, cache_k, cache_v, c_ctx, w_mod, b_mod, ln_g, ln_b, pool_w_in, pool_w_grp,
           pool_scale, pool_w_out, sgu_w_in, sgu_ln_g, sgu_ln_b, sgu_w_s, sgu_b_s, sgu_w_out, na_w_in, na_rpb,
           na_w_out):
    n_p, seq_p, d = x_prompt.shape
    n_s, seq_s, _ = x_sample.shape
    assert n_s + 1 <= N_COND_ROWS and (n_p * seq_p) % ROW_TILE == 0 and seq_s % ROW_TILE == 0
    assert seq_p % CHUNK == 0 and (seq_p % ROW_TILE == 0 or ROW_TILE % seq_p == 0) and seq_p % POOL_BLOCK == 0
    assert seq_s % (GRID_W * ATT_Q_ROWS) == 0 and seq_s // GRID_W >= ATT_K_ROWS
    conds = jnp.zeros((N_COND_ROWS, d), F32).at[0].set(c_ctx).at[1:1 + n_s].set(c)
    mods = _mods(conds, w_mod, b_mod)
    pool_w_fold = _pool_fold(pool_w_in, pool_w_grp)
    pool_band = _pool_band()

    yp = x_prompt.reshape(n_p * seq_p, d)
    ys = x_sample.reshape(n_s * seq_s, d)
    streams = ((seq_p, False), (seq_s, True))
    ctx_k = ctx_v = None
    for i in range(DEPTH):
        kind, j = i % N_MIXERS, i // N_MIXERS
        ys_in = (yp, ys)
        outs = []
        if kind == 0:
            for x2d, (seq, per_sample) in zip(ys_in, streams):
                outs.append(_pool_layer(x2d, seq, per_sample, mods, i, j, pool_band, pool_w_fold, pool_w_in,
                                        pool_scale, pool_w_out, ln_g[i], ln_b[i]))
        elif kind == 1:
            w_in, w_s, w_out = sgu_w_in[j].astype(BF16), sgu_w_s[j].astype(BF16), sgu_w_out[j].astype(BF16)
            for x2d, (seq, per_sample) in zip(ys_in, streams):
                outs.append(_sgu_layer(x2d, seq, per_sample, mods, i, w_in, sgu_ln_g[j], sgu_ln_b[j], w_s,
                                       sgu_b_s[j].T, w_out, ln_g[i], ln_b[i]))
        else:
            w_in, w_out = na_w_in[j], na_w_out[j]
            q, k, v, sg, kf, vf = _na_proj(yp, seq_p, False, mods, i, w_in, True, DH_C ** -0.5)
            ctx_k = kf.reshape(n_p, 1, H_C, DH_C, seq_p).transpose(0, 1, 4, 2, 3)
            ctx_v = vf.reshape(n_p, 1, H_C, DH_C, seq_p).transpose(0, 1, 4, 2, 3)
            gated = _ctx_attn(q, k, v, sg, seq_p)
            outs.append(_na_out(yp, gated, seq_p, False, mods, i, w_out, ln_g[i], ln_b[i]))
            q, k, v, sg = _na_proj(ys, seq_s, True, mods, i, w_in, False, DH_C ** -0.5 * LOG2_E)
            gated = _nbr_attn(q, k, v, sg, cache_k, cache_v, j, na_rpb[j], n_s)
            outs.append(_na_out(ys, gated, seq_s, True, mods, i, w_out, ln_g[i], ln_b[i]))
        yp, ys = outs
    return (yp.reshape(n_p, seq_p, d), ys.reshape(n_s, seq_s, d), ctx_k, ctx_v)
```

```python
import functools

import jax
import jax.numpy as jnp
import numpy as np
from jax import lax
from jax.experimental import pallas as pl
from jax.experimental.pallas import tpu as pltpu

F32 = jnp.float32
BF16 = jnp.bfloat16

DEPTH = 4
N_MIXERS = 3
POOL_WINDOWS = (2, 4, 8, 16)
POOL_HALO = 16
POOL_BLOCK = 128
CHUNK = 128
H_B = 8
H_C = 16
DH_C = 64
GRID_W = 64
WIN_R = 8
WIN_C = 16
DEEPNORM_ALPHA = (2 * DEPTH) ** 0.25
LN_EPS = 1e-5
LOG2_E = float(np.log2(np.e))

N_COND_ROWS = 8
ROW_TILE = 512
ATT_Q_ROWS = 4
ATT_K_ROWS = 12
LANES = 128
MXU_TILE = 256
VMEM_LIMIT = 56 * 1024 * 1024


def _const_spec(shape):
    nd = len(shape)
    return pl.BlockSpec(shape, lambda *_: (0,) * nd, pipeline_mode=pl.Buffered(1))


def _params(n_axes):
    return pltpu.CompilerParams(dimension_semantics=("arbitrary",) * n_axes, vmem_limit_bytes=VMEM_LIMIT)


def _mod_spec(layer, d, tiles_per_seq, per_sample):
    base = layer * N_COND_ROWS
    if per_sample:
        return pl.BlockSpec((1, 1, 3 * d), lambda i: (base + 1 + i // tiles_per_seq, 0, 0))
    return pl.BlockSpec((1, 1, 3 * d), lambda i: (base, 0, 0))


def _split_mod(mod_ref, d):
    m = mod_ref[0]
    return m[:, :d], m[:, d:2 * d], m[:, 2 * d:]


def _silu(x):
    return x / (1.0 + jnp.exp(-x))


def _gelu_tanh_x2(x):
    c = np.float32(np.sqrt(2.0 / np.pi))
    return x * (1.0 + jnp.tanh(x * (c + (c * np.float32(0.044715)) * (x * x))))


def _layer_norm(x, g, b, eps=LN_EPS):
    mu = jnp.mean(x, axis=-1, keepdims=True)
    d = x - mu
    var = jnp.mean(d * d, axis=-1, keepdims=True)
    return d * lax.rsqrt(var + eps) * g + b


def _resid_layer_norm(x, gate, o, g, b):
    return _layer_norm(x + (gate * (1.0 / DEEPNORM_ALPHA)) * o, g, b, LN_EPS / DEEPNORM_ALPHA ** 2)


def _dot(a, b):
    return jnp.dot(a, b, preferred_element_type=F32)


def _dot_nt(a, b):
    return lax.dot_general(a, b, (((1,), (1,)), ((), ())), preferred_element_type=F32)


def _mods_kernel(cond_ref, w_ref, b_ref, o_ref):
    a = _silu(cond_ref[...]).astype(BF16)
    o_ref[0] = _dot(a, w_ref[0].astype(BF16)) + b_ref[0]


def _mods(conds, w_mod, b_mod):
    depth, d, n = w_mod.shape
    tn = n
    out = pl.pallas_call(
        _mods_kernel,
        grid=(depth, n // tn),
        in_specs=[
            pl.BlockSpec((N_COND_ROWS, d), lambda l, j: (0, 0)),
            pl.BlockSpec((1, d, tn), lambda l, j: (l, 0, j)),
            pl.BlockSpec((1, 1, tn), lambda l, j: (l, 0, j)),
        ],
        out_specs=pl.BlockSpec((1, N_COND_ROWS, tn), lambda l, j: (l, 0, j)),
        out_shape=jax.ShapeDtypeStruct((depth, N_COND_ROWS, n), F32),
        compiler_params=_params(2),
        name="adaln_mods",
    )(conds, w_mod, b_mod.reshape(depth, 1, n))
    return out.reshape(depth * N_COND_ROWS, 1, n)


def _pool_fold_kernel(win_x_ref, wgrp_ref, fold_ref):
    fold_ref[0] = _dot(win_x_ref[0].astype(BF16), wgrp_ref[0, 0].astype(BF16)).astype(BF16)


def _pool_fold(w_in, w_grp):
    n_layers, d, two_e = w_in.shape
    n_grp, grp = w_grp.shape[1:3]
    e_dim = two_e // 2
    return pl.pallas_call(
        _pool_fold_kernel,
        grid=(n_layers, n_grp),
        in_specs=[
            pl.BlockSpec((1, d, grp), lambda l, g: (l, 0, g)),
            pl.BlockSpec((1, 1, grp, grp), lambda l, g: (l, g, 0, 0)),
        ],
        out_specs=pl.BlockSpec((1, d, grp), lambda l, g: (l, 0, g)),
        out_shape=jax.ShapeDtypeStruct((n_layers, d, e_dim), BF16),
        compiler_params=_params(2),
        name="pool_fold",
    )(w_in, w_grp)


def _pool_band():
    t = np.arange(POOL_BLOCK)[:, None] + POOL_HALO
    e = np.arange(POOL_BLOCK + 2 * POOL_HALO)[None, :]
    return jnp.asarray(np.stack([(e >= t - w // 2) & (e <= t + w // 2 - 1) for w in POOL_WINDOWS]), BF16)


def _pool_kernel(seq_len, x_ref, xp_ref, xn_ref, mod_ref, band_ref, wfold_ref, wgate_ref, psc_ref, wout_ref,
                 g_ref, b_ref, o_ref, e_ref, u_ref):
    tm, d = x_ref.shape
    e_dim = psc_ref.shape[1]
    grp = e_dim // len(POOL_WINDOWS)
    seg = min(tm, seq_len)
    stride = seg + 2 * POOL_HALO
    shift, scale, gate = _split_mod(mod_ref, d)
    x = x_ref[...]
    hb = (x * (1.0 + scale) + shift).astype(BF16)
    zeros = jnp.zeros((POOL_HALO, d), BF16)
    seg_pos = lax.broadcasted_iota(jnp.int32, (seg, 1), 0)
    if seg == tm:
        first_pos = (pl.program_id(0) * tm) % seq_len
        pos = first_pos + seg_pos
        hp = (xp_ref[...] * (1.0 + scale) + shift).astype(BF16)
        hn = (xn_ref[...] * (1.0 + scale) + shift).astype(BF16)
        halos = [(jnp.where(first_pos != 0, hp, zeros), jnp.where(first_pos + tm != seq_len, hn, zeros))]
    else:
        pos = jnp.concatenate([seg_pos] * (tm // seg), axis=0)
        halos = [(zeros, zeros)] * (tm // seg)
    for sg, (before, after) in enumerate(halos):
        e_ref[sg * stride:sg * stride + POOL_HALO] = before
        e_ref[sg * stride + POOL_HALO:sg * stride + POOL_HALO + seg] = hb[sg * seg:(sg + 1) * seg]
        e_ref[sg * stride + POOL_HALO + seg:(sg + 1) * stride] = after
    u_ref[...] = _dot(e_ref[...], wfold_ref[...]).astype(BF16)

    mixed = []
    for gi, w in enumerate(POOL_WINDOWS):
        cols = slice(gi * grp, (gi + 1) * grp)
        lo = jnp.maximum(pos - w // 2, 0)
        hi = jnp.minimum(pos + w // 2 - 1, seq_len - 1)
        inv_cnt = 1.0 / (hi - lo + 1).astype(F32)
        sums, own = [], []
        for sg in range(len(halos)):
            for rb in range(seg // POOL_BLOCK):
                r0 = sg * stride + rb * POOL_BLOCK
                sums.append(_dot(band_ref[gi], u_ref[r0:r0 + POOL_BLOCK + 2 * POOL_HALO, cols]))
            own.append(u_ref[sg * stride + POOL_HALO:sg * stride + POOL_HALO + seg, cols])
        sums = jnp.concatenate(sums, axis=0)
        own = own[0] if len(own) == 1 else jnp.concatenate(own, axis=0)
        mixed.append(sums * inv_cnt - own.astype(F32))
    mixed = jnp.concatenate(mixed, axis=1)
    gate_pre = _dot(hb, wgate_ref[...].astype(BF16))
    y = (mixed * psc_ref[...] * _silu(gate_pre)).astype(BF16)
    o = _dot(y, wout_ref[...].astype(BF16))
    o_ref[...] = _resid_layer_norm(x, gate, o, g_ref[...], b_ref[...])


def _pool_layer(x2d, seq_len, per_sample, mods, layer, j, band, w_fold, w_in, p_scale, w_out, ln_g, ln_b):
    t, d = x2d.shape
    tm = ROW_TILE
    tps = max(seq_len // tm, 1)
    e_dim = w_out.shape[1]
    hb = tm // POOL_HALO
    last = t // POOL_HALO - 1
    n_seg = max(tm // seq_len, 1)
    e_rows = tm + 2 * POOL_HALO * n_seg

    def layer_spec(shape):
        return pl.BlockSpec((None,) + shape, lambda i: (j,) + (0,) * len(shape), pipeline_mode=pl.Buffered(1))

    return pl.pallas_call(
        functools.partial(_pool_kernel, seq_len),
        grid=(t // tm,),
        in_specs=[
            pl.BlockSpec((tm, d), lambda i: (i, 0)),
            pl.BlockSpec((POOL_HALO, d), lambda i: (jnp.maximum(i * hb - 1, 0), 0)),
            pl.BlockSpec((POOL_HALO, d), lambda i: (jnp.minimum((i + 1) * hb, last), 0)),
            _mod_spec(layer, d, tps, per_sample),
            _const_spec(band.shape),
            layer_spec((d, e_dim)),
            pl.BlockSpec((None, d, e_dim), lambda i: (j, 0, 1), pipeline_mode=pl.Buffered(1)),
            layer_spec((1, e_dim)),
            layer_spec((e_dim, d)),
            _const_spec((1, d)),
            _const_spec((1, d)),
        ],
        out_specs=pl.BlockSpec((tm, d), lambda i: (i, 0)),
        out_shape=jax.ShapeDtypeStruct((t, d), F32),
        scratch_shapes=[pltpu.VMEM((e_rows, d), BF16), pltpu.VMEM((e_rows, e_dim), BF16)],
        compiler_params=_params(1),
        name="pool_layer",
    )(x2d, x2d, x2d, mods, band, w_fold, w_in, p_scale.reshape(-1, 1, e_dim), w_out,
      ln_g.reshape(1, d), ln_b.reshape(1, d))


def _sgu_kernel(x_ref, mod_ref, win_ref, lg_ref, lb_ref, ws_ref, bs_ref, wout_ref, g_ref, b_ref, o_ref, y_ref):
    tm, d = x_ref.shape
    e_dim = lg_ref.shape[1]
    dh = e_dim // H_B
    shift, scale, gate = _split_mod(mod_ref, d)
    x = x_ref[...]
    hb = (x * (1.0 + scale) + shift).astype(BF16)
    v2 = _gelu_tanh_x2(_dot(hb, win_ref[:, e_dim:2 * e_dim]))
    v = _layer_norm(v2, lg_ref[...], lb_ref[...], 4.0 * LN_EPS).astype(BF16)
    u2 = _gelu_tanh_x2(_dot(hb, win_ref[:, 0:e_dim]))
    ug = u2 * _silu(_dot(hb, win_ref[:, 2 * e_dim:3 * e_dim]))
    for c in range(tm // CHUNK):
        rows = slice(c * CHUNK, (c + 1) * CHUNK)
        for hh in range(H_B):
            cols = slice(hh * dh, (hh + 1) * dh)
            sv = _dot(ws_ref[hh], v[rows, cols]) + bs_ref[:, hh:hh + 1]
            y_ref[rows, cols] = (ug[rows, cols] * sv).astype(BF16)
    o2 = _dot(y_ref[...], wout_ref[...])
    o_ref[...] = _resid_layer_norm(x, 0.5 * gate, o2, g_ref[...], b_ref[...])


def _sgu_layer(x2d, seq_len, per_sample, mods, layer, w_in, sln_g, sln_b, w_s, b_s_t, w_out, ln_g, ln_b):
    t, d = x2d.shape
    tm = ROW_TILE
    tps = max(seq_len // tm, 1)
    e_dim = w_out.shape[0]
    return pl.pallas_call(
        _sgu_kernel,
        grid=(t // tm,),
        in_specs=[
            pl.BlockSpec((tm, d), lambda i: (i, 0)),
            _mod_spec(layer, d, tps, per_sample),
            _const_spec(w_in.shape),
            _const_spec((1, e_dim)),
            _const_spec((1, e_dim)),
            _const_spec(w_s.shape),
            _const_spec(b_s_t.shape),
            _const_spec(w_out.shape),
            _const_spec((1, d)),
            _const_spec((1, d)),
        ],
        out_specs=pl.BlockSpec((tm, d), lambda i: (i, 0)),
        out_shape=jax.ShapeDtypeStruct((t, d), F32),
        scratch_shapes=[pltpu.VMEM((tm, e_dim), BF16)],
        compiler_params=_params(1),
        name="sgu_layer",
    )(x2d, mods, w_in, sln_g.reshape(1, e_dim), sln_b.reshape(1, e_dim), w_s, b_s_t, w_out,
      ln_g.reshape(1, d), ln_b.reshape(1, d))


def _na_proj_kernel(emit_f32_kv, q_scale, x_ref, mod_ref, win_ref, *out_refs):
    tm, d = x_ref.shape
    e = win_ref.shape[1] // 4
    shift, scale, _ = _split_mod(mod_ref, d)
    hb = (x_ref[...] * (1.0 + scale) + shift).astype(BF16)
    q_ref, k_ref, v_ref, sg_ref = out_refs[:4]
    q_ref[...] = (_dot(hb, win_ref[:, 0:e].astype(BF16)) * q_scale).astype(BF16)
    k = _dot(hb, win_ref[:, e:2 * e].astype(BF16))
    v = _dot(hb, win_ref[:, 2 * e:3 * e].astype(BF16))
    k_ref[...] = k.astype(BF16)
    v_ref[...] = v.astype(BF16)
    sg_ref[...] = _silu(_dot(hb, win_ref[:, 3 * e:4 * e].astype(BF16))).astype(BF16)
    if emit_f32_kv:
        seq = out_refs[4].shape[2]
        for kv, ref in ((k, out_refs[4]), (v, out_refs[5])):
            for sq in range(tm // seq):
                ref[sq] = kv[sq * seq:(sq + 1) * seq, :].T


def _na_proj(x2d, seq_len, per_sample, mods, layer, w_in, emit_f32_kv, q_scale):
    t, d = x2d.shape
    tm = ROW_TILE
    tps = max(seq_len // tm, 1)
    e = w_in.shape[1] // 4
    row_spec = pl.BlockSpec((tm, e), lambda i: (i, 0))
    out_specs = [row_spec] * 4
    out_shape = [jax.ShapeDtypeStruct((t, e), BF16)] * 4
    if emit_f32_kv:
        out_specs += [pl.BlockSpec((tm // seq_len, e, seq_len), lambda i: (i, 0, 0))] * 2
        out_shape += [jax.ShapeDtypeStruct((t // seq_len, e, seq_len), F32)] * 2
    return pl.pallas_call(
        functools.partial(_na_proj_kernel, emit_f32_kv, q_scale),
        grid=(t // tm,),
        in_specs=[
            pl.BlockSpec((tm, d), lambda i: (i, 0)),
            _mod_spec(layer, d, tps, per_sample),
            _const_spec(w_in.shape),
        ],
        out_specs=out_specs,
        out_shape=out_shape,
        compiler_params=_params(1),
        name="na_proj",
    )(x2d, mods, w_in)


def _lane_is_first_head():
    return lax.broadcasted_iota(jnp.int32, (1, LANES), 1) < DH_C


def _ctx_attn_kernel(seq_len, x_ref, q_ref, k_ref, v_ref, sg_ref, mod_ref, wout_ref, g_ref, b_ref, o_ref, y_ref):
    tm, d = x_ref.shape
    first = _lane_is_first_head()
    for sq in range(tm // seq_len):
        rows = slice(sq * seq_len, (sq + 1) * seq_len)
        for j in range(q_ref.shape[1] // LANES):
            cols = slice(j * LANES, (j + 1) * LANES)
            q, k, v = q_ref[rows, cols], k_ref[rows, cols], v_ref[rows, cols]
            outs = []
            for sel in (first, jnp.logical_not(first)):
                s = _dot_nt(jnp.where(sel, q, jnp.zeros_like(q)), k)
                p = jnp.exp(s - jnp.max(s, axis=-1, keepdims=True))
                l = jnp.sum(p, axis=-1, keepdims=True)
                outs.append(_dot(p.astype(BF16), v) / l)
            gated = jnp.where(first, outs[0], outs[1]) * sg_ref[rows, cols].astype(F32)
            y_ref[rows, cols] = gated.astype(BF16)
    _, _, gate = _split_mod(mod_ref, d)
    o = _dot(y_ref[...], wout_ref[...].astype(BF16))
    o_ref[...] = _resid_layer_norm(x_ref[...], gate, o, g_ref[...], b_ref[...])


def _ctx_attn_out(x2d, q, k, v, sg, seq_len, mods, layer, w_out, ln_g, ln_b):
    t, d = x2d.shape
    e = q.shape[1]
    tm = ROW_TILE
    row = pl.BlockSpec((tm, e), lambda i: (i, 0))
    return pl.pallas_call(
        functools.partial(_ctx_attn_kernel, seq_len),
        grid=(t // tm,),
        in_specs=[pl.BlockSpec((tm, d), lambda i: (i, 0)), row, row, row, row,
                  _mod_spec(layer, d, 1, False), _const_spec(w_out.shape), _const_spec((1, d)), _const_spec((1, d))],
        out_specs=pl.BlockSpec((tm, d), lambda i: (i, 0)),
        out_shape=jax.ShapeDtypeStruct((t, d), F32),
        scratch_shapes=[pltpu.VMEM((tm, e), BF16)],
        compiler_params=_params(1),
        name="ctx_attn_out",
    )(x2d, q, k, v, sg, mods, w_out, ln_g.reshape(1, d), ln_b.reshape(1, d))


def _key_row_start(r_blk, rows):
    return jnp.clip(r_blk * ATT_Q_ROWS - WIN_R // 2, 0, rows - ATT_K_ROWS)


def _nbr_bias_build(rows, rpb_ref, bias_ref):
    n_blk = rows // ATT_Q_ROWS
    kc = lax.broadcasted_iota(jnp.int32, (GRID_W, LANES), 0)
    lane = lax.broadcasted_iota(jnp.int32, (GRID_W, LANES), 1)
    qc = lane & (GRID_W - 1)
    c0 = jnp.clip(qc - WIN_C // 2, 0, GRID_W - WIN_C)
    col_in = (kc >= c0) & (kc < c0 + WIN_C)
    first = lane < GRID_W
    neg = jnp.full((GRID_W, LANES), -jnp.inf, F32)
    for hh in range(2):
        lo, hi = [], []
        for dr in range(2 * WIN_R - 1):
            row = jnp.broadcast_to(rpb_ref[hh, dr:dr + 1, :] * LOG2_E, (GRID_W, LANES))
            lo.append(pltpu.roll(row, LANES - (WIN_C - 1), 1, stride=1, stride_axis=0))
            hi.append(pltpu.roll(row, GRID_W - (WIN_C - 1), 1, stride=1, stride_axis=0))
        for var, r_blk in enumerate((0, 1, n_blk - 1)):
            ks = min(max(r_blk * ATT_Q_ROWS - WIN_R // 2, 0), rows - ATT_K_ROWS)
            for ki in range(ATT_K_ROWS):
                kr = ks + ki
                for qp in range(ATT_Q_ROWS // 2):
                    halves = []
                    for half, src in enumerate((lo, hi)):
                        r = r_blk * ATT_Q_ROWS + 2 * qp + half
                        r0 = min(max(r - WIN_R // 2, 0), rows - WIN_R)
                        halves.append(src[kr - r + WIN_R - 1] if r0 <= kr < r0 + WIN_R else neg)
                    blk = jnp.where(col_in, jnp.where(first, halves[0], halves[1]), neg)
                    bias_ref[hh, var, ki * GRID_W:(ki + 1) * GRID_W, qp * LANES:(qp + 1) * LANES] = blk


def _nbr_attn_kernel(rows, rpb_ref, q_ref, k_ref, v_ref, sg_ref, ck_ref, cv_ref, o_ref, bias_ref, kc_ref, vx_ref,
                     s_ref, p_ref, oa_ref):
    n_blk = rows // ATT_Q_ROWS
    tq = ATT_Q_ROWS * GRID_W
    nk = ATT_K_ROWS * GRID_W
    seq = q_ref.shape[0]
    n_tok_tiles = seq // MXU_TILE
    n_ctx_tiles = kc_ref.shape[0] // MXU_TILE

    @pl.when(pl.program_id(1) == 0)
    def _():
        _nbr_bias_build(rows, rpb_ref, bias_ref)

    first = _lane_is_first_head()
    head_lanes = (first, jnp.logical_not(first))
    kc_ref[...] = ck_ref[...].T.astype(BF16)
    v_t = v_ref[...].astype(F32).T
    top = lax.broadcasted_iota(jnp.int32, (LANES, 1), 0) < DH_C
    for hh, sel in enumerate((top, jnp.logical_not(top))):
        for kt in range(n_tok_tiles):
            vx_ref[hh, kt] = jnp.where(sel, v_t[:, kt * MXU_TILE:(kt + 1) * MXU_TILE], 1.0).astype(BF16)
        for kt in range(n_ctx_tiles):
            vx_ref[hh, n_tok_tiles + kt] = jnp.where(
                sel, cv_ref[:, kt * MXU_TILE:(kt + 1) * MXU_TILE], 1.0).astype(BF16)

    def q_rows(r):
        return pl.ds(pl.multiple_of(r * tq, tq), tq)

    def k_start(r):
        return pl.multiple_of(_key_row_start(r, rows) * GRID_W, MXU_TILE)

    def logits(r, hh):
        q = q_ref[q_rows(r), :]
        qh = jnp.where(head_lanes[hh], q, jnp.zeros_like(q))
        var = jnp.where(r == 0, 0, jnp.where(r == n_blk - 1, 2, 1))
        s_ref[hh, 0:nk, :] = _dot_nt(k_ref[pl.ds(k_start(r), nk), :], qh) + bias_ref[hh, var]
        s_ref[hh, nk:, :] = _dot_nt(kc_ref[...], qh)

    def softmax(hh):
        s = s_ref[hh]
        p_ref[hh] = jnp.exp2((s - jnp.max(s, axis=0, keepdims=True)).astype(BF16))

    def weighted_values(r, hh):
        kt0 = k_start(r) // MXU_TILE
        tiles = [vx_ref[hh, kt0 + i] for i in range(nk // MXU_TILE)]
        tiles += [vx_ref[hh, n_tok_tiles + i] for i in range(n_ctx_tiles)]
        return _dot(jnp.concatenate(tiles, axis=1), p_ref[hh])

    def emit(r, o_second):
        o_first = oa_ref[...]
        num = jnp.concatenate([o_first[0:DH_C], o_second[DH_C:]], axis=0)
        den = jnp.concatenate([o_first[DH_C:], o_second[0:DH_C]], axis=0)
        o_ref[q_rows(r), :] = ((num / den).T * sg_ref[q_rows(r), :].astype(F32)).astype(BF16)

    p_ref[1] = jnp.ones(p_ref.shape[1:], BF16)
    oa_ref[...] = jnp.ones(oa_ref.shape, F32)
    logits(0, 0)

    def block(r, carry):
        prev = jnp.maximum(r - 1, 0)
        emit(prev, weighted_values(prev, 1))
        logits(r, 1)
        softmax(0)
        oa_ref[...] = weighted_values(r, 0)
        logits(jnp.minimum(r + 1, n_blk - 1), 0)
        softmax(1)
        return carry

    lax.fori_loop(0, n_blk, block, 0, unroll=2)
    emit(n_blk - 1, weighted_values(n_blk - 1, 1))


def _nbr_attn(q, k, v, sg, cache_k, cache_v, cache_layer, rpb, n_batch):
    t, e = q.shape
    seq = t // n_batch
    rows = seq // GRID_W
    n_cached, past = cache_k.shape[1:3]
    ck = jnp.transpose(cache_k, (0, 1, 3, 4, 2)).reshape(n_batch * n_cached, e, past)
    cv = jnp.transpose(cache_v, (0, 1, 3, 4, 2)).reshape(n_batch * n_cached, e, past)
    tq, nk = ATT_Q_ROWS * GRID_W, ATT_K_ROWS * GRID_W
    assert nk % MXU_TILE == 0 and past % MXU_TILE == 0 and (ATT_Q_ROWS * GRID_W) % MXU_TILE == 0
    n_dr, n_dc = rpb.shape[1:]
    rpb_pad = jnp.pad(rpb.astype(F32)[:, :, ::-1], ((0, 0), (0, 2 * WIN_R - n_dr), (0, LANES - n_dc)))
    tok_spec = pl.BlockSpec((seq, LANES), lambda j, b: (b, j))
    ctx_spec = pl.BlockSpec((None, LANES, past), lambda j, b: (b * n_cached + cache_layer, j, 0))
    return pl.pallas_call(
        functools.partial(_nbr_attn_kernel, rows),
        grid=(e // LANES, n_batch),
        in_specs=[pl.BlockSpec((2, 2 * WIN_R, LANES), lambda j, b: (j, 0, 0)),
                  tok_spec, tok_spec, tok_spec, tok_spec, ctx_spec, ctx_spec],
        out_specs=tok_spec,
        out_shape=jax.ShapeDtypeStruct((t, e), BF16),
        scratch_shapes=[
            pltpu.VMEM((2, 3, nk, tq), F32),
            pltpu.VMEM((past, LANES), BF16),
            pltpu.VMEM((2, (seq + past) // MXU_TILE, LANES, MXU_TILE), BF16),
            pltpu.VMEM((2, nk + past, tq), F32),
            pltpu.VMEM((2, nk + past, tq), BF16),
            pltpu.VMEM((LANES, tq), F32),
        ],
        compiler_params=_params(2),
        name="nbr_attn",
    )(rpb_pad, q, k, v, sg, ck, cv)


def _na_out_kernel(x_ref, y_ref, mod_ref, wout_ref, g_ref, b_ref, o_ref):
    tm, d = x_ref.shape
    _, _, gate = _split_mod(mod_ref, d)
    o = _dot(y_ref[...], wout_ref[...].astype(BF16))
    o_ref[...] = _resid_layer_norm(x_ref[...], gate, o, g_ref[...], b_ref[...])


def _na_out(x2d, gated, seq_len, per_sample, mods, layer, w_out, ln_g, ln_b):
    t, d = x2d.shape
    tm = ROW_TILE
    tps = max(seq_len // tm, 1)
    e = w_out.shape[0]
    return pl.pallas_call(
        _na_out_kernel,
        grid=(t // tm,),
        in_specs=[
            pl.BlockSpec((tm, d), lambda i: (i, 0)),
            pl.BlockSpec((tm, e), lambda i: (i, 0)),
            _mod_spec(layer, d, tps, per_sample),
            _const_spec(w_out.shape),
            _const_spec((1, d)),
            _const_spec((1, d)),
        ],
        out_specs=pl.BlockSpec((tm, d), lambda i: (i, 0)),
        out_shape=jax.ShapeDtypeStruct((t, d), F32),
        compiler_params=_params(1),
        name="na_out",
    )(x2d, gated, mods, w_out, ln_g.reshape(1, d), ln_b.reshape(1, d))


def kernel(x_prompt, x_sample, c, cache_k, cache_v, c_ctx, w_mod, b_mod, ln_g, ln_b, pool_w_in, pool_w_grp,
           pool_scale, pool_w_out, sgu_w_in, sgu_ln_g, sgu_ln_b, sgu_w_s, sgu_b_s, sgu_w_out, na_w_in, na_rpb,
           na_w_out):
    n_p, seq_p, d = x_prompt.shape
    n_s, seq_s, _ = x_sample.shape
    assert n_s + 1 <= N_COND_ROWS and (n_p * seq_p) % ROW_TILE == 0 and seq_s % ROW_TILE == 0
    assert seq_p % CHUNK == 0 and (seq_p % ROW_TILE == 0 or ROW_TILE % seq_p == 0) and seq_p % POOL_BLOCK == 0
    assert seq_s % (GRID_W * ATT_Q_ROWS) == 0 and seq_s // GRID_W >= ATT_K_ROWS
    conds = jnp.zeros((N_COND_ROWS, d), F32).at[0].set(c_ctx).at[1:1 + n_s].set(c)
    mods = _mods(conds, w_mod, b_mod)
    pool_w_fold = _pool_fold(pool_w_in, pool_w_grp)
    pool_band = _pool_band()

    yp = x_prompt.reshape(n_p * seq_p, d)
    ys = x_sample.reshape(n_s * seq_s, d)
    streams = ((seq_p, False), (seq_s, True))
    ctx_k = ctx_v = None
    for i in range(DEPTH):
        kind, j = i % N_MIXERS, i // N_MIXERS
        ys_in = (yp, ys)
        outs = []
        if kind == 0:
            for x2d, (seq, per_sample) in zip(ys_in, streams):
                outs.append(_pool_layer(x2d, seq, per_sample, mods, i, j, pool_band, pool_w_fold, pool_w_in,
                                        pool_scale, pool_w_out, ln_g[i], ln_b[i]))
        elif kind == 1:
            w_in, w_s, w_out = sgu_w_in[j].astype(BF16), sgu_w_s[j].astype(BF16), sgu_w_out[j].astype(BF16)
            for x2d, (seq, per_sample) in zip(ys_in, streams):
                outs.append(_sgu_layer(x2d, seq, per_sample, mods, i, w_in, sgu_ln_g[j], sgu_ln_b[j], w_s,
                                       sgu_b_s[j].T, w_out, ln_g[i], ln_b[i]))
        else:
            w_in, w_out = na_w_in[j], na_w_out[j]
            q, k, v, sg, kf, vf = _na_proj(yp, seq_p, False, mods, i, w_in, True, DH_C ** -0.5)
            ctx_k = kf.reshape(n_p, 1, H_C, DH_C, seq_p).transpose(0, 1, 4, 2, 3)
            ctx_v = vf.reshape(n_p, 1, H_C, DH_C, seq_p).transpose(0, 1, 4, 2, 3)
            outs.append(_ctx_attn_out(yp, q, k, v, sg, seq_p, mods, i, w_out, ln_g[i], ln_b[i]))
            q, k, v, sg = _na_proj(ys, seq_s, True, mods, i, w_in, False, DH_C ** -0.5 * LOG2_E)
            gated = _nbr_attn(q, k, v, sg, cache_k, cache_v, j, na_rpb[j], n_s)
            outs.append(_na_out(ys, gated, seq_s, True, mods, i, w_out, ln_g[i], ln_b[i]))
        yp, ys = outs
    return (yp.reshape(n_p, seq_p, d), ys.reshape(n_s, seq_s, d), ctx_k, ctx_v)
```

```python
import functools

import jax
import jax.numpy as jnp
import numpy as np
from jax import lax
from jax.experimental import pallas as pl
from jax.experimental.pallas import tpu as pltpu

F32 = jnp.float32
BF16 = jnp.bfloat16

DEPTH = 4
N_MIXERS = 3
POOL_WINDOWS = (2, 4, 8, 16)
POOL_HALO = 16
POOL_BLOCK = 128
CHUNK = 128
H_B = 8
H_C = 16
DH_C = 64
GRID_W = 64
WIN_R = 8
WIN_C = 16
DEEPNORM_ALPHA = (2 * DEPTH) ** 0.25
LN_EPS = 1e-5
LOG2_E = float(np.log2(np.e))

N_COND_ROWS = 8
ROW_TILE = 512
ATT_Q_ROWS = 4
ATT_K_ROWS = 12
LANES = 128
MXU_TILE = 256
VMEM_LIMIT = 56 * 1024 * 1024


def _const_spec(shape):
    nd = len(shape)
    return pl.BlockSpec(shape, lambda *_: (0,) * nd, pipeline_mode=pl.Buffered(1))


def _params(n_axes):
    return pltpu.CompilerParams(dimension_semantics=("arbitrary",) * n_axes, vmem_limit_bytes=VMEM_LIMIT)


def _mod_spec(layer, d, tiles_per_seq, per_sample):
    base = layer * N_COND_ROWS
    if per_sample:
        return pl.BlockSpec((1, 1, 3 * d), lambda i: (base + 1 + i // tiles_per_seq, 0, 0))
    return pl.BlockSpec((1, 1, 3 * d), lambda i: (base, 0, 0))


def _split_mod(mod_ref, d):
    m = mod_ref[0]
    return m[:, :d], m[:, d:2 * d], m[:, 2 * d:]


def _silu(x):
    hx = 0.5 * x
    return hx + hx * jnp.tanh(hx)


def _gelu_tanh_x2(x):
    c = np.float32(np.sqrt(2.0 / np.pi))
    return x * (1.0 + jnp.tanh(x * (c + (c * np.float32(0.044715)) * (x * x))))


def _layer_norm(x, g, b, eps=LN_EPS):
    mu = jnp.mean(x, axis=-1, keepdims=True)
    d = x - mu
    var = jnp.mean(d * d, axis=-1, keepdims=True)
    return d * lax.rsqrt(var + eps) * g + b


def _resid_layer_norm(x, gate, o, g, b):
    return _layer_norm(x + (gate * (1.0 / DEEPNORM_ALPHA)) * o, g, b, LN_EPS / DEEPNORM_ALPHA ** 2)


def _dot(a, b):
    return jnp.dot(a, b, preferred_element_type=F32)


def _dot_nt(a, b):
    return lax.dot_general(a, b, (((1,), (1,)), ((), ())), preferred_element_type=F32)


def _mods_kernel(cond_ref, w_ref, b_ref, o_ref):
    a = _silu(cond_ref[...]).astype(BF16)
    o_ref[0] = _dot(a, w_ref[0].astype(BF16)) + b_ref[0]


def _mods(conds, w_mod, b_mod):
    depth, d, n = w_mod.shape
    tn = n
    out = pl.pallas_call(
        _mods_kernel,
        grid=(depth, n // tn),
        in_specs=[
            pl.BlockSpec((N_COND_ROWS, d), lambda l, j: (0, 0)),
            pl.BlockSpec((1, d, tn), lambda l, j: (l, 0, j)),
            pl.BlockSpec((1, 1, tn), lambda l, j: (l, 0, j)),
        ],
        out_specs=pl.BlockSpec((1, N_COND_ROWS, tn), lambda l, j: (l, 0, j)),
        out_shape=jax.ShapeDtypeStruct((depth, N_COND_ROWS, n), F32),
        compiler_params=_params(2),
        name="adaln_mods",
    )(conds, w_mod, b_mod.reshape(depth, 1, n))
    return out.reshape(depth * N_COND_ROWS, 1, n)


def _pool_fold_kernel(win_x_ref, wgrp_ref, fold_ref):
    fold_ref[0] = _dot(win_x_ref[0].astype(BF16), wgrp_ref[0, 0].astype(BF16)).astype(BF16)


def _pool_fold(w_in, w_grp):
    n_layers, d, two_e = w_in.shape
    n_grp, grp = w_grp.shape[1:3]
    e_dim = two_e // 2
    return pl.pallas_call(
        _pool_fold_kernel,
        grid=(n_layers, n_grp),
        in_specs=[
            pl.BlockSpec((1, d, grp), lambda l, g: (l, 0, g)),
            pl.BlockSpec((1, 1, grp, grp), lambda l, g: (l, g, 0, 0)),
        ],
        out_specs=pl.BlockSpec((1, d, grp), lambda l, g: (l, 0, g)),
        out_shape=jax.ShapeDtypeStruct((n_layers, d, e_dim), BF16),
        compiler_params=_params(2),
        name="pool_fold",
    )(w_in, w_grp)


def _pool_band():
    t = np.arange(POOL_BLOCK)[:, None] + POOL_HALO
    e = np.arange(POOL_BLOCK + 2 * POOL_HALO)[None, :]
    return jnp.asarray(np.stack([(e >= t - w // 2) & (e <= t + w // 2 - 1) for w in POOL_WINDOWS]), BF16)


def _pool_kernel(seq_len, x_ref, xp_ref, xn_ref, mod_ref, band_ref, wfold_ref, wgate_ref, psc_ref, wout_ref,
                 g_ref, b_ref, o_ref, e_ref, u_ref):
    tm, d = x_ref.shape
    e_dim = psc_ref.shape[1]
    grp = e_dim // len(POOL_WINDOWS)
    seg = min(tm, seq_len)
    stride = seg + 2 * POOL_HALO
    shift, scale, gate = _split_mod(mod_ref, d)
    x = x_ref[...]
    hb = (x * (1.0 + scale) + shift).astype(BF16)
    zeros = jnp.zeros((POOL_HALO, d), BF16)
    seg_pos = lax.broadcasted_iota(jnp.int32, (seg, 1), 0)
    if seg == tm:
        first_pos = (pl.program_id(0) * tm) % seq_len
        pos = first_pos + seg_pos
        hp = (xp_ref[...] * (1.0 + scale) + shift).astype(BF16)
        hn = (xn_ref[...] * (1.0 + scale) + shift).astype(BF16)
        halos = [(jnp.where(first_pos != 0, hp, zeros), jnp.where(first_pos + tm != seq_len, hn, zeros))]
    else:
        pos = jnp.concatenate([seg_pos] * (tm // seg), axis=0)
        halos = [(zeros, zeros)] * (tm // seg)
    for sg, (before, after) in enumerate(halos):
        e_ref[sg * stride:sg * stride + POOL_HALO] = before
        e_ref[sg * stride + POOL_HALO:sg * stride + POOL_HALO + seg] = hb[sg * seg:(sg + 1) * seg]
        e_ref[sg * stride + POOL_HALO + seg:(sg + 1) * stride] = after
    u_ref[...] = _dot(e_ref[...], wfold_ref[...]).astype(BF16)

    mixed = []
    for gi, w in enumerate(POOL_WINDOWS):
        cols = slice(gi * grp, (gi + 1) * grp)
        lo = jnp.maximum(pos - w // 2, 0)
        hi = jnp.minimum(pos + w // 2 - 1, seq_len - 1)
        inv_cnt = 1.0 / (hi - lo + 1).astype(F32)
        sums, own = [], []
        for sg in range(len(halos)):
            for rb in range(seg // POOL_BLOCK):
                r0 = sg * stride + rb * POOL_BLOCK
                sums.append(_dot(band_ref[gi], u_ref[r0:r0 + POOL_BLOCK + 2 * POOL_HALO, cols]))
            own.append(u_ref[sg * stride + POOL_HALO:sg * stride + POOL_HALO + seg, cols])
        sums = jnp.concatenate(sums, axis=0)
        own = own[0] if len(own) == 1 else jnp.concatenate(own, axis=0)
        mixed.append(sums * inv_cnt - own.astype(F32))
    mixed = jnp.concatenate(mixed, axis=1)
    gate_pre = _dot(hb, wgate_ref[...].astype(BF16))
    y = (mixed * psc_ref[...] * _silu(gate_pre)).astype(BF16)
    o = _dot(y, wout_ref[...].astype(BF16))
    o_ref[...] = _resid_layer_norm(x, gate, o, g_ref[...], b_ref[...])


def _pool_layer(x2d, seq_len, per_sample, mods, layer, j, band, w_fold, w_in, p_scale, w_out, ln_g, ln_b):
    t, d = x2d.shape
    tm = ROW_TILE
    tps = max(seq_len // tm, 1)
    e_dim = w_out.shape[1]
    hb = tm // POOL_HALO
    last = t // POOL_HALO - 1
    n_seg = max(tm // seq_len, 1)
    e_rows = tm + 2 * POOL_HALO * n_seg

    def layer_spec(shape):
        return pl.BlockSpec((None,) + shape, lambda i: (j,) + (0,) * len(shape), pipeline_mode=pl.Buffered(1))

    return pl.pallas_call(
        functools.partial(_pool_kernel, seq_len),
        grid=(t // tm,),
        in_specs=[
            pl.BlockSpec((tm, d), lambda i: (i, 0)),
            pl.BlockSpec((POOL_HALO, d), lambda i: (jnp.maximum(i * hb - 1, 0), 0)),
            pl.BlockSpec((POOL_HALO, d), lambda i: (jnp.minimum((i + 1) * hb, last), 0)),
            _mod_spec(layer, d, tps, per_sample),
            _const_spec(band.shape),
            layer_spec((d, e_dim)),
            pl.BlockSpec((None, d, e_dim), lambda i: (j, 0, 1), pipeline_mode=pl.Buffered(1)),
            layer_spec((1, e_dim)),
            layer_spec((e_dim, d)),
            _const_spec((1, d)),
            _const_spec((1, d)),
        ],
        out_specs=pl.BlockSpec((tm, d), lambda i: (i, 0)),
        out_shape=jax.ShapeDtypeStruct((t, d), F32),
        scratch_shapes=[pltpu.VMEM((e_rows, d), BF16), pltpu.VMEM((e_rows, e_dim), BF16)],
        compiler_params=_params(1),
        name="pool_layer",
    )(x2d, x2d, x2d, mods, band, w_fold, w_in, p_scale.reshape(-1, 1, e_dim), w_out,
      ln_g.reshape(1, d), ln_b.reshape(1, d))


def _sgu_kernel(x_ref, mod_ref, win_ref, lg_ref, lb_ref, ws_ref, bs_ref, wout_ref, g_ref, b_ref, o_ref, y_ref):
    tm, d = x_ref.shape
    e_dim = lg_ref.shape[1]
    dh = e_dim // H_B
    shift, scale, gate = _split_mod(mod_ref, d)
    x = x_ref[...]
    hb = (x * (1.0 + scale) + shift).astype(BF16)
    v2 = _gelu_tanh_x2(_dot(hb, win_ref[:, e_dim:2 * e_dim]))
    v = _layer_norm(v2, lg_ref[...], lb_ref[...], 4.0 * LN_EPS).astype(BF16)
    u2 = _gelu_tanh_x2(_dot(hb, win_ref[:, 0:e_dim]))
    ug = u2 * _silu(_dot(hb, win_ref[:, 2 * e_dim:3 * e_dim]))
    for c in range(tm // CHUNK):
        rows = slice(c * CHUNK, (c + 1) * CHUNK)
        for hh in range(H_B):
            cols = slice(hh * dh, (hh + 1) * dh)
            sv = _dot(ws_ref[hh], v[rows, cols]) + bs_ref[:, hh:hh + 1]
            y_ref[rows, cols] = (ug[rows, cols] * sv).astype(BF16)
    o2 = _dot(y_ref[...], wout_ref[...])
    o_ref[...] = _resid_layer_norm(x, 0.5 * gate, o2, g_ref[...], b_ref[...])


def _sgu_layer(x2d, seq_len, per_sample, mods, layer, w_in, sln_g, sln_b, w_s, b_s_t, w_out, ln_g, ln_b):
    t, d = x2d.shape
    tm = ROW_TILE
    tps = max(seq_len // tm, 1)
    e_dim = w_out.shape[0]
    return pl.pallas_call(
        _sgu_kernel,
        grid=(t // tm,),
        in_specs=[
            pl.BlockSpec((tm, d), lambda i: (i, 0)),
            _mod_spec(layer, d, tps, per_sample),
            _const_spec(w_in.shape),
            _const_spec((1, e_dim)),
            _const_spec((1, e_dim)),
            _const_spec(w_s.shape),
            _const_spec(b_s_t.shape),
            _const_spec(w_out.shape),
            _const_spec((1, d)),
            _const_spec((1, d)),
        ],
        out_specs=pl.BlockSpec((tm, d), lambda i: (i, 0)),
        out_shape=jax.ShapeDtypeStruct((t, d), F32),
        scratch_shapes=[pltpu.VMEM((tm, e_dim), BF16)],
        compiler_params=_params(1),
        name="sgu_layer",
    )(x2d, mods, w_in, sln_g.reshape(1, e_dim), sln_b.reshape(1, e_dim), w_s, b_s_t, w_out,
      ln_g.reshape(1, d), ln_b.reshape(1, d))


def _na_proj_kernel(emit_f32_kv, q_scale, x_ref, mod_ref, win_ref, *out_refs):
    tm, d = x_ref.shape
    e = win_ref.shape[1] // 4
    shift, scale, _ = _split_mod(mod_ref, d)
    hb = (x_ref[...] * (1.0 + scale) + shift).astype(BF16)
    q_ref, k_ref, v_ref, sg_ref = out_refs[:4]
    q_ref[...] = (_dot(hb, win_ref[:, 0:e].astype(BF16)) * q_scale).astype(BF16)
    k = _dot(hb, win_ref[:, e:2 * e].astype(BF16))
    v = _dot(hb, win_ref[:, 2 * e:3 * e].astype(BF16))
    k_ref[...] = k.astype(BF16)
    v_ref[...] = v.astype(BF16)
    sg_ref[...] = _silu(_dot(hb, win_ref[:, 3 * e:4 * e].astype(BF16))).astype(BF16)
    if emit_f32_kv:
        seq = out_refs[4].shape[2]
        for kv, ref in ((k, out_refs[4]), (v, out_refs[5])):
            for sq in range(tm // seq):
                ref[sq] = kv[sq * seq:(sq + 1) * seq, :].T


def _na_proj(x2d, seq_len, per_sample, mods, layer, w_in, emit_f32_kv, q_scale):
    t, d = x2d.shape
    tm = ROW_TILE
    tps = max(seq_len // tm, 1)
    e = w_in.shape[1] // 4
    row_spec = pl.BlockSpec((tm, e), lambda i: (i, 0))
    out_specs = [row_spec] * 4
    out_shape = [jax.ShapeDtypeStruct((t, e), BF16)] * 4
    if emit_f32_kv:
        out_specs += [pl.BlockSpec((tm // seq_len, e, seq_len), lambda i: (i, 0, 0))] * 2
        out_shape += [jax.ShapeDtypeStruct((t // seq_len, e, seq_len), F32)] * 2
    return pl.pallas_call(
        functools.partial(_na_proj_kernel, emit_f32_kv, q_scale),
        grid=(t // tm,),
        in_specs=[
            pl.BlockSpec((tm, d), lambda i: (i, 0)),
            _mod_spec(layer, d, tps, per_sample),
            _const_spec(w_in.shape),
        ],
        out_specs=out_specs,
        out_shape=out_shape,
        compiler_params=_params(1),
        name="na_proj",
    )(x2d, mods, w_in)


def _lane_is_first_head():
    return lax.broadcasted_iota(jnp.int32, (1, LANES), 1) < DH_C


def _ctx_attn_kernel(seq_len, x_ref, q_ref, k_ref, v_ref, sg_ref, mod_ref, wout_ref, g_ref, b_ref, o_ref, y_ref):
    tm, d = x_ref.shape
    first = _lane_is_first_head()
    for sq in range(tm // seq_len):
        rows = slice(sq * seq_len, (sq + 1) * seq_len)
        for j in range(q_ref.shape[1] // LANES):
            cols = slice(j * LANES, (j + 1) * LANES)
            q, k, v = q_ref[rows, cols], k_ref[rows, cols], v_ref[rows, cols]
            outs = []
            for sel in (first, jnp.logical_not(first)):
                s = _dot_nt(jnp.where(sel, q, jnp.zeros_like(q)), k)
                p = jnp.exp(s - jnp.max(s, axis=-1, keepdims=True))
                l = jnp.sum(p, axis=-1, keepdims=True)
                outs.append(_dot(p.astype(BF16), v) / l)
            gated = jnp.where(first, outs[0], outs[1]) * sg_ref[rows, cols].astype(F32)
            y_ref[rows, cols] = gated.astype(BF16)
    _, _, gate = _split_mod(mod_ref, d)
    o = _dot(y_ref[...], wout_ref[...].astype(BF16))
    o_ref[...] = _resid_layer_norm(x_ref[...], gate, o, g_ref[...], b_ref[...])


def _ctx_attn_out(x2d, q, k, v, sg, seq_len, mods, layer, w_out, ln_g, ln_b):
    t, d = x2d.shape
    e = q.shape[1]
    tm = ROW_TILE
    row = pl.BlockSpec((tm, e), lambda i: (i, 0))
    return pl.pallas_call(
        functools.partial(_ctx_attn_kernel, seq_len),
        grid=(t // tm,),
        in_specs=[pl.BlockSpec((tm, d), lambda i: (i, 0)), row, row, row, row,
                  _mod_spec(layer, d, 1, False), _const_spec(w_out.shape), _const_spec((1, d)), _const_spec((1, d))],
        out_specs=pl.BlockSpec((tm, d), lambda i: (i, 0)),
        out_shape=jax.ShapeDtypeStruct((t, d), F32),
        scratch_shapes=[pltpu.VMEM((tm, e), BF16)],
        compiler_params=_params(1),
        name="ctx_attn_out",
    )(x2d, q, k, v, sg, mods, w_out, ln_g.reshape(1, d), ln_b.reshape(1, d))


def _key_row_start(r_blk, rows):
    return jnp.clip(r_blk * ATT_Q_ROWS - WIN_R // 2, 0, rows - ATT_K_ROWS)


def _nbr_bias_build(rows, rpb_ref, bias_ref):
    n_blk = rows // ATT_Q_ROWS
    kc = lax.broadcasted_iota(jnp.int32, (GRID_W, LANES), 0)
    lane = lax.broadcasted_iota(jnp.int32, (GRID_W, LANES), 1)
    qc = lane & (GRID_W - 1)
    c0 = jnp.clip(qc - WIN_C // 2, 0, GRID_W - WIN_C)
    col_in = (kc >= c0) & (kc < c0 + WIN_C)
    first = lane < GRID_W
    neg = jnp.full((GRID_W, LANES), -jnp.inf, F32)
    for hh in range(2):
        lo, hi = [], []
        for dr in range(2 * WIN_R - 1):
            row = jnp.broadcast_to(rpb_ref[hh, dr:dr + 1, :] * LOG2_E, (GRID_W, LANES))
            lo.append(pltpu.roll(row, LANES - (WIN_C - 1), 1, stride=1, stride_axis=0))
            hi.append(pltpu.roll(row, GRID_W - (WIN_C - 1), 1, stride=1, stride_axis=0))
        for var, r_blk in enumerate((0, 1, n_blk - 1)):
            ks = min(max(r_blk * ATT_Q_ROWS - WIN_R // 2, 0), rows - ATT_K_ROWS)
            for ki in range(ATT_K_ROWS):
                kr = ks + ki
                for qp in range(ATT_Q_ROWS // 2):
                    halves = []
                    for half, src in enumerate((lo, hi)):
                        r = r_blk * ATT_Q_ROWS + 2 * qp + half
                        r0 = min(max(r - WIN_R // 2, 0), rows - WIN_R)
                        halves.append(src[kr - r + WIN_R - 1] if r0 <= kr < r0 + WIN_R else neg)
                    blk = jnp.where(col_in, jnp.where(first, halves[0], halves[1]), neg)
                    bias_ref[hh, var, ki * GRID_W:(ki + 1) * GRID_W, qp * LANES:(qp + 1) * LANES] = blk


def _nbr_attn_kernel(rows, rpb_ref, q_ref, k_ref, v_ref, sg_ref, ck_ref, cv_ref, o_ref, bias_ref, kc_ref, vx_ref,
                     s_ref, p_ref, oa_ref):
    n_blk = rows // ATT_Q_ROWS
    tq = ATT_Q_ROWS * GRID_W
    nk = ATT_K_ROWS * GRID_W
    seq = q_ref.shape[0]
    n_tok_tiles = seq // MXU_TILE
    n_ctx_tiles = kc_ref.shape[0] // MXU_TILE

    @pl.when(pl.program_id(1) == 0)
    def _():
        _nbr_bias_build(rows, rpb_ref, bias_ref)

    first = _lane_is_first_head()
    head_lanes = (first, jnp.logical_not(first))
    kc_ref[...] = ck_ref[...].T.astype(BF16)
    v_t = v_ref[...].astype(F32).T
    top = lax.broadcasted_iota(jnp.int32, (LANES, 1), 0) < DH_C
    for hh, sel in enumerate((top, jnp.logical_not(top))):
        for kt in range(n_tok_tiles):
            vx_ref[hh, kt] = jnp.where(sel, v_t[:, kt * MXU_TILE:(kt + 1) * MXU_TILE], 1.0).astype(BF16)
        for kt in range(n_ctx_tiles):
            vx_ref[hh, n_tok_tiles + kt] = jnp.where(
                sel, cv_ref[:, kt * MXU_TILE:(kt + 1) * MXU_TILE], 1.0).astype(BF16)

    def q_rows(r):
        return pl.ds(pl.multiple_of(r * tq, tq), tq)

    def k_start(r):
        return pl.multiple_of(_key_row_start(r, rows) * GRID_W, MXU_TILE)

    def logits(r, hh):
        q = q_ref[q_rows(r), :]
        qh = jnp.where(head_lanes[hh], q, jnp.zeros_like(q))
        var = jnp.where(r == 0, 0, jnp.where(r == n_blk - 1, 2, 1))
        s_ref[hh, 0:nk, :] = _dot_nt(k_ref[pl.ds(k_start(r), nk), :], qh) + bias_ref[hh, var]
        s_ref[hh, nk:, :] = _dot_nt(kc_ref[...], qh)

    def softmax(hh):
        s = s_ref[hh]
        p_ref[hh] = jnp.exp2((s - jnp.max(s, axis=0, keepdims=True)).astype(BF16))

    def weighted_values(r, hh):
        kt0 = k_start(r) // MXU_TILE
        tiles = [vx_ref[hh, kt0 + i] for i in range(nk // MXU_TILE)]
        tiles += [vx_ref[hh, n_tok_tiles + i] for i in range(n_ctx_tiles)]
        return _dot(jnp.concatenate(tiles, axis=1), p_ref[hh])

    def emit(r, o_second):
        o_first = oa_ref[...]
        num = jnp.concatenate([o_first[0:DH_C], o_second[DH_C:]], axis=0)
        den = jnp.concatenate([o_first[DH_C:], o_second[0:DH_C]], axis=0)
        o_ref[q_rows(r), :] = ((num / den).T * sg_ref[q_rows(r), :].astype(F32)).astype(BF16)

    p_ref[1] = jnp.ones(p_ref.shape[1:], BF16)
    oa_ref[...] = jnp.ones(oa_ref.shape, F32)
    logits(0, 0)

    def block(r, carry):
        prev = jnp.maximum(r - 1, 0)
        emit(prev, weighted_values(prev, 1))
        logits(r, 1)
        softmax(0)
        oa_ref[...] = weighted_values(r, 0)
        logits(jnp.minimum(r + 1, n_blk - 1), 0)
        softmax(1)
        return carry

    lax.fori_loop(0, n_blk, block, 0, unroll=2)
    emit(n_blk - 1, weighted_values(n_blk - 1, 1))


def _nbr_attn(q, k, v, sg, cache_k, cache_v, cache_layer, rpb, n_batch):
    t, e = q.shape
    seq = t // n_batch
    rows = seq // GRID_W
    n_cached, past = cache_k.shape[1:3]
    ck = jnp.transpose(cache_k, (0, 1, 3, 4, 2)).reshape(n_batch * n_cached, e, past)
    cv = jnp.transpose(cache_v, (0, 1, 3, 4, 2)).reshape(n_batch * n_cached, e, past)
    tq, nk = ATT_Q_ROWS * GRID_W, ATT_K_ROWS * GRID_W
    assert nk % MXU_TILE == 0 and past % MXU_TILE == 0 and (ATT_Q_ROWS * GRID_W) % MXU_TILE == 0
    n_dr, n_dc = rpb.shape[1:]
    rpb_pad = jnp.pad(rpb.astype(F32)[:, :, ::-1], ((0, 0), (0, 2 * WIN_R - n_dr), (0, LANES - n_dc)))
    tok_spec = pl.BlockSpec((seq, LANES), lambda j, b: (b, j))
    ctx_spec = pl.BlockSpec((None, LANES, past), lambda j, b: (b * n_cached + cache_layer, j, 0))
    return pl.pallas_call(
        functools.partial(_nbr_attn_kernel, rows),
        grid=(e // LANES, n_batch),
        in_specs=[pl.BlockSpec((2, 2 * WIN_R, LANES), lambda j, b: (j, 0, 0)),
                  tok_spec, tok_spec, tok_spec, tok_spec, ctx_spec, ctx_spec],
        out_specs=tok_spec,
        out_shape=jax.ShapeDtypeStruct((t, e), BF16),
        scratch_shapes=[
            pltpu.VMEM((2, 3, nk, tq), F32),
            pltpu.VMEM((past, LANES), BF16),
            pltpu.VMEM((2, (seq + past) // MXU_TILE, LANES, MXU_TILE), BF16),
            pltpu.VMEM((2, nk + past, tq), F32),
            pltpu.VMEM((2, nk + past, tq), BF16),
            pltpu.VMEM((LANES, tq), F32),
        ],
        compiler_params=_params(2),
        name="nbr_attn",
    )(rpb_pad, q, k, v, sg, ck, cv)


def _na_out_kernel(x_ref, y_ref, mod_ref, wout_ref, g_ref, b_ref, o_ref):
    tm, d = x_ref.shape
    _, _, gate = _split_mod(mod_ref, d)
    o = _dot(y_ref[...], wout_ref[...].astype(BF16))
    o_ref[...] = _resid_layer_norm(x_ref[...], gate, o, g_ref[...], b_ref[...])


def _na_out(x2d, gated, seq_len, per_sample, mods, layer, w_out, ln_g, ln_b):
    t, d = x2d.shape
    tm = ROW_TILE
    tps = max(seq_len // tm, 1)
    e = w_out.shape[0]
    return pl.pallas_call(
        _na_out_kernel,
        grid=(t // tm,),
        in_specs=[
            pl.BlockSpec((tm, d), lambda i: (i, 0)),
            pl.BlockSpec((tm, e), lambda i: (i, 0)),
            _mod_spec(layer, d, tps, per_sample),
            _const_spec(w_out.shape),
            _const_spec((1, d)),
            _const_spec((1, d)),
        ],
        out_specs=pl.BlockSpec((tm, d), lambda i: (i, 0)),
        out_shape=jax.ShapeDtypeStruct((t, d), F32),
        compiler_params=_params(1),
        name="na_out",
    )(x2d, gated, mods, w_out, ln_g.reshape(1, d), ln_b.reshape(1, d))


def kernel(x_prompt, x_sample, c, cache_k, cache_v, c_ctx, w_mod, b_mod, ln_g, ln_b, pool_w_in, pool_w_grp,
           pool_scale, pool_w_out, sgu_w_in, sgu_ln_g, sgu_ln_b, sgu_w_s, sgu_b_s, sgu_w_out, na_w_in, na_rpb,
           na_w_out):
    n_p, seq_p, d = x_prompt.shape
    n_s, seq_s, _ = x_sample.shape
    assert n_s + 1 <= N_COND_ROWS and (n_p * seq_p) % ROW_TILE == 0 and seq_s % ROW_TILE == 0
    assert seq_p % CHUNK == 0 and (seq_p % ROW_TILE == 0 or ROW_TILE % seq_p == 0) and seq_p % POOL_BLOCK == 0
    assert seq_s % (GRID_W * ATT_Q_ROWS) == 0 and seq_s // GRID_W >= ATT_K_ROWS
    conds = jnp.zeros((N_COND_ROWS, d), F32).at[0].set(c_ctx).at[1:1 + n_s].set(c)
    mods = _mods(conds, w_mod, b_mod)
    pool_w_fold = _pool_fold(pool_w_in, pool_w_grp)
    pool_band = _pool_band()

    yp = x_prompt.reshape(n_p * seq_p, d)
    ys = x_sample.reshape(n_s * seq_s, d)
    streams = ((seq_p, False), (seq_s, True))
    ctx_k = ctx_v = None
    for i in range(DEPTH):
        kind, j = i % N_MIXERS, i // N_MIXERS
        ys_in = (yp, ys)
        outs = []
        if kind == 0:
            for x2d, (seq, per_sample) in zip(ys_in, streams):
                outs.append(_pool_layer(x2d, seq, per_sample, mods, i, j, pool_band, pool_w_fold, pool_w_in,
                                        pool_scale, pool_w_out, ln_g[i], ln_b[i]))
        elif kind == 1:
            w_in, w_s, w_out = sgu_w_in[j].astype(BF16), sgu_w_s[j].astype(BF16), sgu_w_out[j].astype(BF16)
            for x2d, (seq, per_sample) in zip(ys_in, streams):
                outs.append(_sgu_layer(x2d, seq, per_sample, mods, i, w_in, sgu_ln_g[j], sgu_ln_b[j], w_s,
                                       sgu_b_s[j].T, w_out, ln_g[i], ln_b[i]))
        else:
            w_in, w_out = na_w_in[j], na_w_out[j]
            q, k, v, sg, kf, vf = _na_proj(yp, seq_p, False, mods, i, w_in, True, DH_C ** -0.5)
            ctx_k = kf.reshape(n_p, 1, H_C, DH_C, seq_p).transpose(0, 1, 4, 2, 3)
            ctx_v = vf.reshape(n_p, 1, H_C, DH_C, seq_p).transpose(0, 1, 4, 2, 3)
            outs.append(_ctx_attn_out(yp, q, k, v, sg, seq_p, mods, i, w_out, ln_g[i], ln_b[i]))
            q, k, v, sg = _na_proj(ys, seq_s, True, mods, i, w_in, False, DH_C ** -0.5 * LOG2_E)
            gated = _nbr_attn(q, k, v, sg, cache_k, cache_v, j, na_rpb[j], n_s)
            outs.append(_na_out(ys, gated, seq_s, True, mods, i, w_out, ln_g[i], ln_b[i]))
        yp, ys = outs
    return (yp.reshape(n_p, seq_p, d), ys.reshape(n_s, seq_s, d), ctx_k, ctx_v)
```

```python
import functools

import jax
import jax.numpy as jnp
import numpy as np
from jax import lax
from jax.experimental import pallas as pl
from jax.experimental.pallas import tpu as pltpu

F32 = jnp.float32
BF16 = jnp.bfloat16

DEPTH = 4
N_MIXERS = 3
POOL_WINDOWS = (2, 4, 8, 16)
POOL_HALO = 16
POOL_BLOCK = 128
CHUNK = 128
H_B = 8
H_C = 16
DH_C = 64
GRID_W = 64
WIN_R = 8
WIN_C = 16
DEEPNORM_ALPHA = (2 * DEPTH) ** 0.25
LN_EPS = 1e-5
LOG2_E = float(np.log2(np.e))

N_COND_ROWS = 8
ROW_TILE = 512
ATT_Q_ROWS = 4
ATT_K_ROWS = 12
LANES = 128
MXU_TILE = 256
VMEM_LIMIT = 56 * 1024 * 1024


def _const_spec(shape):
    nd = len(shape)
    return pl.BlockSpec(shape, lambda *_: (0,) * nd, pipeline_mode=pl.Buffered(1))


def _params(n_axes):
    return pltpu.CompilerParams(dimension_semantics=("arbitrary",) * n_axes, vmem_limit_bytes=VMEM_LIMIT)


def _mod_spec(layer, d, tiles_per_seq, per_sample):
    base = layer * N_COND_ROWS
    if per_sample:
        return pl.BlockSpec((1, 1, 3 * d), lambda i: (base + 1 + i // tiles_per_seq, 0, 0))
    return pl.BlockSpec((1, 1, 3 * d), lambda i: (base, 0, 0))


def _split_mod(mod_ref, d):
    m = mod_ref[0]
    return m[:, :d], m[:, d:2 * d], m[:, 2 * d:]


def _silu(x):
    hx = 0.5 * x
    return hx + hx * jnp.tanh(hx)


def _gelu_tanh_x2(x):
    c = np.float32(np.sqrt(2.0 / np.pi))
    return x * (1.0 + jnp.tanh(x * (c + (c * np.float32(0.044715)) * (x * x))))


def _layer_norm(x, g, b, eps=LN_EPS):
    mu = jnp.mean(x, axis=-1, keepdims=True)
    d = x - mu
    var = jnp.mean(d * d, axis=-1, keepdims=True)
    return d * lax.rsqrt(var + eps) * g + b


def _resid_layer_norm(x, gate, o, g, b):
    return _layer_norm(x + (gate * (1.0 / DEEPNORM_ALPHA)) * o, g, b, LN_EPS / DEEPNORM_ALPHA ** 2)


def _dot(a, b):
    return jnp.dot(a, b, preferred_element_type=F32)


def _dot_nt(a, b):
    return lax.dot_general(a, b, (((1,), (1,)), ((), ())), preferred_element_type=F32)


def _mods_kernel(cond_ref, w_ref, b_ref, o_ref):
    a = _silu(cond_ref[...]).astype(BF16)
    o_ref[0] = _dot(a, w_ref[0].astype(BF16)) + b_ref[0]


def _mods(conds, w_mod, b_mod):
    depth, d, n = w_mod.shape
    tn = n
    out = pl.pallas_call(
        _mods_kernel,
        grid=(depth, n // tn),
        in_specs=[
            pl.BlockSpec((N_COND_ROWS, d), lambda l, j: (0, 0)),
            pl.BlockSpec((1, d, tn), lambda l, j: (l, 0, j)),
            pl.BlockSpec((1, 1, tn), lambda l, j: (l, 0, j)),
        ],
        out_specs=pl.BlockSpec((1, N_COND_ROWS, tn), lambda l, j: (l, 0, j)),
        out_shape=jax.ShapeDtypeStruct((depth, N_COND_ROWS, n), F32),
        compiler_params=_params(2),
        name="adaln_mods",
    )(conds, w_mod, b_mod.reshape(depth, 1, n))
    return out.reshape(depth * N_COND_ROWS, 1, n)


def _pool_fold_kernel(win_x_ref, wgrp_ref, fold_ref):
    fold_ref[0] = _dot(win_x_ref[0].astype(BF16), wgrp_ref[0, 0].astype(BF16)).astype(BF16)


def _pool_fold(w_in, w_grp):
    n_layers, d, two_e = w_in.shape
    n_grp, grp = w_grp.shape[1:3]
    e_dim = two_e // 2
    return pl.pallas_call(
        _pool_fold_kernel,
        grid=(n_layers, n_grp),
        in_specs=[
            pl.BlockSpec((1, d, grp), lambda l, g: (l, 0, g)),
            pl.BlockSpec((1, 1, grp, grp), lambda l, g: (l, g, 0, 0)),
        ],
        out_specs=pl.BlockSpec((1, d, grp), lambda l, g: (l, 0, g)),
        out_shape=jax.ShapeDtypeStruct((n_layers, d, e_dim), BF16),
        compiler_params=_params(2),
        name="pool_fold",
    )(w_in, w_grp)


def _pool_band():
    t = np.arange(POOL_BLOCK)[:, None] + POOL_HALO
    e = np.arange(POOL_BLOCK + 2 * POOL_HALO)[None, :]
    return jnp.asarray(np.stack([(e >= t - w // 2) & (e <= t + w // 2 - 1) for w in POOL_WINDOWS]), BF16)


def _pool_kernel(seq_len, fuse_prev, x_ref, xp_ref, xn_ref, *refs):
    tm, d = x_ref.shape
    if fuse_prev:
        (y_ref, yp_ref, yn_ref, pmod_ref, pw_ref, pg_ref, pb_ref, mod_ref, band_ref, wfold_ref, wgate_ref, psc_ref,
         wout_ref, g_ref, b_ref, o_ref, e_ref, u_ref, xe_ref, ye_ref) = refs
        for ext, parts in ((xe_ref, (xp_ref, x_ref, xn_ref)), (ye_ref, (yp_ref, y_ref, yn_ref))):
            ext[0:POOL_HALO] = parts[0][...]
            ext[POOL_HALO:POOL_HALO + tm] = parts[1][...]
            ext[POOL_HALO + tm:] = parts[2][...]
        _, _, prev_gate = _split_mod(pmod_ref, d)
        o_prev = _dot(ye_ref[...], pw_ref[...].astype(BF16))
        x_ext = _resid_layer_norm(xe_ref[...], prev_gate, o_prev, pg_ref[...], pb_ref[...])
        x_before, x, x_after = x_ext[0:POOL_HALO], x_ext[POOL_HALO:POOL_HALO + tm], x_ext[POOL_HALO + tm:]
    else:
        mod_ref, band_ref, wfold_ref, wgate_ref, psc_ref, wout_ref, g_ref, b_ref, o_ref, e_ref, u_ref = refs
        x_before, x, x_after = xp_ref[...], x_ref[...], xn_ref[...]
    e_dim = psc_ref.shape[1]
    grp = e_dim // len(POOL_WINDOWS)
    seg = min(tm, seq_len)
    stride = seg + 2 * POOL_HALO
    shift, scale, gate = _split_mod(mod_ref, d)
    hb = (x * (1.0 + scale) + shift).astype(BF16)
    zeros = jnp.zeros((POOL_HALO, d), BF16)
    seg_pos = lax.broadcasted_iota(jnp.int32, (seg, 1), 0)
    if seg == tm:
        first_pos = (pl.program_id(0) * tm) % seq_len
        pos = first_pos + seg_pos
        hp = (x_before * (1.0 + scale) + shift).astype(BF16)
        hn = (x_after * (1.0 + scale) + shift).astype(BF16)
        halos = [(jnp.where(first_pos != 0, hp, zeros), jnp.where(first_pos + tm != seq_len, hn, zeros))]
    else:
        pos = jnp.concatenate([seg_pos] * (tm // seg), axis=0)
        halos = [(zeros, zeros)] * (tm // seg)
    for sg, (before, after) in enumerate(halos):
        e_ref[sg * stride:sg * stride + POOL_HALO] = before
        e_ref[sg * stride + POOL_HALO:sg * stride + POOL_HALO + seg] = hb[sg * seg:(sg + 1) * seg]
        e_ref[sg * stride + POOL_HALO + seg:(sg + 1) * stride] = after
    u_ref[...] = _dot(e_ref[...], wfold_ref[...]).astype(BF16)

    mixed = []
    for gi, w in enumerate(POOL_WINDOWS):
        cols = slice(gi * grp, (gi + 1) * grp)
        lo = jnp.maximum(pos - w // 2, 0)
        hi = jnp.minimum(pos + w // 2 - 1, seq_len - 1)
        inv_cnt = 1.0 / (hi - lo + 1).astype(F32)
        sums, own = [], []
        for sg in range(len(halos)):
            for rb in range(seg // POOL_BLOCK):
                r0 = sg * stride + rb * POOL_BLOCK
                sums.append(_dot(band_ref[gi], u_ref[r0:r0 + POOL_BLOCK + 2 * POOL_HALO, cols]))
            own.append(u_ref[sg * stride + POOL_HALO:sg * stride + POOL_HALO + seg, cols])
        sums = jnp.concatenate(sums, axis=0)
        own = own[0] if len(own) == 1 else jnp.concatenate(own, axis=0)
        mixed.append(sums * inv_cnt - own.astype(F32))
    mixed = jnp.concatenate(mixed, axis=1)
    gate_pre = _dot(hb, wgate_ref[...].astype(BF16))
    y = (mixed * psc_ref[...] * _silu(gate_pre)).astype(BF16)
    o = _dot(y, wout_ref[...].astype(BF16))
    o_ref[...] = _resid_layer_norm(x, gate, o, g_ref[...], b_ref[...])


def _pool_layer(x2d, seq_len, per_sample, mods, layer, j, band, w_fold, w_in, p_scale, w_out, ln_g, ln_b, prev=None):
    t, d = x2d.shape
    tm = ROW_TILE
    tps = max(seq_len // tm, 1)
    e_dim = w_out.shape[1]
    hb = tm // POOL_HALO
    last = t // POOL_HALO - 1
    n_seg = max(tm // seq_len, 1)
    e_rows = tm + 2 * POOL_HALO * n_seg

    def layer_spec(shape):
        return pl.BlockSpec((None,) + shape, lambda i: (j,) + (0,) * len(shape), pipeline_mode=pl.Buffered(1))

    def halo_specs(width):
        return [pl.BlockSpec((tm, width), lambda i: (i, 0)),
                pl.BlockSpec((POOL_HALO, width), lambda i: (jnp.maximum(i * hb - 1, 0), 0)),
                pl.BlockSpec((POOL_HALO, width), lambda i: (jnp.minimum((i + 1) * hb, last), 0))]

    prev_specs, prev_args, prev_scratch = [], [], []
    if prev is not None:
        assert n_seg == 1
        y2d, prev_layer, prev_w, prev_g, prev_b = prev
        e_prev = y2d.shape[1]
        prev_specs = halo_specs(e_prev) + [_mod_spec(prev_layer, d, tps, per_sample), _const_spec(prev_w.shape),
                                           _const_spec((1, d)), _const_spec((1, d))]
        prev_args = [y2d, y2d, y2d, mods, prev_w, prev_g.reshape(1, d), prev_b.reshape(1, d)]
        prev_scratch = [pltpu.VMEM((e_rows, d), F32), pltpu.VMEM((e_rows, e_prev), BF16)]

    return pl.pallas_call(
        functools.partial(_pool_kernel, seq_len, prev is not None),
        grid=(t // tm,),
        in_specs=halo_specs(d) + prev_specs + [
            _mod_spec(layer, d, tps, per_sample),
            _const_spec(band.shape),
            layer_spec((d, e_dim)),
            pl.BlockSpec((None, d, e_dim), lambda i: (j, 0, 1), pipeline_mode=pl.Buffered(1)),
            layer_spec((1, e_dim)),
            layer_spec((e_dim, d)),
            _const_spec((1, d)),
            _const_spec((1, d)),
        ],
        out_specs=pl.BlockSpec((tm, d), lambda i: (i, 0)),
        out_shape=jax.ShapeDtypeStruct((t, d), F32),
        scratch_shapes=[pltpu.VMEM((e_rows, d), BF16), pltpu.VMEM((e_rows, e_dim), BF16)] + prev_scratch,
        compiler_params=_params(1),
        name="pool_layer",
    )(x2d, x2d, x2d, *prev_args, mods, band, w_fold, w_in, p_scale.reshape(-1, 1, e_dim), w_out,
      ln_g.reshape(1, d), ln_b.reshape(1, d))


def _sgu_kernel(x_ref, mod_ref, win_ref, lg_ref, lb_ref, ws_ref, bs_ref, wout_ref, g_ref, b_ref, o_ref, y_ref):
    tm, d = x_ref.shape
    e_dim = lg_ref.shape[1]
    dh = e_dim // H_B
    shift, scale, gate = _split_mod(mod_ref, d)
    x = x_ref[...]
    hb = (x * (1.0 + scale) + shift).astype(BF16)
    v2 = _gelu_tanh_x2(_dot(hb, win_ref[:, e_dim:2 * e_dim]))
    v = _layer_norm(v2, lg_ref[...], lb_ref[...], 4.0 * LN_EPS).astype(BF16)
    u2 = _gelu_tanh_x2(_dot(hb, win_ref[:, 0:e_dim]))
    ug = u2 * _silu(_dot(hb, win_ref[:, 2 * e_dim:3 * e_dim]))
    for c in range(tm // CHUNK):
        rows = slice(c * CHUNK, (c + 1) * CHUNK)
        for hh in range(H_B):
            cols = slice(hh * dh, (hh + 1) * dh)
            sv = _dot(ws_ref[hh], v[rows, cols]) + bs_ref[:, hh:hh + 1]
            y_ref[rows, cols] = (ug[rows, cols] * sv).astype(BF16)
    o2 = _dot(y_ref[...], wout_ref[...])
    o_ref[...] = _resid_layer_norm(x, 0.5 * gate, o2, g_ref[...], b_ref[...])


def _sgu_layer(x2d, seq_len, per_sample, mods, layer, w_in, sln_g, sln_b, w_s, b_s_t, w_out, ln_g, ln_b):
    t, d = x2d.shape
    tm = ROW_TILE
    tps = max(seq_len // tm, 1)
    e_dim = w_out.shape[0]
    return pl.pallas_call(
        _sgu_kernel,
        grid=(t // tm,),
        in_specs=[
            pl.BlockSpec((tm, d), lambda i: (i, 0)),
            _mod_spec(layer, d, tps, per_sample),
            _const_spec(w_in.shape),
            _const_spec((1, e_dim)),
            _const_spec((1, e_dim)),
            _const_spec(w_s.shape),
            _const_spec(b_s_t.shape),
            _const_spec(w_out.shape),
            _const_spec((1, d)),
            _const_spec((1, d)),
        ],
        out_specs=pl.BlockSpec((tm, d), lambda i: (i, 0)),
        out_shape=jax.ShapeDtypeStruct((t, d), F32),
        scratch_shapes=[pltpu.VMEM((tm, e_dim), BF16)],
        compiler_params=_params(1),
        name="sgu_layer",
    )(x2d, mods, w_in, sln_g.reshape(1, e_dim), sln_b.reshape(1, e_dim), w_s, b_s_t, w_out,
      ln_g.reshape(1, d), ln_b.reshape(1, d))


def _na_proj_kernel(emit_f32_kv, q_scale, x_ref, mod_ref, win_ref, *out_refs):
    tm, d = x_ref.shape
    e = win_ref.shape[1] // 4
    shift, scale, _ = _split_mod(mod_ref, d)
    hb = (x_ref[...] * (1.0 + scale) + shift).astype(BF16)
    q_ref, k_ref, v_ref, sg_ref = out_refs[:4]
    q_ref[...] = (_dot(hb, win_ref[:, 0:e].astype(BF16)) * q_scale).astype(BF16)
    k = _dot(hb, win_ref[:, e:2 * e].astype(BF16))
    v = _dot(hb, win_ref[:, 2 * e:3 * e].astype(BF16))
    k_ref[...] = k.astype(BF16)
    v_ref[...] = v.astype(BF16)
    sg_ref[...] = _silu(_dot(hb, win_ref[:, 3 * e:4 * e].astype(BF16))).astype(BF16)
    if emit_f32_kv:
        seq = out_refs[4].shape[2]
        for kv, ref in ((k, out_refs[4]), (v, out_refs[5])):
            for sq in range(tm // seq):
                ref[sq] = kv[sq * seq:(sq + 1) * seq, :].T


def _na_proj(x2d, seq_len, per_sample, mods, layer, w_in, emit_f32_kv, q_scale):
    t, d = x2d.shape
    tm = ROW_TILE
    tps = max(seq_len // tm, 1)
    e = w_in.shape[1] // 4
    row_spec = pl.BlockSpec((tm, e), lambda i: (i, 0))
    out_specs = [row_spec] * 4
    out_shape = [jax.ShapeDtypeStruct((t, e), BF16)] * 4
    if emit_f32_kv:
        out_specs += [pl.BlockSpec((tm // seq_len, e, seq_len), lambda i: (i, 0, 0))] * 2
        out_shape += [jax.ShapeDtypeStruct((t // seq_len, e, seq_len), F32)] * 2
    return pl.pallas_call(
        functools.partial(_na_proj_kernel, emit_f32_kv, q_scale),
        grid=(t // tm,),
        in_specs=[
            pl.BlockSpec((tm, d), lambda i: (i, 0)),
            _mod_spec(layer, d, tps, per_sample),
            _const_spec(w_in.shape),
        ],
        out_specs=out_specs,
        out_shape=out_shape,
        compiler_params=_params(1),
        name="na_proj",
    )(x2d, mods, w_in)


def _lane_is_first_head():
    return lax.broadcasted_iota(jnp.int32, (1, LANES), 1) < DH_C


def _ctx_attn_kernel(seq_len, x_ref, q_ref, k_ref, v_ref, sg_ref, mod_ref, wout_ref, g_ref, b_ref, o_ref, y_ref):
    tm, d = x_ref.shape
    first = _lane_is_first_head()
    for sq in range(tm // seq_len):
        rows = slice(sq * seq_len, (sq + 1) * seq_len)
        for j in range(q_ref.shape[1] // LANES):
            cols = slice(j * LANES, (j + 1) * LANES)
            q, k, v = q_ref[rows, cols], k_ref[rows, cols], v_ref[rows, cols]
            outs = []
            for sel in (first, jnp.logical_not(first)):
                s = _dot_nt(jnp.where(sel, q, jnp.zeros_like(q)), k)
                p = jnp.exp(s - jnp.max(s, axis=-1, keepdims=True))
                l = jnp.sum(p, axis=-1, keepdims=True)
                outs.append(_dot(p.astype(BF16), v) / l)
            gated = jnp.where(first, outs[0], outs[1]) * sg_ref[rows, cols].astype(F32)
            y_ref[rows, cols] = gated.astype(BF16)
    _, _, gate = _split_mod(mod_ref, d)
    o = _dot(y_ref[...], wout_ref[...].astype(BF16))
    o_ref[...] = _resid_layer_norm(x_ref[...], gate, o, g_ref[...], b_ref[...])


def _ctx_attn_out(x2d, q, k, v, sg, seq_len, mods, layer, w_out, ln_g, ln_b):
    t, d = x2d.shape
    e = q.shape[1]
    tm = ROW_TILE
    row = pl.BlockSpec((tm, e), lambda i: (i, 0))
    return pl.pallas_call(
        functools.partial(_ctx_attn_kernel, seq_len),
        grid=(t // tm,),
        in_specs=[pl.BlockSpec((tm, d), lambda i: (i, 0)), row, row, row, row,
                  _mod_spec(layer, d, 1, False), _const_spec(w_out.shape), _const_spec((1, d)), _const_spec((1, d))],
        out_specs=pl.BlockSpec((tm, d), lambda i: (i, 0)),
        out_shape=jax.ShapeDtypeStruct((t, d), F32),
        scratch_shapes=[pltpu.VMEM((tm, e), BF16)],
        compiler_params=_params(1),
        name="ctx_attn_out",
    )(x2d, q, k, v, sg, mods, w_out, ln_g.reshape(1, d), ln_b.reshape(1, d))


def _key_row_start(r_blk, rows):
    return jnp.clip(r_blk * ATT_Q_ROWS - WIN_R // 2, 0, rows - ATT_K_ROWS)


def _nbr_bias_build(rows, rpb_ref, bias_ref):
    n_blk = rows // ATT_Q_ROWS
    kc = lax.broadcasted_iota(jnp.int32, (GRID_W, LANES), 0)
    lane = lax.broadcasted_iota(jnp.int32, (GRID_W, LANES), 1)
    qc = lane & (GRID_W - 1)
    c0 = jnp.clip(qc - WIN_C // 2, 0, GRID_W - WIN_C)
    col_in = (kc >= c0) & (kc < c0 + WIN_C)
    first = lane < GRID_W
    neg = jnp.full((GRID_W, LANES), -jnp.inf, F32)
    for hh in range(2):
        lo, hi = [], []
        for dr in range(2 * WIN_R - 1):
            row = jnp.broadcast_to(rpb_ref[hh, dr:dr + 1, :] * LOG2_E, (GRID_W, LANES))
            lo.append(pltpu.roll(row, LANES - (WIN_C - 1), 1, stride=1, stride_axis=0))
            hi.append(pltpu.roll(row, GRID_W - (WIN_C - 1), 1, stride=1, stride_axis=0))
        for var, r_blk in enumerate((0, 1, n_blk - 1)):
            ks = min(max(r_blk * ATT_Q_ROWS - WIN_R // 2, 0), rows - ATT_K_ROWS)
            for ki in range(ATT_K_ROWS):
                kr = ks + ki
                for qp in range(ATT_Q_ROWS // 2):
                    halves = []
                    for half, src in enumerate((lo, hi)):
                        r = r_blk * ATT_Q_ROWS + 2 * qp + half
                        r0 = min(max(r - WIN_R // 2, 0), rows - WIN_R)
                        halves.append(src[kr - r + WIN_R - 1] if r0 <= kr < r0 + WIN_R else neg)
                    blk = jnp.where(col_in, jnp.where(first, halves[0], halves[1]), neg)
                    bias_ref[hh, var, ki * GRID_W:(ki + 1) * GRID_W, qp * LANES:(qp + 1) * LANES] = blk


def _nbr_attn_kernel(rows, rpb_ref, q_ref, k_ref, v_ref, sg_ref, ck_ref, cv_ref, o_ref, bias_ref, kc_ref, vx_ref,
                     s_ref, p_ref, oa_ref):
    n_blk = rows // ATT_Q_ROWS
    tq = ATT_Q_ROWS * GRID_W
    nk = ATT_K_ROWS * GRID_W
    seq = q_ref.shape[0]
    n_tok_tiles = seq // MXU_TILE
    n_ctx_tiles = kc_ref.shape[0] // MXU_TILE

    @pl.when(pl.program_id(1) == 0)
    def _():
        _nbr_bias_build(rows, rpb_ref, bias_ref)

    first = _lane_is_first_head()
    head_lanes = (first, jnp.logical_not(first))
    kc_ref[...] = ck_ref[...].T.astype(BF16)
    v_t = v_ref[...].astype(F32).T
    top = lax.broadcasted_iota(jnp.int32, (LANES, 1), 0) < DH_C
    for hh, sel in enumerate((top, jnp.logical_not(top))):
        for kt in range(n_tok_tiles):
            vx_ref[hh, kt] = jnp.where(sel, v_t[:, kt * MXU_TILE:(kt + 1) * MXU_TILE], 1.0).astype(BF16)
        for kt in range(n_ctx_tiles):
            vx_ref[hh, n_tok_tiles + kt] = jnp.where(
                sel, cv_ref[:, kt * MXU_TILE:(kt + 1) * MXU_TILE], 1.0).astype(BF16)

    def q_rows(r):
        return pl.ds(pl.multiple_of(r * tq, tq), tq)

    def k_start(r):
        return pl.multiple_of(_key_row_start(r, rows) * GRID_W, MXU_TILE)

    def logits(r, hh):
        q = q_ref[q_rows(r), :]
        qh = jnp.where(head_lanes[hh], q, jnp.zeros_like(q))
        var = jnp.where(r == 0, 0, jnp.where(r == n_blk - 1, 2, 1))
        s_ref[hh, 0:nk, :] = _dot_nt(k_ref[pl.ds(k_start(r), nk), :], qh) + bias_ref[hh, var]
        s_ref[hh, nk:, :] = _dot_nt(kc_ref[...], qh)

    def softmax(hh):
        s = s_ref[hh]
        p_ref[hh] = jnp.exp2((s - jnp.max(s, axis=0, keepdims=True)).astype(BF16))

    def weighted_values(r, hh):
        kt0 = k_start(r) // MXU_TILE
        tiles = [vx_ref[hh, kt0 + i] for i in range(nk // MXU_TILE)]
        tiles += [vx_ref[hh, n_tok_tiles + i] for i in range(n_ctx_tiles)]
        return _dot(jnp.concatenate(tiles, axis=1), p_ref[hh])

    def emit(r, o_second):
        o_first = oa_ref[...]
        num = jnp.concatenate([o_first[0:DH_C], o_second[DH_C:]], axis=0)
        den = jnp.concatenate([o_first[DH_C:], o_second[0:DH_C]], axis=0)
        o_ref[q_rows(r), :] = ((num / den).T * sg_ref[q_rows(r), :].astype(F32)).astype(BF16)

    p_ref[1] = jnp.ones(p_ref.shape[1:], BF16)
    oa_ref[...] = jnp.ones(oa_ref.shape, F32)
    logits(0, 0)

    def block(r, carry):
        prev = jnp.maximum(r - 1, 0)
        emit(prev, weighted_values(prev, 1))
        logits(r, 1)
        softmax(0)
        oa_ref[...] = weighted_values(r, 0)
        logits(jnp.minimum(r + 1, n_blk - 1), 0)
        softmax(1)
        return carry

    lax.fori_loop(0, n_blk, block, 0, unroll=2)
    emit(n_blk - 1, weighted_values(n_blk - 1, 1))


def _nbr_attn(q, k, v, sg, cache_k, cache_v, cache_layer, rpb, n_batch):
    t, e = q.shape
    seq = t // n_batch
    rows = seq // GRID_W
    n_cached, past = cache_k.shape[1:3]
    ck = jnp.transpose(cache_k, (0, 1, 3, 4, 2)).reshape(n_batch * n_cached, e, past)
    cv = jnp.transpose(cache_v, (0, 1, 3, 4, 2)).reshape(n_batch * n_cached, e, past)
    tq, nk = ATT_Q_ROWS * GRID_W, ATT_K_ROWS * GRID_W
    assert nk % MXU_TILE == 0 and past % MXU_TILE == 0 and (ATT_Q_ROWS * GRID_W) % MXU_TILE == 0
    n_dr, n_dc = rpb.shape[1:]
    rpb_pad = jnp.pad(rpb.astype(F32)[:, :, ::-1], ((0, 0), (0, 2 * WIN_R - n_dr), (0, LANES - n_dc)))
    tok_spec = pl.BlockSpec((seq, LANES), lambda j, b: (b, j))
    ctx_spec = pl.BlockSpec((None, LANES, past), lambda j, b: (b * n_cached + cache_layer, j, 0))
    return pl.pallas_call(
        functools.partial(_nbr_attn_kernel, rows),
        grid=(e // LANES, n_batch),
        in_specs=[pl.BlockSpec((2, 2 * WIN_R, LANES), lambda j, b: (j, 0, 0)),
                  tok_spec, tok_spec, tok_spec, tok_spec, ctx_spec, ctx_spec],
        out_specs=tok_spec,
        out_shape=jax.ShapeDtypeStruct((t, e), BF16),
        scratch_shapes=[
            pltpu.VMEM((2, 3, nk, tq), F32),
            pltpu.VMEM((past, LANES), BF16),
            pltpu.VMEM((2, (seq + past) // MXU_TILE, LANES, MXU_TILE), BF16),
            pltpu.VMEM((2, nk + past, tq), F32),
            pltpu.VMEM((2, nk + past, tq), BF16),
            pltpu.VMEM((LANES, tq), F32),
        ],
        compiler_params=_params(2),
        name="nbr_attn",
    )(rpb_pad, q, k, v, sg, ck, cv)


def _na_out_kernel(x_ref, y_ref, mod_ref, wout_ref, g_ref, b_ref, o_ref):
    tm, d = x_ref.shape
    _, _, gate = _split_mod(mod_ref, d)
    o = _dot(y_ref[...], wout_ref[...].astype(BF16))
    o_ref[...] = _resid_layer_norm(x_ref[...], gate, o, g_ref[...], b_ref[...])


def _na_out(x2d, gated, seq_len, per_sample, mods, layer, w_out, ln_g, ln_b):
    t, d = x2d.shape
    tm = ROW_TILE
    tps = max(seq_len // tm, 1)
    e = w_out.shape[0]
    return pl.pallas_call(
        _na_out_kernel,
        grid=(t // tm,),
        in_specs=[
            pl.BlockSpec((tm, d), lambda i: (i, 0)),
            pl.BlockSpec((tm, e), lambda i: (i, 0)),
            _mod_spec(layer, d, tps, per_sample),
            _const_spec(w_out.shape),
            _const_spec((1, d)),
            _const_spec((1, d)),
        ],
        out_specs=pl.BlockSpec((tm, d), lambda i: (i, 0)),
        out_shape=jax.ShapeDtypeStruct((t, d), F32),
        compiler_params=_params(1),
        name="na_out",
    )(x2d, gated, mods, w_out, ln_g.reshape(1, d), ln_b.reshape(1, d))


def kernel(x_prompt, x_sample, c, cache_k, cache_v, c_ctx, w_mod, b_mod, ln_g, ln_b, pool_w_in, pool_w_grp,
           pool_scale, pool_w_out, sgu_w_in, sgu_ln_g, sgu_ln_b, sgu_w_s, sgu_b_s, sgu_w_out, na_w_in, na_rpb,
           na_w_out):
    n_p, seq_p, d = x_prompt.shape
    n_s, seq_s, _ = x_sample.shape
    assert n_s + 1 <= N_COND_ROWS and (n_p * seq_p) % ROW_TILE == 0 and seq_s % ROW_TILE == 0
    assert seq_p % CHUNK == 0 and (seq_p % ROW_TILE == 0 or ROW_TILE % seq_p == 0) and seq_p % POOL_BLOCK == 0
    assert seq_s % (GRID_W * ATT_Q_ROWS) == 0 and seq_s // GRID_W >= ATT_K_ROWS
    conds = jnp.zeros((N_COND_ROWS, d), F32).at[0].set(c_ctx).at[1:1 + n_s].set(c)
    mods = _mods(conds, w_mod, b_mod)
    pool_w_fold = _pool_fold(pool_w_in, pool_w_grp)
    pool_band = _pool_band()

    yp = x_prompt.reshape(n_p * seq_p, d)
    ys = x_sample.reshape(n_s * seq_s, d)
    streams = ((seq_p, False), (seq_s, True))
    ctx_k = ctx_v = pending = None
    for i in range(DEPTH):
        kind, j = i % N_MIXERS, i // N_MIXERS
        ys_in = (yp, ys)
        outs = []
        if kind == 0:
            for x2d, (seq, per_sample) in zip(ys_in, streams):
                outs.append(_pool_layer(x2d, seq, per_sample, mods, i, j, pool_band, pool_w_fold, pool_w_in,
                                        pool_scale, pool_w_out, ln_g[i], ln_b[i],
                                        prev=pending if per_sample else None))
            pending = None
        elif kind == 1:
            w_in, w_s, w_out = sgu_w_in[j].astype(BF16), sgu_w_s[j].astype(BF16), sgu_w_out[j].astype(BF16)
            for x2d, (seq, per_sample) in zip(ys_in, streams):
                outs.append(_sgu_layer(x2d, seq, per_sample, mods, i, w_in, sgu_ln_g[j], sgu_ln_b[j], w_s,
                                       sgu_b_s[j].T, w_out, ln_g[i], ln_b[i]))
        else:
            w_in, w_out = na_w_in[j], na_w_out[j]
            q, k, v, sg, kf, vf = _na_proj(yp, seq_p, False, mods, i, w_in, True, DH_C ** -0.5)
            ctx_k = kf.reshape(n_p, 1, H_C, DH_C, seq_p).transpose(0, 1, 4, 2, 3)
            ctx_v = vf.reshape(n_p, 1, H_C, DH_C, seq_p).transpose(0, 1, 4, 2, 3)
            outs.append(_ctx_attn_out(yp, q, k, v, sg, seq_p, mods, i, w_out, ln_g[i], ln_b[i]))
            q, k, v, sg = _na_proj(ys, seq_s, True, mods, i, w_in, False, DH_C ** -0.5 * LOG2_E)
            gated = _nbr_attn(q, k, v, sg, cache_k, cache_v, j, na_rpb[j], n_s)
            if i + 1 < DEPTH and (i + 1) % N_MIXERS == 0 and seq_s >= ROW_TILE:
                pending = (gated, i, w_out, ln_g[i], ln_b[i])
                outs.append(ys)
            else:
                outs.append(_na_out(ys, gated, seq_s, True, mods, i, w_out, ln_g[i], ln_b[i]))
        yp, ys = outs
    return (yp.reshape(n_p, seq_p, d), ys.reshape(n_s, seq_s, d), ctx_k, ctx_v)
```

```python
import functools

import jax
import jax.numpy as jnp
import numpy as np
from jax import lax
from jax.experimental import pallas as pl
from jax.experimental.pallas import tpu as pltpu

F32 = jnp.float32
BF16 = jnp.bfloat16

DEPTH = 4
N_MIXERS = 3
POOL_WINDOWS = (2, 4, 8, 16)
POOL_HALO = 16
POOL_BLOCK = 128
CHUNK = 128
H_B = 8
H_C = 16
DH_C = 64
GRID_W = 64
WIN_R = 8
WIN_C = 16
DEEPNORM_ALPHA = (2 * DEPTH) ** 0.25
LN_EPS = 1e-5
LOG2_E = float(np.log2(np.e))

N_COND_ROWS = 8
ROW_TILE = 512
ATT_Q_ROWS = 4
ATT_K_ROWS = 12
LANES = 128
MXU_TILE = 256
VMEM_LIMIT = 56 * 1024 * 1024


def _const_spec(shape):
    nd = len(shape)
    return pl.BlockSpec(shape, lambda *_: (0,) * nd, pipeline_mode=pl.Buffered(1))


def _params(n_axes):
    return pltpu.CompilerParams(dimension_semantics=("arbitrary",) * n_axes, vmem_limit_bytes=VMEM_LIMIT)


def _mod_spec(layer, d, tiles_per_seq, per_sample):
    base = layer * N_COND_ROWS
    if per_sample:
        return pl.BlockSpec((1, 1, 3 * d), lambda i: (base + 1 + i // tiles_per_seq, 0, 0))
    return pl.BlockSpec((1, 1, 3 * d), lambda i: (base, 0, 0))


def _split_mod(mod_ref, d):
    m = mod_ref[0]
    return m[:, :d], m[:, d:2 * d], m[:, 2 * d:]


def _silu(x):
    hx = 0.5 * x
    return hx + hx * jnp.tanh(hx)


def _gelu_tanh_x2(x):
    c = np.float32(np.sqrt(2.0 / np.pi))
    return x * (1.0 + jnp.tanh(x * (c + (c * np.float32(0.044715)) * (x * x))))


def _layer_norm(x, g, b, eps=LN_EPS):
    mu = jnp.mean(x, axis=-1, keepdims=True)
    d = x - mu
    var = jnp.mean(d * d, axis=-1, keepdims=True)
    return d * lax.rsqrt(var + eps) * g + b


def _resid_layer_norm(x, gate, o, g, b):
    return _layer_norm(x + (gate * (1.0 / DEEPNORM_ALPHA)) * o, g, b, LN_EPS / DEEPNORM_ALPHA ** 2)


def _dot(a, b):
    return jnp.dot(a, b, preferred_element_type=F32)


def _dot_nt(a, b):
    return lax.dot_general(a, b, (((1,), (1,)), ((), ())), preferred_element_type=F32)


def _mods_kernel(cond_ref, w_ref, b_ref, o_ref):
    a = _silu(cond_ref[...]).astype(BF16)
    o_ref[0] = _dot(a, w_ref[0].astype(BF16)) + b_ref[0]


def _mods(conds, w_mod, b_mod):
    depth, d, n = w_mod.shape
    tn = n
    out = pl.pallas_call(
        _mods_kernel,
        grid=(depth, n // tn),
        in_specs=[
            pl.BlockSpec((N_COND_ROWS, d), lambda l, j: (0, 0)),
            pl.BlockSpec((1, d, tn), lambda l, j: (l, 0, j)),
            pl.BlockSpec((1, 1, tn), lambda l, j: (l, 0, j)),
        ],
        out_specs=pl.BlockSpec((1, N_COND_ROWS, tn), lambda l, j: (l, 0, j)),
        out_shape=jax.ShapeDtypeStruct((depth, N_COND_ROWS, n), F32),
        compiler_params=_params(2),
        name="adaln_mods",
    )(conds, w_mod, b_mod.reshape(depth, 1, n))
    return out.reshape(depth * N_COND_ROWS, 1, n)


def _pool_fold_kernel(win_x_ref, wgrp_ref, fold_ref):
    fold_ref[0] = _dot(win_x_ref[0].astype(BF16), wgrp_ref[0, 0].astype(BF16)).astype(BF16)


def _pool_fold(w_in, w_grp):
    n_layers, d, two_e = w_in.shape
    n_grp, grp = w_grp.shape[1:3]
    e_dim = two_e // 2
    return pl.pallas_call(
        _pool_fold_kernel,
        grid=(n_layers, n_grp),
        in_specs=[
            pl.BlockSpec((1, d, grp), lambda l, g: (l, 0, g)),
            pl.BlockSpec((1, 1, grp, grp), lambda l, g: (l, g, 0, 0)),
        ],
        out_specs=pl.BlockSpec((1, d, grp), lambda l, g: (l, 0, g)),
        out_shape=jax.ShapeDtypeStruct((n_layers, d, e_dim), BF16),
        compiler_params=_params(2),
        name="pool_fold",
    )(w_in, w_grp)


def _pool_band():
    t = np.arange(POOL_BLOCK)[:, None] + POOL_HALO
    e = np.arange(POOL_BLOCK + 2 * POOL_HALO)[None, :]
    return jnp.asarray(np.stack([(e >= t - w // 2) & (e <= t + w // 2 - 1) for w in POOL_WINDOWS]), BF16)


def _pool_kernel(seq_len, fuse_prev, x_ref, xp_ref, xn_ref, *refs):
    tm, d = x_ref.shape
    if fuse_prev:
        (y_ref, yp_ref, yn_ref, pmod_ref, pw_ref, pg_ref, pb_ref, mod_ref, band_ref, wfold_ref, wgate_ref, psc_ref,
         wout_ref, g_ref, b_ref, o_ref, e_ref, u_ref, xe_ref, ye_ref) = refs
        for ext, parts in ((xe_ref, (xp_ref, x_ref, xn_ref)), (ye_ref, (yp_ref, y_ref, yn_ref))):
            ext[0:POOL_HALO] = parts[0][...]
            ext[POOL_HALO:POOL_HALO + tm] = parts[1][...]
            ext[POOL_HALO + tm:] = parts[2][...]
        _, _, prev_gate = _split_mod(pmod_ref, d)
        o_prev = _dot(ye_ref[...], pw_ref[...].astype(BF16))
        x_ext = _resid_layer_norm(xe_ref[...], prev_gate, o_prev, pg_ref[...], pb_ref[...])
        x_before, x, x_after = x_ext[0:POOL_HALO], x_ext[POOL_HALO:POOL_HALO + tm], x_ext[POOL_HALO + tm:]
    else:
        mod_ref, band_ref, wfold_ref, wgate_ref, psc_ref, wout_ref, g_ref, b_ref, o_ref, e_ref, u_ref = refs
        x_before, x, x_after = xp_ref[...], x_ref[...], xn_ref[...]
    e_dim = psc_ref.shape[1]
    grp = e_dim // len(POOL_WINDOWS)
    seg = min(tm, seq_len)
    stride = seg + 2 * POOL_HALO
    shift, scale, gate = _split_mod(mod_ref, d)
    hb = (x * (1.0 + scale) + shift).astype(BF16)
    zeros = jnp.zeros((POOL_HALO, d), BF16)
    seg_pos = lax.broadcasted_iota(jnp.int32, (seg, 1), 0)
    if seg == tm:
        first_pos = (pl.program_id(0) * tm) % seq_len
        pos = first_pos + seg_pos
        hp = (x_before * (1.0 + scale) + shift).astype(BF16)
        hn = (x_after * (1.0 + scale) + shift).astype(BF16)
        halos = [(jnp.where(first_pos != 0, hp, zeros), jnp.where(first_pos + tm != seq_len, hn, zeros))]
    else:
        pos = jnp.concatenate([seg_pos] * (tm // seg), axis=0)
        halos = [(zeros, zeros)] * (tm // seg)
    for sg, (before, after) in enumerate(halos):
        e_ref[sg * stride:sg * stride + POOL_HALO] = before
        e_ref[sg * stride + POOL_HALO:sg * stride + POOL_HALO + seg] = hb[sg * seg:(sg + 1) * seg]
        e_ref[sg * stride + POOL_HALO + seg:(sg + 1) * stride] = after
    u_ref[...] = _dot(e_ref[...], wfold_ref[...]).astype(BF16)

    mixed = []
    for gi, w in enumerate(POOL_WINDOWS):
        cols = slice(gi * grp, (gi + 1) * grp)
        lo = jnp.maximum(pos - w // 2, 0)
        hi = jnp.minimum(pos + w // 2 - 1, seq_len - 1)
        inv_cnt = 1.0 / (hi - lo + 1).astype(F32)
        sums, own = [], []
        for sg in range(len(halos)):
            for rb in range(seg // POOL_BLOCK):
                r0 = sg * stride + rb * POOL_BLOCK
                sums.append(_dot(band_ref[gi], u_ref[r0:r0 + POOL_BLOCK + 2 * POOL_HALO, cols]))
            own.append(u_ref[sg * stride + POOL_HALO:sg * stride + POOL_HALO + seg, cols])
        sums = jnp.concatenate(sums, axis=0)
        own = own[0] if len(own) == 1 else jnp.concatenate(own, axis=0)
        mixed.append(sums * inv_cnt - own.astype(F32))
    mixed = jnp.concatenate(mixed, axis=1)
    gate_pre = _dot(hb, wgate_ref[...].astype(BF16))
    y = (mixed * psc_ref[...] * _silu(gate_pre)).astype(BF16)
    o = _dot(y, wout_ref[...].astype(BF16))
    o_ref[...] = _resid_layer_norm(x, gate, o, g_ref[...], b_ref[...])


def _pool_layer(x2d, seq_len, per_sample, mods, layer, j, band, w_fold, w_in, p_scale, w_out, ln_g, ln_b, prev=None):
    t, d = x2d.shape
    tm = ROW_TILE
    tps = max(seq_len // tm, 1)
    e_dim = w_out.shape[1]
    hb = tm // POOL_HALO
    last = t // POOL_HALO - 1
    n_seg = max(tm // seq_len, 1)
    e_rows = tm + 2 * POOL_HALO * n_seg

    def layer_spec(shape):
        return pl.BlockSpec((None,) + shape, lambda i: (j,) + (0,) * len(shape), pipeline_mode=pl.Buffered(1))

    def halo_specs(width):
        return [pl.BlockSpec((tm, width), lambda i: (i, 0)),
                pl.BlockSpec((POOL_HALO, width), lambda i: (jnp.maximum(i * hb - 1, 0), 0)),
                pl.BlockSpec((POOL_HALO, width), lambda i: (jnp.minimum((i + 1) * hb, last), 0))]

    prev_specs, prev_args, prev_scratch = [], [], []
    if prev is not None:
        assert n_seg == 1
        y2d, prev_layer, prev_w, prev_g, prev_b = prev
        e_prev = y2d.shape[1]
        prev_specs = halo_specs(e_prev) + [_mod_spec(prev_layer, d, tps, per_sample), _const_spec(prev_w.shape),
                                           _const_spec((1, d)), _const_spec((1, d))]
        prev_args = [y2d, y2d, y2d, mods, prev_w, prev_g.reshape(1, d), prev_b.reshape(1, d)]
        prev_scratch = [pltpu.VMEM((e_rows, d), F32), pltpu.VMEM((e_rows, e_prev), BF16)]

    return pl.pallas_call(
        functools.partial(_pool_kernel, seq_len, prev is not None),
        grid=(t // tm,),
        in_specs=halo_specs(d) + prev_specs + [
            _mod_spec(layer, d, tps, per_sample),
            _const_spec(band.shape),
            layer_spec((d, e_dim)),
            pl.BlockSpec((None, d, e_dim), lambda i: (j, 0, 1), pipeline_mode=pl.Buffered(1)),
            layer_spec((1, e_dim)),
            layer_spec((e_dim, d)),
            _const_spec((1, d)),
            _const_spec((1, d)),
        ],
        out_specs=pl.BlockSpec((tm, d), lambda i: (i, 0)),
        out_shape=jax.ShapeDtypeStruct((t, d), F32),
        scratch_shapes=[pltpu.VMEM((e_rows, d), BF16), pltpu.VMEM((e_rows, e_dim), BF16)] + prev_scratch,
        compiler_params=_params(1),
        name="pool_layer",
    )(x2d, x2d, x2d, *prev_args, mods, band, w_fold, w_in, p_scale.reshape(-1, 1, e_dim), w_out,
      ln_g.reshape(1, d), ln_b.reshape(1, d))


def _sgu_kernel(x_ref, mod_ref, win_ref, lg_ref, lb_ref, ws_ref, bs_ref, wout_ref, g_ref, b_ref, o_ref, y_ref):
    tm, d = x_ref.shape
    e_dim = lg_ref.shape[1]
    dh = e_dim // H_B
    shift, scale, gate = _split_mod(mod_ref, d)
    x = x_ref[...]
    hb = (x * (1.0 + scale) + shift).astype(BF16)
    v2 = _gelu_tanh_x2(_dot(hb, win_ref[:, e_dim:2 * e_dim]))
    v = _layer_norm(v2, lg_ref[...], lb_ref[...], 4.0 * LN_EPS).astype(BF16)
    u2 = _gelu_tanh_x2(_dot(hb, win_ref[:, 0:e_dim]))
    ug = u2 * _silu(_dot(hb, win_ref[:, 2 * e_dim:3 * e_dim]))
    for c in range(tm // CHUNK):
        rows = slice(c * CHUNK, (c + 1) * CHUNK)
        for hh in range(H_B):
            cols = slice(hh * dh, (hh + 1) * dh)
            sv = _dot(ws_ref[hh], v[rows, cols]) + bs_ref[:, hh:hh + 1]
            y_ref[rows, cols] = (ug[rows, cols] * sv).astype(BF16)
    o2 = _dot(y_ref[...], wout_ref[...])
    o_ref[...] = _resid_layer_norm(x, 0.5 * gate, o2, g_ref[...], b_ref[...])


def _sgu_layer(x2d, seq_len, per_sample, mods, layer, w_in, sln_g, sln_b, w_s, b_s_t, w_out, ln_g, ln_b):
    t, d = x2d.shape
    tm = ROW_TILE
    tps = max(seq_len // tm, 1)
    e_dim = w_out.shape[0]
    return pl.pallas_call(
        _sgu_kernel,
        grid=(t // tm,),
        in_specs=[
            pl.BlockSpec((tm, d), lambda i: (i, 0)),
            _mod_spec(layer, d, tps, per_sample),
            _const_spec(w_in.shape),
            _const_spec((1, e_dim)),
            _const_spec((1, e_dim)),
            _const_spec(w_s.shape),
            _const_spec(b_s_t.shape),
            _const_spec(w_out.shape),
            _const_spec((1, d)),
            _const_spec((1, d)),
        ],
        out_specs=pl.BlockSpec((tm, d), lambda i: (i, 0)),
        out_shape=jax.ShapeDtypeStruct((t, d), F32),
        scratch_shapes=[pltpu.VMEM((tm, e_dim), BF16)],
        compiler_params=_params(1),
        name="sgu_layer",
    )(x2d, mods, w_in, sln_g.reshape(1, e_dim), sln_b.reshape(1, e_dim), w_s, b_s_t, w_out,
      ln_g.reshape(1, d), ln_b.reshape(1, d))


def _na_proj_kernel(q_scale, x_ref, mod_ref, win_ref, q_ref, k_ref, v_ref, sg_ref):
    tm, d = x_ref.shape
    e = win_ref.shape[1] // 4
    shift, scale, _ = _split_mod(mod_ref, d)
    hb = (x_ref[...] * (1.0 + scale) + shift).astype(BF16)
    q_ref[...] = (_dot(hb, win_ref[:, 0:e].astype(BF16)) * q_scale).astype(BF16)
    k = _dot(hb, win_ref[:, e:2 * e].astype(BF16))
    v = _dot(hb, win_ref[:, 2 * e:3 * e].astype(BF16))
    k_ref[...] = k.astype(BF16)
    v_ref[...] = v.astype(BF16)
    sg_ref[...] = _silu(_dot(hb, win_ref[:, 3 * e:4 * e].astype(BF16))).astype(BF16)


def _na_proj(x2d, seq_len, per_sample, mods, layer, w_in, q_scale):
    t, d = x2d.shape
    tm = ROW_TILE
    tps = max(seq_len // tm, 1)
    e = w_in.shape[1] // 4
    row_spec = pl.BlockSpec((tm, e), lambda i: (i, 0))
    return pl.pallas_call(
        functools.partial(_na_proj_kernel, q_scale),
        grid=(t // tm,),
        in_specs=[
            pl.BlockSpec((tm, d), lambda i: (i, 0)),
            _mod_spec(layer, d, tps, per_sample),
            _const_spec(w_in.shape),
        ],
        out_specs=[row_spec] * 4,
        out_shape=[jax.ShapeDtypeStruct((t, e), BF16)] * 4,
        compiler_params=_params(1),
        name="na_proj",
    )(x2d, mods, w_in)


def _lane_is_first_head():
    return lax.broadcasted_iota(jnp.int32, (1, LANES), 1) < DH_C


def _ctx_layer_kernel(seq_len, x_ref, mod_ref, win_ref, wout_ref, g_ref, b_ref, o_ref, kf_ref, vf_ref,
                      q_ref, k_ref, v_ref, y_ref):
    tm, d = x_ref.shape
    e = wout_ref.shape[0]
    shift, scale, gate = _split_mod(mod_ref, d)
    x = x_ref[...]
    hb = (x * (1.0 + scale) + shift).astype(BF16)
    q_ref[...] = (_dot(hb, win_ref[:, 0:e].astype(BF16)) * (DH_C ** -0.5)).astype(BF16)
    k = _dot(hb, win_ref[:, e:2 * e].astype(BF16))
    v = _dot(hb, win_ref[:, 2 * e:3 * e].astype(BF16))
    k_ref[...] = k.astype(BF16)
    v_ref[...] = v.astype(BF16)
    for sq in range(tm // seq_len):
        kf_ref[sq] = k[sq * seq_len:(sq + 1) * seq_len, :].T
        vf_ref[sq] = v[sq * seq_len:(sq + 1) * seq_len, :].T
    sg = _silu(_dot(hb, win_ref[:, 3 * e:4 * e].astype(BF16)))

    first = _lane_is_first_head()
    for sq in range(tm // seq_len):
        rows = slice(sq * seq_len, (sq + 1) * seq_len)
        for j in range(e // LANES):
            cols = slice(j * LANES, (j + 1) * LANES)
            qp, kp, vp = q_ref[rows, cols], k_ref[rows, cols], v_ref[rows, cols]
            outs = []
            for sel in (first, jnp.logical_not(first)):
                s = _dot_nt(jnp.where(sel, qp, jnp.zeros_like(qp)), kp)
                p = jnp.exp(s - jnp.max(s, axis=-1, keepdims=True))
                l = jnp.sum(p, axis=-1, keepdims=True)
                outs.append(_dot(p.astype(BF16), vp) / l)
            y_ref[rows, cols] = (jnp.where(first, outs[0], outs[1]) * sg[rows, cols]).astype(BF16)
    o = _dot(y_ref[...], wout_ref[...].astype(BF16))
    o_ref[...] = _resid_layer_norm(x, gate, o, g_ref[...], b_ref[...])


def _ctx_layer(x2d, seq_len, mods, layer, w_in, w_out, ln_g, ln_b):
    t, d = x2d.shape
    e = w_out.shape[0]
    tm = ROW_TILE
    kv_spec = pl.BlockSpec((tm // seq_len, e, seq_len), lambda i: (i, 0, 0))
    kv_shape = jax.ShapeDtypeStruct((t // seq_len, e, seq_len), F32)
    return pl.pallas_call(
        functools.partial(_ctx_layer_kernel, seq_len),
        grid=(t // tm,),
        in_specs=[pl.BlockSpec((tm, d), lambda i: (i, 0)), _mod_spec(layer, d, 1, False),
                  _const_spec(w_in.shape), _const_spec(w_out.shape), _const_spec((1, d)), _const_spec((1, d))],
        out_specs=[pl.BlockSpec((tm, d), lambda i: (i, 0)), kv_spec, kv_spec],
        out_shape=[jax.ShapeDtypeStruct((t, d), F32), kv_shape, kv_shape],
        scratch_shapes=[pltpu.VMEM((tm, e), BF16)] * 4,
        compiler_params=_params(1),
        name="ctx_layer",
    )(x2d, mods, w_in, w_out, ln_g.reshape(1, d), ln_b.reshape(1, d))


def _key_row_start(r_blk, rows):
    return jnp.clip(r_blk * ATT_Q_ROWS - WIN_R // 2, 0, rows - ATT_K_ROWS)


def _nbr_bias_build(rows, rpb_ref, bias_ref):
    n_blk = rows // ATT_Q_ROWS
    kc = lax.broadcasted_iota(jnp.int32, (GRID_W, LANES), 0)
    lane = lax.broadcasted_iota(jnp.int32, (GRID_W, LANES), 1)
    qc = lane & (GRID_W - 1)
    c0 = jnp.clip(qc - WIN_C // 2, 0, GRID_W - WIN_C)
    col_in = (kc >= c0) & (kc < c0 + WIN_C)
    first = lane < GRID_W
    neg = jnp.full((GRID_W, LANES), -jnp.inf, F32)
    for hh in range(2):
        lo, hi = [], []
        for dr in range(2 * WIN_R - 1):
            row = jnp.broadcast_to(rpb_ref[hh, dr:dr + 1, :] * LOG2_E, (GRID_W, LANES))
            lo.append(pltpu.roll(row, LANES - (WIN_C - 1), 1, stride=1, stride_axis=0))
            hi.append(pltpu.roll(row, GRID_W - (WIN_C - 1), 1, stride=1, stride_axis=0))
        for var, r_blk in enumerate((0, 1, n_blk - 1)):
            ks = min(max(r_blk * ATT_Q_ROWS - WIN_R // 2, 0), rows - ATT_K_ROWS)
            for ki in range(ATT_K_ROWS):
                kr = ks + ki
                for qp in range(ATT_Q_ROWS // 2):
                    halves = []
                    for half, src in enumerate((lo, hi)):
                        r = r_blk * ATT_Q_ROWS + 2 * qp + half
                        r0 = min(max(r - WIN_R // 2, 0), rows - WIN_R)
                        halves.append(src[kr - r + WIN_R - 1] if r0 <= kr < r0 + WIN_R else neg)
                    blk = jnp.where(col_in, jnp.where(first, halves[0], halves[1]), neg)
                    bias_ref[hh, var, ki * GRID_W:(ki + 1) * GRID_W, qp * LANES:(qp + 1) * LANES] = blk


def _nbr_attn_kernel(rows, rpb_ref, q_ref, k_ref, v_ref, sg_ref, ck_ref, cv_ref, o_ref, bias_ref, kc_ref, vx_ref,
                     s_ref, p_ref, oa_ref):
    n_blk = rows // ATT_Q_ROWS
    tq = ATT_Q_ROWS * GRID_W
    nk = ATT_K_ROWS * GRID_W
    seq = q_ref.shape[0]
    n_tok_tiles = seq // MXU_TILE
    n_ctx_tiles = kc_ref.shape[0] // MXU_TILE

    @pl.when(pl.program_id(1) == 0)
    def _():
        _nbr_bias_build(rows, rpb_ref, bias_ref)

    first = _lane_is_first_head()
    head_lanes = (first, jnp.logical_not(first))
    kc_ref[...] = ck_ref[...].T.astype(BF16)
    v_t = v_ref[...].astype(F32).T
    top = lax.broadcasted_iota(jnp.int32, (LANES, 1), 0) < DH_C
    for hh, sel in enumerate((top, jnp.logical_not(top))):
        for kt in range(n_tok_tiles):
            vx_ref[hh, kt] = jnp.where(sel, v_t[:, kt * MXU_TILE:(kt + 1) * MXU_TILE], 1.0).astype(BF16)
        for kt in range(n_ctx_tiles):
            vx_ref[hh, n_tok_tiles + kt] = jnp.where(
                sel, cv_ref[:, kt * MXU_TILE:(kt + 1) * MXU_TILE], 1.0).astype(BF16)

    def q_rows(r):
        return pl.ds(pl.multiple_of(r * tq, tq), tq)

    def k_start(r):
        return pl.multiple_of(_key_row_start(r, rows) * GRID_W, MXU_TILE)

    def logits(r, hh):
        q = q_ref[q_rows(r), :]
        qh = jnp.where(head_lanes[hh], q, jnp.zeros_like(q))
        var = jnp.where(r == 0, 0, jnp.where(r == n_blk - 1, 2, 1))
        s_ref[hh, 0:nk, :] = _dot_nt(k_ref[pl.ds(k_start(r), nk), :], qh) + bias_ref[hh, var]
        s_ref[hh, nk:, :] = _dot_nt(kc_ref[...], qh)

    def softmax(hh):
        s = s_ref[hh]
        p_ref[hh] = jnp.exp2((s - jnp.max(s, axis=0, keepdims=True)).astype(BF16))

    def weighted_values(r, hh):
        kt0 = k_start(r) // MXU_TILE
        tiles = [vx_ref[hh, kt0 + i] for i in range(nk // MXU_TILE)]
        tiles += [vx_ref[hh, n_tok_tiles + i] for i in range(n_ctx_tiles)]
        return _dot(jnp.concatenate(tiles, axis=1), p_ref[hh])

    def emit(r, o_second):
        o_first = oa_ref[...]
        num = jnp.concatenate([o_first[0:DH_C], o_second[DH_C:]], axis=0)
        den = jnp.concatenate([o_first[DH_C:], o_second[0:DH_C]], axis=0)
        o_ref[q_rows(r), :] = ((num / den).T * sg_ref[q_rows(r), :].astype(F32)).astype(BF16)

    p_ref[1] = jnp.ones(p_ref.shape[1:], BF16)
    oa_ref[...] = jnp.ones(oa_ref.shape, F32)
    logits(0, 0)

    def block(r, carry):
        prev = jnp.maximum(r - 1, 0)
        emit(prev, weighted_values(prev, 1))
        logits(r, 1)
        softmax(0)
        oa_ref[...] = weighted_values(r, 0)
        logits(jnp.minimum(r + 1, n_blk - 1), 0)
        softmax(1)
        return carry

    lax.fori_loop(0, n_blk, block, 0, unroll=2)
    emit(n_blk - 1, weighted_values(n_blk - 1, 1))


def _nbr_attn(q, k, v, sg, cache_k, cache_v, cache_layer, rpb, n_batch):
    t, e = q.shape
    seq = t // n_batch
    rows = seq // GRID_W
    n_cached, past = cache_k.shape[1:3]
    ck = jnp.transpose(cache_k, (0, 1, 3, 4, 2)).reshape(n_batch * n_cached, e, past)
    cv = jnp.transpose(cache_v, (0, 1, 3, 4, 2)).reshape(n_batch * n_cached, e, past)
    tq, nk = ATT_Q_ROWS * GRID_W, ATT_K_ROWS * GRID_W
    assert nk % MXU_TILE == 0 and past % MXU_TILE == 0 and (ATT_Q_ROWS * GRID_W) % MXU_TILE == 0
    n_dr, n_dc = rpb.shape[1:]
    rpb_pad = jnp.pad(rpb.astype(F32)[:, :, ::-1], ((0, 0), (0, 2 * WIN_R - n_dr), (0, LANES - n_dc)))
    tok_spec = pl.BlockSpec((seq, LANES), lambda j, b: (b, j))
    ctx_spec = pl.BlockSpec((None, LANES, past), lambda j, b: (b * n_cached + cache_layer, j, 0))
    return pl.pallas_call(
        functools.partial(_nbr_attn_kernel, rows),
        grid=(e // LANES, n_batch),
        in_specs=[pl.BlockSpec((2, 2 * WIN_R, LANES), lambda j, b: (j, 0, 0)),
                  tok_spec, tok_spec, tok_spec, tok_spec, ctx_spec, ctx_spec],
        out_specs=tok_spec,
        out_shape=jax.ShapeDtypeStruct((t, e), BF16),
        scratch_shapes=[
            pltpu.VMEM((2, 3, nk, tq), F32),
            pltpu.VMEM((past, LANES), BF16),
            pltpu.VMEM((2, (seq + past) // MXU_TILE, LANES, MXU_TILE), BF16),
            pltpu.VMEM((2, nk + past, tq), F32),
            pltpu.VMEM((2, nk + past, tq), BF16),
            pltpu.VMEM((LANES, tq), F32),
        ],
        compiler_params=_params(2),
        name="nbr_attn",
    )(rpb_pad, q, k, v, sg, ck, cv)


def _na_out_kernel(x_ref, y_ref, mod_ref, wout_ref, g_ref, b_ref, o_ref):
    tm, d = x_ref.shape
    _, _, gate = _split_mod(mod_ref, d)
    o = _dot(y_ref[...], wout_ref[...].astype(BF16))
    o_ref[...] = _resid_layer_norm(x_ref[...], gate, o, g_ref[...], b_ref[...])


def _na_out(x2d, gated, seq_len, per_sample, mods, layer, w_out, ln_g, ln_b):
    t, d = x2d.shape
    tm = ROW_TILE
    tps = max(seq_len // tm, 1)
    e = w_out.shape[0]
    return pl.pallas_call(
        _na_out_kernel,
        grid=(t // tm,),
        in_specs=[
            pl.BlockSpec((tm, d), lambda i: (i, 0)),
            pl.BlockSpec((tm, e), lambda i: (i, 0)),
            _mod_spec(layer, d, tps, per_sample),
            _const_spec(w_out.shape),
            _const_spec((1, d)),
            _const_spec((1, d)),
        ],
        out_specs=pl.BlockSpec((tm, d), lambda i: (i, 0)),
        out_shape=jax.ShapeDtypeStruct((t, d), F32),
        compiler_params=_params(1),
        name="na_out",
    )(x2d, gated, mods, w_out, ln_g.reshape(1, d), ln_b.reshape(1, d))


def kernel(x_prompt, x_sample, c, cache_k, cache_v, c_ctx, w_mod, b_mod, ln_g, ln_b, pool_w_in, pool_w_grp,
           pool_scale, pool_w_out, sgu_w_in, sgu_ln_g, sgu_ln_b, sgu_w_s, sgu_b_s, sgu_w_out, na_w_in, na_rpb,
           na_w_out):
    n_p, seq_p, d = x_prompt.shape
    n_s, seq_s, _ = x_sample.shape
    assert n_s + 1 <= N_COND_ROWS and (n_p * seq_p) % ROW_TILE == 0 and seq_s % ROW_TILE == 0
    assert seq_p % CHUNK == 0 and (seq_p % ROW_TILE == 0 or ROW_TILE % seq_p == 0) and seq_p % POOL_BLOCK == 0
    assert seq_s % (GRID_W * ATT_Q_ROWS) == 0 and seq_s // GRID_W >= ATT_K_ROWS
    conds = jnp.zeros((N_COND_ROWS, d), F32).at[0].set(c_ctx).at[1:1 + n_s].set(c)
    mods = _mods(conds, w_mod, b_mod)
    pool_w_fold = _pool_fold(pool_w_in, pool_w_grp)
    pool_band = _pool_band()

    yp = x_prompt.reshape(n_p * seq_p, d)
    ys = x_sample.reshape(n_s * seq_s, d)
    streams = ((seq_p, False), (seq_s, True))
    ctx_k = ctx_v = pending = None
    for i in range(DEPTH):
        kind, j = i % N_MIXERS, i // N_MIXERS
        ys_in = (yp, ys)
        outs = []
        if kind == 0:
            for x2d, (seq, per_sample) in zip(ys_in, streams):
                outs.append(_pool_layer(x2d, seq, per_sample, mods, i, j, pool_band, pool_w_fold, pool_w_in,
                                        pool_scale, pool_w_out, ln_g[i], ln_b[i],
                                        prev=pending if per_sample else None))
            pending = None
        elif kind == 1:
            w_in, w_s, w_out = sgu_w_in[j].astype(BF16), sgu_w_s[j].astype(BF16), sgu_w_out[j].astype(BF16)
            for x2d, (seq, per_sample) in zip(ys_in, streams):
                outs.append(_sgu_layer(x2d, seq, per_sample, mods, i, w_in, sgu_ln_g[j], sgu_ln_b[j], w_s,
                                       sgu_b_s[j].T, w_out, ln_g[i], ln_b[i]))
        else:
            w_in, w_out = na_w_in[j], na_w_out[j]
            out_p, kf, vf = _ctx_layer(yp, seq_p, mods, i, w_in, w_out, ln_g[i], ln_b[i])
            ctx_k = kf.reshape(n_p, 1, H_C, DH_C, seq_p).transpose(0, 1, 4, 2, 3)
            ctx_v = vf.reshape(n_p, 1, H_C, DH_C, seq_p).transpose(0, 1, 4, 2, 3)
            outs.append(out_p)
            q, k, v, sg = _na_proj(ys, seq_s, True, mods, i, w_in, DH_C ** -0.5 * LOG2_E)
            gated = _nbr_attn(q, k, v, sg, cache_k, cache_v, j, na_rpb[j], n_s)
            if i + 1 < DEPTH and (i + 1) % N_MIXERS == 0 and seq_s >= ROW_TILE:
                pending = (gated, i, w_out, ln_g[i], ln_b[i])
                outs.append(ys)
            else:
                outs.append(_na_out(ys, gated, seq_s, True, mods, i, w_out, ln_g[i], ln_b[i]))
        yp, ys = outs
    return (yp.reshape(n_p, seq_p, d), ys.reshape(n_s, seq_s, d), ctx_k, ctx_v)
```
